```python
import jax, jax.numpy as jnp
from jax import lax
import numpy as np

D_MODEL = 1024
BATCH = 8
SEQ = 4096
DEPTH = 2

RET_HEADS = 4
RET_DK = 128
RET_DV = 128
RET_CHUNK = 128
ROPE_BASE = 10000.0
RET_QK_W = RET_HEADS * RET_DK
RET_V_W = RET_HEADS * RET_DV
POOL_WINDOWS = (2, 4, 8, 16)
POOL_GROUPS = 4
POOL_GROUP_DIM = 128
POOL_W = POOL_GROUPS * POOL_GROUP_DIM
SSM_HEADS = 16
SSM_HEAD_DIM = 64
SSM_INNER = SSM_HEADS * SSM_HEAD_DIM
SSM_GROUPS = 2
SSM_STATE = 128
SSM_CONV = 4
SSM_CHUNK = 128
SSM_XBC = SSM_INNER + 2 * SSM_GROUPS * SSM_STATE
FOX_HEADS = 4
FOX_HEAD_DIM = 128
FOX_BLOCK = 128
FOX_W = FOX_HEADS * FOX_HEAD_DIM
N_BRANCH = 4
D_FF = 2816
N_SUBLAYERS = 3
RMS_EPS = 1e-6
GN_EPS = 1e-5
IN_SIZES = (RET_QK_W, RET_QK_W, RET_V_W, RET_V_W, POOL_W, SSM_INNER, SSM_XBC, SSM_HEADS, FOX_W, FOX_W, FOX_W, FOX_HEADS, N_BRANCH * D_MODEL)
IN_WIDTH = 2 * RET_QK_W + 2 * RET_V_W + POOL_W + SSM_INNER + SSM_XBC + SSM_HEADS + 3 * FOX_W + FOX_HEADS + N_BRANCH * D_MODEL

kernel_name = "hybrid_gated_ret_pool_ssd_fox_macaron"

F32 = jnp.float32


def rms_norm(x, gain):
    x32 = x.astype(F32)
    y = x32 * lax.rsqrt(jnp.mean(x32 * x32, axis=-1, keepdims=True) + RMS_EPS)
    return (y * gain.astype(F32)).astype(x.dtype)


def modulate(h, shift, scale):
    return h * (1 + scale[:, None, :]) + shift[:, None, :]


def swiglu_ffn(h, w_in, w_out):
    g, u = jnp.split(h @ w_in, 2, axis=-1)
    return (jax.nn.silu(g) * u) @ w_out


def rotary(x, pos):
    half = x.shape[-1] // 2
    inv = ROPE_BASE ** (-jnp.arange(half, dtype=F32) / half)
    ang = pos.astype(F32)[:, None] * inv[None, :]
    cos = jnp.cos(ang)[None, :, None, :]
    sin = jnp.sin(ang)[None, :, None, :]
    x1 = x[..., :half].astype(F32)
    x2 = x[..., half:].astype(F32)
    return jnp.concatenate([x1 * cos - x2 * sin, x1 * sin + x2 * cos], axis=-1)


def head_group_norm(o):
    mu = jnp.mean(o, axis=-1, keepdims=True)
    var = jnp.mean(jnp.square(o - mu), axis=-1, keepdims=True)
    return (o - mu) * lax.rsqrt(var + GN_EPS)


def retention_chunkwise(q, k, v):
    bsz, seq, nh, dk = q.shape
    dv = v.shape[-1]
    cl = RET_CHUNK
    n = seq // cl
    log_gamma = jnp.log1p(-jnp.exp2(-5.0 - jnp.arange(nh, dtype=F32)))
    idx = jnp.arange(cl, dtype=F32)
    rel = idx[:, None] - idx[None, :]
    intra_decay = jnp.where(rel >= 0, jnp.exp(log_gamma[:, None, None] * jnp.maximum(rel, 0.0)), 0.0)
    q_decay = jnp.exp(log_gamma[:, None] * (idx + 1.0))
    k_decay = jnp.exp(log_gamma[:, None] * (cl - 1.0 - idx))
    chunk_decay = jnp.exp(log_gamma * cl)

    def to_chunks(t):
        return t.reshape(bsz, n, cl, nh, t.shape[-1]).transpose(1, 0, 3, 2, 4)

    def step(state, inp):
        qc, kc, vc = inp
        scores = jnp.einsum('bhid,bhjd->bhij', qc, kc) * intra_decay
        o = jnp.einsum('bhij,bhje->bhie', scores, vc) + jnp.einsum('bhid,bhde->bhie', qc, state) * q_decay[:, :, None]
        state = chunk_decay[:, None, None] * state + jnp.einsum('bhjd,bhje->bhde', kc * k_decay[:, :, None], vc)
        return state, o

    s0 = jnp.zeros((bsz, nh, dk, dv), F32)
    _, o = lax.scan(step, s0, (to_chunks(q), to_chunks(k), to_chunks(v)))
    return o.transpose(1, 0, 3, 2, 4).reshape(bsz, seq, nh, dv)


def pool_mixer(u, w_pool, pool_scale):
    bsz, seq, _ = u.shape
    u32 = u.reshape(bsz, seq, POOL_GROUPS, POOL_GROUP_DIM).astype(F32)
    cs = jnp.pad(jnp.cumsum(u32, axis=1), ((0, 0), (1, 0), (0, 0), (0, 0)))
    t = jnp.arange(seq)
    win = jnp.array(POOL_WINDOWS, dtype=jnp.int32)
    lo = jnp.maximum(t[:, None] + 1 - win[None, :], 0)
    g_idx = jnp.arange(POOL_GROUPS)[None, :]
    lo_sum = cs[:, lo, g_idx]
    count = (t[:, None] + 1 - lo).astype(F32)
    pooled = (cs[:, 1:] - lo_sum) / count[None, :, :, None] - u32
    mixed = jnp.einsum('blgc,gcd->blgd', pooled.astype(u.dtype), w_pool)
    return mixed.reshape(bsz, seq, POOL_W) * pool_scale


def causal_depthwise_conv(u, w, b):
    out = lax.conv_general_dilated(u, w[:, None, :].astype(u.dtype), window_strides=(1,), padding=[(SSM_CONV - 1, 0)], dimension_numbers=('NWC', 'WIO', 'NWC'), feature_group_count=u.shape[-1])
    return out + b


def ssd_scan(x, dt, a, bm, cm):
    bsz, seq, nh, hp = x.shape
    ng = bm.shape[2]
    hg = nh // ng
    cl = SSM_CHUNK
    n = seq // cl
    da = dt * a
    xdt = x * dt[..., None]
    causal = jnp.tril(jnp.ones((cl, cl), dtype=bool))

    def chunks(t):
        return t.reshape((bsz, n, cl) + t.shape[2:]).swapaxes(0, 1)

    def step(state, inp):
        xc, dac, bc, cc = inp
        acum = jnp.cumsum(dac, axis=1)
        seg = acum[:, :, None, :] - acum[:, None, :, :]
        lmat = jnp.exp(jnp.where(causal[None, :, :, None], seg, -jnp.inf)).reshape(bsz, cl, cl, ng, hg)
        cb = jnp.einsum('bign,bjgn->bijg', cc, bc)
        xg = xc.reshape(bsz, cl, ng, hg, hp)
        y = jnp.einsum('bijgh,bjghp->bighp', lmat * cb[..., None], xg)
        acum_g = acum.reshape(bsz, cl, ng, hg)
        y = y + jnp.einsum('bign,bghpn->bighp', cc, state) * jnp.exp(acum_g)[..., None]
        decay_end = jnp.exp(acum_g[:, -1:] - acum_g)
        state = jnp.exp(acum_g[:, -1])[..., None, None] * state + jnp.einsum('bjgn,bjgh,bjghp->bghpn', bc, decay_end, xg)
        return state, y.reshape(bsz, cl, nh, hp)

    s0 = jnp.zeros((bsz, ng, hg, hp, bm.shape[-1]), F32)
    _, y = lax.scan(step, s0, (chunks(xdt), chunks(da), chunks(bm), chunks(cm)))
    return y.swapaxes(0, 1).reshape(bsz, seq, nh, hp)


def ssd_mixer(z, xbc, dt_raw, conv_w, conv_b, dt_bias, a_log, d_skip, norm_w):
    bsz, seq, _ = z.shape
    xbc = jax.nn.silu(causal_depthwise_conv(xbc, conv_w, conv_b))
    xs, bm, cm = jnp.split(xbc, [SSM_INNER, SSM_INNER + SSM_GROUPS * SSM_STATE], axis=-1)
    xs = xs.reshape(bsz, seq, SSM_HEADS, SSM_HEAD_DIM).astype(F32)
    bm = bm.reshape(bsz, seq, SSM_GROUPS, SSM_STATE).astype(F32)
    cm = cm.reshape(bsz, seq, SSM_GROUPS, SSM_STATE).astype(F32)
    dt = jax.nn.softplus(dt_raw.astype(F32) + dt_bias.astype(F32))
    a = -jnp.exp(a_log.astype(F32))
    y = ssd_scan(xs, dt, a, bm, cm) + d_skip.astype(F32)[:, None] * xs
    y = y.reshape(bsz, seq, SSM_INNER).astype(z.dtype)
    return rms_norm(y * jax.nn.silu(z), norm_w)


def forgetting_attention(q, k, v, log_f):
    bsz, seq, nh, hd = q.shape
    nb = seq // FOX_BLOCK
    cum = jnp.cumsum(log_f, axis=1).transpose(0, 2, 1)
    qh = (q.astype(F32) * (hd ** -0.5)).transpose(0, 2, 1, 3)
    kh = k.astype(F32).transpose(0, 2, 1, 3)
    vh = v.astype(F32).transpose(0, 2, 1, 3)
    qb = qh.reshape(bsz, nh, nb, FOX_BLOCK, hd).transpose(2, 0, 1, 3, 4)
    cb = cum.reshape(bsz, nh, nb, FOX_BLOCK).transpose(2, 0, 1, 3)
    kpos = jnp.arange(seq)

    def block(inp):
        qi, ci, i = inp
        s = jnp.einsum('bhqd,bhkd->bhqk', qi, kh) + ci[..., None] - cum[:, :, None, :]
        qpos = i * FOX_BLOCK + jnp.arange(FOX_BLOCK)
        s = jnp.where(kpos[None, :] <= qpos[:, None], s, -jnp.inf)
        p = jax.nn.softmax(s, axis=-1)
        return jnp.einsum('bhqk,bhkd->bhqd', p, vh)

    o = lax.map(block, (qb, cb, jnp.arange(nb)))
    return o.transpose(1, 0, 3, 2, 4).reshape(bsz, seq, nh * hd).astype(q.dtype)


def hybrid_mixer(h, w_in, b_forget, pool_w, pool_scale, conv_w, conv_b, dt_bias, a_log, d_skip, ssm_norm, w_ret_out, w_pool_out, w_ssm_out, w_fox_out, w_out):
    bsz, seq, _ = h.shape
    proj = h @ w_in
    points = [int(p) for p in np.cumsum(IN_SIZES)[:-1]]
    rq, rk, rv, rg, pu, sz, sxbc, sdt, fq, fk, fv, ff, gl = jnp.split(proj, points, axis=-1)
    pos = jnp.arange(seq)
    rq = rotary(rq.reshape(bsz, seq, RET_HEADS, RET_DK), pos)
    rk = rotary(rk.reshape(bsz, seq, RET_HEADS, RET_DK), pos) * (RET_DK ** -0.5)
    ret = retention_chunkwise(rq, rk, rv.reshape(bsz, seq, RET_HEADS, RET_DV).astype(F32))
    y_a = jax.nn.silu(rg) * head_group_norm(ret).reshape(bsz, seq, RET_V_W).astype(h.dtype)
    y_b = pool_mixer(pu, pool_w, pool_scale)
    y_c = ssd_mixer(sz, sxbc, sdt, conv_w, conv_b, dt_bias, a_log, d_skip, ssm_norm)
    log_f = jax.nn.log_sigmoid(ff.astype(F32) + b_forget.astype(F32))
    y_d = forgetting_attention(fq.reshape(bsz, seq, FOX_HEADS, FOX_HEAD_DIM), fk.reshape(bsz, seq, FOX_HEADS, FOX_HEAD_DIM), fv.reshape(bsz, seq, FOX_HEADS, FOX_HEAD_DIM), log_f)
    gates = jax.nn.sigmoid(gl.reshape(bsz, seq, N_BRANCH, D_MODEL))
    merged = (gates[:, :, 0] * (y_a @ w_ret_out) + gates[:, :, 1] * (y_b @ w_pool_out)
              + gates[:, :, 2] * (y_c @ w_ssm_out) + gates[:, :, 3] * (y_d @ w_fox_out))
    return merged @ w_out


def setup_inputs(seed: int = 0) -> dict:
    key = jax.random.key(seed)
    ks = jax.random.split(key, 24)
    nrm = jax.random.normal
    L, D = DEPTH, D_MODEL
    dt0 = jnp.exp(jax.random.uniform(ks[15], (L, SSM_HEADS), minval=np.log(1e-3), maxval=np.log(1e-1)))
    return {
        "x": nrm(ks[0], (BATCH, SEQ, D), F32),
        "c": nrm(ks[1], (BATCH, D), F32),
        "w_ada": nrm(ks[2], (L, D, 3 * N_SUBLAYERS * D), F32) * D ** -0.5,
        "b_ada": nrm(ks[3], (L, 3 * N_SUBLAYERS * D), F32) * 0.01,
        "norm_pre": 1.0 + 0.05 * nrm(ks[4], (L, N_SUBLAYERS, D), F32),
        "norm_post": 1.0 + 0.05 * nrm(ks[5], (L, N_SUBLAYERS, D), F32),
        "w_ffn_in": nrm(ks[6], (L, 2, D, 2 * D_FF), F32) * D ** -0.5,
        "w_ffn_out": nrm(ks[7], (L, 2, D_FF, D), F32) * D_FF ** -0.5,
        "w_in": nrm(ks[8], (L, D, IN_WIDTH), F32) * D ** -0.5,
        "b_forget": 2.0 + 0.5 * nrm(ks[9], (L, FOX_HEADS), F32),
        "pool_w": nrm(ks[10], (L, POOL_GROUPS, POOL_GROUP_DIM, POOL_GROUP_DIM), F32) * POOL_GROUP_DIM ** -0.5,
        "pool_scale": 1.0 + 0.1 * nrm(ks[11], (L, POOL_W), F32),
        "conv_w": nrm(ks[12], (L, SSM_CONV, SSM_XBC), F32) * SSM_CONV ** -0.5,
        "conv_b": nrm(ks[13], (L, SSM_XBC), F32) * 0.01,
        "dt_bias": dt0 + jnp.log(-jnp.expm1(-dt0)),
        "a_log": jnp.log(jax.random.uniform(ks[14], (L, SSM_HEADS), minval=1.0, maxval=16.0)),
        "d_skip": 1.0 + 0.1 * nrm(ks[16], (L, SSM_HEADS), F32),
        "ssm_norm": 1.0 + 0.05 * nrm(ks[17], (L, SSM_INNER), F32),
        "w_ret_out": nrm(ks[18], (L, RET_V_W, D), F32) * RET_V_W ** -0.5,
        "w_pool_out": nrm(ks[19], (L, POOL_W, D), F32) * POOL_W ** -0.5,
        "w_ssm_out": nrm(ks[20], (L, SSM_INNER, D), F32) * SSM_INNER ** -0.5,
        "w_fox_out": nrm(ks[21], (L, FOX_W, D), F32) * FOX_W ** -0.5,
        "w_out": nrm(ks[22], (L, D, D), F32) * D ** -0.5,
    }


def reference(x, c, w_ada, b_ada, norm_pre, norm_post, w_ffn_in, w_ffn_out, w_in, b_forget, pool_w, pool_scale, conv_w, conv_b, dt_bias, a_log, d_skip, ssm_norm, w_ret_out, w_pool_out, w_ssm_out, w_fox_out, w_out):
    for i in range(DEPTH):
        mod = jax.nn.silu(c) @ w_ada[i] + b_ada[i]
        sh1, sc1, g1, sh2, sc2, g2, sh3, sc3, g3 = jnp.split(mod, 3 * N_SUBLAYERS, axis=-1)
        h = modulate(rms_norm(x, norm_pre[i, 0]), sh1, sc1)
        y = swiglu_ffn(h, w_ffn_in[i, 0], w_ffn_out[i, 0])
        x = x + 0.5 * g1[:, None, :] * rms_norm(y, norm_post[i, 0])
        h = modulate(rms_norm(x, norm_pre[i, 1]), sh2, sc2)
        y = hybrid_mixer(h, w_in[i], b_forget[i], pool_w[i], pool_scale[i], conv_w[i], conv_b[i], dt_bias[i], a_log[i], d_skip[i], ssm_norm[i], w_ret_out[i], w_pool_out[i], w_ssm_out[i], w_fox_out[i], w_out[i])
        x = x + g2[:, None, :] * rms_norm(y, norm_post[i, 1])
        h = modulate(rms_norm(x, norm_pre[i, 2]), sh3, sc3)
        y = swiglu_ffn(h, w_ffn_in[i, 1], w_ffn_out[i, 1])
        x = x + 0.5 * g3[:, None, :] * rms_norm(y, norm_post[i, 2])
    return x
```

```python
import functools

import jax
import jax.numpy as jnp
import numpy as np
from jax import lax
from jax.experimental import pallas as pl
from jax.experimental.pallas import tpu as pltpu

F32 = jnp.float32
BF16 = jnp.bfloat16

D_MODEL = 1024
RET_HEADS = 4
RET_DK = 128
RET_DV = 128
RET_W = RET_HEADS * RET_DK
ROPE_BASE = 10000.0
POOL_WINDOWS = (2, 4, 8, 16)
POOL_GROUPS = 4
POOL_GROUP_DIM = 128
POOL_W = POOL_GROUPS * POOL_GROUP_DIM
SSM_HEADS = 16
SSM_HEAD_DIM = 64
SSM_INNER = SSM_HEADS * SSM_HEAD_DIM
SSM_GROUPS = 2
SSM_STATE = 128
SSM_CONV = 4
SSM_XBC = SSM_INNER + 2 * SSM_GROUPS * SSM_STATE
FOX_HEADS = 4
FOX_HEAD_DIM = 128
FOX_W = FOX_HEADS * FOX_HEAD_DIM
N_BRANCH = 4
D_FF = 2816
N_SUBLAYERS = 3
RMS_EPS = 1e-6
GN_EPS = 1e-5
IN_SIZES = (RET_W, RET_W, RET_W, RET_W, POOL_W, SSM_INNER, SSM_XBC, SSM_HEADS,
            FOX_W, FOX_W, FOX_W, FOX_HEADS, N_BRANCH * D_MODEL)

LANES = 128
VMEM_LIMIT = 56 * 1024 * 1024
ADA_TN = 1152
TOK_TILE = 512
FFN_CHUNK = D_FF // 2
SEQ_TILE = 512
RET_CHUNK = 256
SSD_CHUNK = 128
FOX_TILE = 512
NEG_BIG = -1e30

SC_DT = 0
SC_ACUM = 16
SC_CUMF = 32
SC_ROWS_T = 40


def _sigmoid(v):
    return 1.0 / (1.0 + jnp.exp(-v))


def _silu(v):
    return v * _sigmoid(v)


def _rms(v, gain):
    return v * lax.rsqrt(jnp.mean(v * v, axis=-1, keepdims=True) + RMS_EPS) * gain


def _modulated(x, mod_ref, gpre_ref, sub):
    shift = mod_ref[0, 3 * sub:3 * sub + 1, :]
    scale = mod_ref[0, 3 * sub + 1:3 * sub + 2, :]
    return _rms(x, gpre_ref[sub:sub + 1, :]) * (1.0 + scale) + shift


def _dot(a, b):
    return jnp.dot(a, b, preferred_element_type=F32)


def _dot_nt(a, b):
    return lax.dot_general(a, b, (((1,), (1,)), ((), ())), preferred_element_type=F32)


def _dot_tn(a, b):
    return lax.dot_general(a, b, (((0,), (0,)), ((), ())), preferred_element_type=F32)


def _resident(shape):
    nd = len(shape)
    return pl.BlockSpec(shape, lambda *_: (0,) * nd, pipeline_mode=pl.Buffered(1))


def _params(n_axes):
    return pltpu.CompilerParams(dimension_semantics=("arbitrary",) * n_axes,
                                vmem_limit_bytes=VMEM_LIMIT)


def _ada_kernel(c_ref, w_ref, b_ref, o_ref):
    sc = _silu(c_ref[...]).astype(BF16)
    o_ref[0] = _dot(sc, w_ref[0].astype(BF16)) + b_ref[0]


def _ada_call(c, w_ada, b_ada):
    depth, d, width = w_ada.shape
    bsz = c.shape[0]
    return pl.pallas_call(
        _ada_kernel,
        grid=(depth, width // ADA_TN),
        in_specs=[pl.BlockSpec((bsz, d), lambda l, n: (0, 0)),
                  pl.BlockSpec((1, d, ADA_TN), lambda l, n: (l, 0, n)),
                  pl.BlockSpec((1, 1, ADA_TN), lambda l, n: (l, 0, n))],
        out_specs=pl.BlockSpec((1, bsz, ADA_TN), lambda l, n: (l, 0, n)),
        out_shape=jax.ShapeDtypeStruct((depth, bsz, width), F32),
        compiler_params=_params(2),
        name="adaln",
    )(c, w_ada, b_ada.reshape(depth, 1, width))


def _ffn_kernel(x_ref, mod_ref, gpre_ref, gpost_ref, win_ref, wout_ref, o_ref, *, sub):
    x = x_ref[...]
    hb = _modulated(x, mod_ref, gpre_ref, sub).astype(BF16)
    y = None
    for lo in range(0, D_FF, FFN_CHUNK):
        g = _dot(hb, win_ref[:, lo:lo + FFN_CHUNK])
        u = _dot(hb, win_ref[:, D_FF + lo:D_FF + lo + FFN_CHUNK])
        part = _dot((_silu(g) * u).astype(BF16), wout_ref[lo:lo + FFN_CHUNK, :])
        y = part if y is None else y + part
    gate = mod_ref[0, 3 * sub + 2:3 * sub + 3, :]
    o_ref[...] = x + (0.5 * gate) * _rms(y, gpost_ref[sub:sub + 1, :])


def _ffn_call(x2d, mod, gpre, gpost, w_in_b, w_out_b, sub, seq):
    n, d = x2d.shape
    tiles_per_seq = seq // TOK_TILE
    return pl.pallas_call(
        functools.partial(_ffn_kernel, sub=sub),
        grid=(n // TOK_TILE,),
        in_specs=[pl.BlockSpec((TOK_TILE, d), lambda i: (i, 0)),
                  pl.BlockSpec((1, 3 * N_SUBLAYERS, d), lambda i: (i // tiles_per_seq, 0, 0)),
                  _resident(gpre.shape), _resident(gpost.shape),
                  _resident(w_in_b.shape), _resident(w_out_b.shape)],
        out_specs=pl.BlockSpec((TOK_TILE, d), lambda i: (i, 0)),
        out_shape=jax.ShapeDtypeStruct((n, d), F32),
        compiler_params=_params(1),
        name="ffn",
    )(x2d, mod, gpre, gpost, w_in_b, w_out_b)


PJ_RQ = 0
PJ_RK = PJ_RQ + RET_W
PJ_RV = PJ_RK + RET_W
PJ_PU = PJ_RV + RET_W
PJ_XBC = PJ_PU + POOL_W
PJ_FQ = PJ_XBC + SSM_XBC
PJ_FK = PJ_FQ + FOX_W
PJ_FV = PJ_FK + FOX_W
PJ_SC = PJ_FV + FOX_W
PJ_WIDTH = PJ_SC + LANES
POOL_HALO = 16
CONV_HALO = 8


def _cumsum_rows(tril_b, s):
    p1 = s.astype(BF16)
    r1 = s - p1.astype(F32)
    p2 = r1.astype(BF16)
    p3 = (r1 - p2.astype(F32)).astype(BF16)
    return _dot(tril_b, p1) + _dot(tril_b, p2) + _dot(tril_b, p3)


def _proj_kernel(x_ref, mod_ref, gpre_ref, w_ref, cos_ref, sin_ref, poolw_ref, pscale_ref,
                 convw_ref, convb_ref, sbias_ref, alog_ref, tril_ref,
                 rq_ref, rk_ref, rv_ref, yb_ref, xbc_ref, fq_ref, fk_ref, fv_ref, scal_ref,
                 pbuf, cbuf, carry):
    j = pl.program_id(1)
    tm = x_ref.shape[1]
    hb = _modulated(x_ref[0], mod_ref, gpre_ref, 1).astype(BF16)

    cos = cos_ref[...]
    sin = sin_ref[...]
    for base, ref, scl in ((PJ_RQ, rq_ref, None), (PJ_RK, rk_ref, RET_DK ** -0.5)):
        t = _dot(hb, w_ref[:, base:base + RET_W])
        for hd in range(RET_HEADS):
            th = t[:, hd * RET_DK:(hd + 1) * RET_DK]
            r = th * cos + pltpu.roll(th, RET_DK // 2, 1) * sin
            if scl is not None:
                r = r * scl
            ref[0, :, hd * RET_DK:(hd + 1) * RET_DK] = r.astype(BF16)
    rv_ref[0] = _dot(hb, w_ref[:, PJ_RV:PJ_RV + RET_W]).astype(BF16)

    pu = _dot(hb, w_ref[:, PJ_PU:PJ_PU + POOL_W])

    @pl.when(j == 0)
    def _():
        pbuf[0:POOL_HALO, :] = jnp.zeros((POOL_HALO, POOL_W), F32)
        cbuf[0:CONV_HALO, :] = jnp.zeros((CONV_HALO, SSM_XBC), F32)
        carry[...] = jnp.zeros_like(carry)

    @pl.when(j > 0)
    def _():
        pbuf[0:POOL_HALO, :] = pbuf[tm:tm + POOL_HALO, :]
        cbuf[0:CONV_HALO, :] = cbuf[tm:tm + CONV_HALO, :]

    pbuf[POOL_HALO:POOL_HALO + tm, :] = pu
    pos = j * tm + lax.broadcasted_iota(jnp.int32, (tm, POOL_GROUP_DIM), 0)
    for g, win in enumerate(POOL_WINDOWS):
        ls = slice(g * POOL_GROUP_DIM, (g + 1) * POOL_GROUP_DIM)
        cur = pu[:, ls]
        acc = cur
        for s in range(1, win):
            acc = acc + pbuf[POOL_HALO - s:POOL_HALO - s + tm, ls]
        count = jnp.minimum(pos + 1, win).astype(F32)
        pooled = acc / count - cur
        mixed = _dot(pooled.astype(BF16), poolw_ref[g]) * pscale_ref[:, ls]
        yb_ref[0, :, ls] = mixed.astype(BF16)

    xr = _dot(hb, w_ref[:, PJ_XBC:PJ_XBC + SSM_XBC])
    cbuf[CONV_HALO:CONV_HALO + tm, :] = xr
    conv = xr * convw_ref[SSM_CONV - 1:SSM_CONV, :] + convb_ref[...]
    for k in range(SSM_CONV - 1):
        off = CONV_HALO - (SSM_CONV - 1) + k
        conv = conv + cbuf[off:off + tm, :] * convw_ref[k:k + 1, :]
    xbc_ref[0] = _silu(conv).astype(BF16)

    fq_ref[0] = (_dot(hb, w_ref[:, PJ_FQ:PJ_FQ + FOX_W]) * (FOX_HEAD_DIM ** -0.5)).astype(BF16)
    fk_ref[0] = _dot(hb, w_ref[:, PJ_FK:PJ_FK + FOX_W]).astype(BF16)
    fv_ref[0] = _dot(hb, w_ref[:, PJ_FV:PJ_FV + FOX_W]).astype(BF16)

    z = _dot(hb, w_ref[:, PJ_SC:PJ_SC + LANES]) + sbias_ref[...]
    tail = jnp.log1p(jnp.exp(-jnp.abs(z)))
    softplus = jnp.maximum(z, 0.0) + tail
    log_sig = jnp.minimum(z, 0.0) - tail
    lane = lax.broadcasted_iota(jnp.int32, (tm, LANES), 1)
    a_row = -jnp.exp(alog_ref[...])
    in_acum = (lane >= SC_ACUM) & (lane < SC_CUMF)
    in_cumf = (lane >= SC_CUMF) & (lane < SC_CUMF + FOX_HEADS)
    src = jnp.where(in_acum, softplus * a_row, jnp.where(in_cumf, log_sig, 0.0))
    tril_b = tril_ref[...]
    run = carry[...]
    lane_c = lax.broadcasted_iota(jnp.int32, (SSD_CHUNK, LANES), 1)
    for c0 in range(0, tm, SSD_CHUNK):
        rows = slice(c0, c0 + SSD_CHUNK)
        local = _cumsum_rows(tril_b, src[rows, :])
        total = local + run
        scal_ref[0, rows, :] = jnp.where(lane_c < SC_ACUM, softplus[rows, :],
                                         jnp.where(lane_c < SC_CUMF, local, total))
        run = total[SSD_CHUNK - 1:SSD_CHUNK, :]
    carry[...] = run


def _proj_call(x, mod, gpre, w_b, cos_t, sin_t, poolw_b, pscale, convw, convb, sbias, alog, tril_b):
    bsz, seq, d = x.shape
    tm = TOK_TILE
    tok = lambda w: pl.BlockSpec((1, tm, w), lambda b, j: (b, j, 0))
    bf = lambda w: jax.ShapeDtypeStruct((bsz, seq, w), BF16)
    return pl.pallas_call(
        _proj_kernel,
        grid=(bsz, seq // tm),
        in_specs=[tok(d),
                  pl.BlockSpec((1, 3 * N_SUBLAYERS, d), lambda b, j: (b, 0, 0)),
                  _resident(gpre.shape), _resident(w_b.shape),
                  pl.BlockSpec((tm, RET_DK), lambda b, j: (j, 0)),
                  pl.BlockSpec((tm, RET_DK), lambda b, j: (j, 0)),
                  _resident(poolw_b.shape), _resident(pscale.shape),
                  _resident(convw.shape), _resident(convb.shape),
                  _resident(sbias.shape), _resident(alog.shape), _resident(tril_b.shape)],
        out_specs=[tok(RET_W), tok(RET_W), tok(RET_W), tok(POOL_W), tok(SSM_XBC),
                   tok(FOX_W), tok(FOX_W), tok(FOX_W), tok(LANES)],
        out_shape=[bf(RET_W), bf(RET_W), bf(RET_W), bf(POOL_W), bf(SSM_XBC),
                   bf(FOX_W), bf(FOX_W), bf(FOX_W),
                   jax.ShapeDtypeStruct((bsz, seq, LANES), F32)],
        scratch_shapes=[pltpu.VMEM((POOL_HALO + tm, POOL_W), F32),
                        pltpu.VMEM((CONV_HALO + tm, SSM_XBC), F32),
                        pltpu.VMEM((1, LANES), F32)],
        compiler_params=_params(2),
        name="mixer_proj",
    )(x, mod, gpre, w_b, cos_t, sin_t, poolw_b, pscale, convw, convb, sbias, alog, tril_b)


def _ret_kernel(q_ref, k_ref, v_ref, idec_ref, qdec_ref, kdec_ref, o_ref, state, *, chunk_decay):
    @pl.when(pl.program_id(1) == 0)
    def _():
        state[...] = jnp.zeros_like(state)

    tile = q_ref.shape[1]
    for hd in range(RET_HEADS):
        ls = slice(hd * RET_DK, (hd + 1) * RET_DK)
        for c0 in range(0, tile, RET_CHUNK):
            rows = slice(c0, c0 + RET_CHUNK)
            q = q_ref[0, rows, ls]
            k = k_ref[0, rows, ls]
            v = v_ref[0, rows, ls]
            st = state[hd]
            scores = _dot_nt(q, k) * idec_ref[hd]
            o = _dot(scores.astype(BF16), v) + _dot(q, st.astype(BF16)) * qdec_ref[hd]
            kd = (k.astype(F32) * kdec_ref[hd]).astype(BF16)
            state[hd] = chunk_decay[hd] * st + _dot_tn(kd, v)
            dev = o - jnp.mean(o, axis=-1, keepdims=True)
            var = jnp.mean(dev * dev, axis=-1, keepdims=True)
            o_ref[0, rows, ls] = (dev * lax.rsqrt(var + GN_EPS)).astype(BF16)


def _ret_tables():
    log_gamma = np.log1p(-np.exp2(-5.0 - np.arange(RET_HEADS, dtype=np.float64)))
    idx = np.arange(RET_CHUNK, dtype=np.float64)
    rel = idx[:, None] - idx[None, :]
    intra = np.where(rel >= 0, np.exp(log_gamma[:, None, None] * np.maximum(rel, 0.0)), 0.0)
    q_decay = np.exp(log_gamma[:, None] * (idx + 1.0))
    k_decay = np.exp(log_gamma[:, None] * (RET_CHUNK - 1.0 - idx))
    widen = lambda t: np.broadcast_to(t[:, :, None], (RET_HEADS, RET_CHUNK, RET_DK))
    chunk_decay = tuple(float(v) for v in np.exp(log_gamma * RET_CHUNK))
    return (jnp.asarray(intra, F32), jnp.asarray(widen(q_decay), F32),
            jnp.asarray(widen(k_decay), F32), chunk_decay)


def _ret_call(rq, rk, rv):
    bsz, seq, w = rq.shape
    idec, qdec, kdec, chunk_decay = _ret_tables()
    tok = pl.BlockSpec((1, SEQ_TILE, w), lambda b, j: (b, j, 0))
    return pl.pallas_call(
        functools.partial(_ret_kernel, chunk_decay=chunk_decay),
        grid=(bsz, seq // SEQ_TILE),
        in_specs=[tok, tok, tok, _resident(idec.shape), _resident(qdec.shape), _resident(kdec.shape)],
        out_specs=tok,
        out_shape=jax.ShapeDtypeStruct((bsz, seq, w), BF16),
        scratch_shapes=[pltpu.VMEM((RET_HEADS, RET_DK, RET_DV), F32)],
        compiler_params=_params(2),
        name="retention",
    )(rq, rk, rv, idec, qdec, kdec)


HEADS_PER_GROUP = SSM_HEADS // SSM_GROUPS
GROUP_W = HEADS_PER_GROUP * SSM_HEAD_DIM


def _ssd_kernel(xbc_ref, scal_ref, scalT_ref, ea_sel_ref, dt_sel_ref, dskip_ref, o_ref, state):
    @pl.when(pl.program_id(1) == 0)
    def _():
        state[...] = jnp.zeros_like(state)

    tile = xbc_ref.shape[1]
    cl = SSD_CHUNK
    lane = lax.broadcasted_iota(jnp.int32, (cl, LANES), 1)
    row_i = lax.broadcasted_iota(jnp.int32, (cl, cl), 0)
    col_i = lax.broadcasted_iota(jnp.int32, (cl, cl), 1)
    causal = col_i <= row_i
    low_half = lane < SSM_HEAD_DIM
    ea_sel = ea_sel_ref[...]
    dt_sel = dt_sel_ref[...]
    for c0 in range(0, tile, cl):
        rows = slice(c0, c0 + cl)
        sc = scal_ref[0, rows, :]
        in_acum = (lane >= SC_ACUM) & (lane < SC_CUMF)
        acum = jnp.where(in_acum, sc, 0.0)
        ea = jnp.exp(acum)
        dec_end = jnp.exp(acum[cl - 1:cl, :] - acum)
        dt = jnp.where(lane < SC_ACUM, sc, 0.0)
        ea_hi = ea.astype(BF16)
        ea_lo = (ea - ea_hi.astype(F32)).astype(BF16)
        ea_x = _dot(ea_hi, ea_sel) + _dot(ea_lo, ea_sel)
        w_x = _dot(dec_end.astype(BF16), ea_sel) * _dot(dt.astype(BF16), dt_sel)
        xs_b = xbc_ref[0, rows, 0:SSM_INNER]
        xs = xs_b.astype(F32)
        xw_b = (xs * w_x).astype(BF16)
        for g in range(SSM_GROUPS):
            gs = slice(g * GROUP_W, (g + 1) * GROUP_W)
            bm = xbc_ref[0, rows, SSM_INNER + g * SSM_STATE:SSM_INNER + (g + 1) * SSM_STATE]
            cm_lo = SSM_INNER + SSM_GROUPS * SSM_STATE + g * SSM_STATE
            cm = xbc_ref[0, rows, cm_lo:cm_lo + SSM_STATE]
            cb = _dot_nt(cm, bm)
            st = state[g]
            y_inter = _dot(cm, st.astype(BF16)) * ea_x[:, gs]
            state[g] = st * ea_x[cl - 1:cl, gs] + _dot_tn(bm, xw_b[:, gs])
            for pair in range(HEADS_PER_GROUP // 2):
                mats = []
                for hh in (2 * pair, 2 * pair + 1):
                    hd = g * HEADS_PER_GROUP + hh
                    a_col = jnp.sum(jnp.where(lane == SC_ACUM + hd, sc, 0.0), axis=-1, keepdims=True)
                    a_row = scalT_ref[0, SC_ACUM + hd:SC_ACUM + hd + 1, rows]
                    dt_row = scalT_ref[0, SC_DT + hd:SC_DT + hd + 1, rows]
                    lmat = jnp.exp(jnp.where(causal, a_col - a_row, NEG_BIG))
                    mats.append((lmat * dt_row * cb).astype(BF16))
                lo = g * GROUP_W + pair * LANES
                x_pair = xs_b[:, lo:lo + LANES]
                zero = jnp.zeros_like(x_pair)
                rhs = jnp.concatenate([jnp.where(low_half, x_pair, zero),
                                       jnp.where(low_half, zero, x_pair)], axis=0)
                y_pair = _dot(jnp.concatenate(mats, axis=1), rhs)
                y_pair = y_pair + y_inter[:, pair * LANES:(pair + 1) * LANES]
                y_pair = y_pair + dskip_ref[:, lo:lo + LANES] * xs[:, lo:lo + LANES]
                o_ref[0, rows, lo:lo + LANES] = y_pair.astype(BF16)


def _ssd_tables():
    ea_sel = np.zeros((LANES, SSM_INNER), np.float32)
    dt_sel = np.zeros((LANES, SSM_INNER), np.float32)
    for hd in range(SSM_HEADS):
        ea_sel[SC_ACUM + hd, hd * SSM_HEAD_DIM:(hd + 1) * SSM_HEAD_DIM] = 1.0
        dt_sel[SC_DT + hd, hd * SSM_HEAD_DIM:(hd + 1) * SSM_HEAD_DIM] = 1.0
    return jnp.asarray(ea_sel, BF16), jnp.asarray(dt_sel, BF16)


def _ssd_call(xbc, scal, scal_t, dskip_x):
    bsz, seq, _ = xbc.shape
    ea_sel, dt_sel = _ssd_tables()
    return pl.pallas_call(
        _ssd_kernel,
        grid=(bsz, seq // SEQ_TILE),
        in_specs=[pl.BlockSpec((1, SEQ_TILE, SSM_XBC), lambda b, j: (b, j, 0)),
                  pl.BlockSpec((1, SEQ_TILE, LANES), lambda b, j: (b, j, 0)),
                  pl.BlockSpec((1, SC_ROWS_T, SEQ_TILE), lambda b, j: (b, 0, j)),
                  _resident(ea_sel.shape), _resident(dt_sel.shape), _resident(dskip_x.shape)],
        out_specs=pl.BlockSpec((1, SEQ_TILE, SSM_INNER), lambda b, j: (b, j, 0)),
        out_shape=jax.ShapeDtypeStruct((bsz, seq, SSM_INNER), BF16),
        scratch_shapes=[pltpu.VMEM((SSM_GROUPS, SSM_STATE, GROUP_W), F32)],
        compiler_params=_params(2),
        name="ssd",
    )(xbc, scal, scal_t, ea_sel, dt_sel, dskip_x)


FOX_CUM_ROWS = 8


def _fox_kernel(q_ref, k_ref, v_ref, scal_ref, scalT_ref, o_ref, m_s, l_s, acc_s):
    hd = pl.program_id(1)
    i = pl.program_id(2)
    tq = q_ref.shape[1]
    q = q_ref[0]
    lane = lax.broadcasted_iota(jnp.int32, (tq, LANES), 1)
    cq = jnp.sum(jnp.where(lane == SC_CUMF + hd, scal_ref[0], 0.0), axis=-1, keepdims=True)
    m_s[...] = jnp.full_like(m_s, NEG_BIG)
    l_s[...] = jnp.zeros_like(l_s)
    acc_s[...] = jnp.zeros_like(acc_s)

    def step(jt, masked):
        ks = pl.multiple_of(jt * tq, tq)
        k = k_ref[0, pl.ds(ks, tq), :]
        v = v_ref[0, pl.ds(ks, tq), :]
        ck = scalT_ref[0, pl.ds(hd, 1), pl.ds(ks, tq)]
        s = _dot_nt(q, k) + (cq - ck)
        if masked:
            row_i = lax.broadcasted_iota(jnp.int32, (tq, tq), 0)
            col_i = lax.broadcasted_iota(jnp.int32, (tq, tq), 1)
            s = jnp.where(col_i <= row_i, s, NEG_BIG)
        m_old = m_s[...]
        m_new = jnp.maximum(m_old, jnp.max(s, axis=-1, keepdims=True))
        p = jnp.exp(s - m_new)
        alpha = jnp.exp(m_old - m_new)
        l_s[...] = alpha * l_s[...] + jnp.sum(p, axis=-1, keepdims=True)
        acc_s[...] = alpha * acc_s[...] + _dot(p.astype(BF16), v)
        m_s[...] = m_new

    def body(jt, carry):
        step(jt, False)
        return carry

    lax.fori_loop(0, i, body, 0)
    step(i, True)
    o_ref[0] = (acc_s[...] / l_s[...]).astype(BF16)


def _fox_call(fq, fk, fv, scal, scal_t):
    bsz, seq, _ = fq.shape
    tq = FOX_TILE
    cum_block = SC_CUMF // FOX_CUM_ROWS
    return pl.pallas_call(
        _fox_kernel,
        grid=(bsz, FOX_HEADS, seq // tq),
        in_specs=[pl.BlockSpec((1, tq, FOX_HEAD_DIM), lambda b, h, i: (b, i, h)),
                  pl.BlockSpec((1, seq, FOX_HEAD_DIM), lambda b, h, i: (b, 0, h)),
                  pl.BlockSpec((1, seq, FOX_HEAD_DIM), lambda b, h, i: (b, 0, h)),
                  pl.BlockSpec((1, tq, LANES), lambda b, h, i: (b, i, 0)),
                  pl.BlockSpec((1, FOX_CUM_ROWS, seq), lambda b, h, i: (b, cum_block, 0))],
        out_specs=pl.BlockSpec((1, tq, FOX_HEAD_DIM), lambda b, h, i: (b, i, h)),
        out_shape=jax.ShapeDtypeStruct((bsz, seq, FOX_W), BF16),
        scratch_shapes=[pltpu.VMEM((tq, 1), F32), pltpu.VMEM((tq, 1), F32),
                        pltpu.VMEM((tq, FOX_HEAD_DIM), F32)],
        compiler_params=_params(3),
        name="fox_attention",
    )(fq, fk, fv, scal, scal_t)


MG_RG = 0
MG_SZ = MG_RG + RET_W
MG_GL = MG_SZ + SSM_INNER
MG_WIDTH = MG_GL + N_BRANCH * D_MODEL


def _merge_kernel(x_ref, mod_ref, gpre_ref, gpost_ref, wg_ref, ret_ref, yb_ref, ssd_ref, fox_ref,
                  ssmnorm_ref, wro_ref, wpo_ref, wso_ref, wfo_ref, wout_ref, o_ref):
    x = x_ref[...]
    hb = _modulated(x, mod_ref, gpre_ref, 1).astype(BF16)
    rg = _dot(hb, wg_ref[:, MG_RG:MG_RG + RET_W])
    y_a = (_silu(rg) * ret_ref[...].astype(F32)).astype(BF16)
    sz = _dot(hb, wg_ref[:, MG_SZ:MG_SZ + SSM_INNER])
    y_c = _rms(ssd_ref[...].astype(F32) * _silu(sz), ssmnorm_ref[...]).astype(BF16)
    merged = None
    for br, (y, w_ref) in enumerate(((y_a, wro_ref), (yb_ref[...], wpo_ref),
                                     (y_c, wso_ref), (fox_ref[...], wfo_ref))):
        lo = MG_GL + br * D_MODEL
        gate = _sigmoid(_dot(hb, wg_ref[:, lo:lo + D_MODEL]))
        part = gate * _dot(y, w_ref[...])
        merged = part if merged is None else merged + part
    y_out = _dot(merged.astype(BF16), wout_ref[...])
    o_ref[...] = x + mod_ref[0, 5:6, :] * _rms(y_out, gpost_ref[1:2, :])


def _merge_call(x2d, mod, gpre, gpost, wg_b, ret, yb, ssd, fox, ssmnorm, wro, wpo, wso, wfo, wout, seq):
    n, d = x2d.shape
    tm = TOK_TILE
    tiles_per_seq = seq // tm
    tok = lambda w: pl.BlockSpec((tm, w), lambda i: (i, 0))
    return pl.pallas_call(
        _merge_kernel,
        grid=(n // tm,),
        in_specs=[tok(d),
                  pl.BlockSpec((1, 3 * N_SUBLAYERS, d), lambda i: (i // tiles_per_seq, 0, 0)),
                  _resident(gpre.shape), _resident(gpost.shape), _resident(wg_b.shape),
                  tok(RET_W), tok(POOL_W), tok(SSM_INNER), tok(FOX_W),
                  _resident(ssmnorm.shape), _resident(wro.shape), _resident(wpo.shape),
                  _resident(wso.shape), _resident(wfo.shape), _resident(wout.shape)],
        out_specs=tok(d),
        out_shape=jax.ShapeDtypeStruct((n, d), F32),
        compiler_params=_params(1),
        name="mixer_merge",
    )(x2d, mod, gpre, gpost, wg_b, ret.reshape(n, -1), yb.reshape(n, -1), ssd.reshape(n, -1),
      fox.reshape(n, -1), ssmnorm, wro, wpo, wso, wfo, wout)


def _rotary_tables(seq):
    half = RET_DK // 2
    inv = ROPE_BASE ** (-jnp.arange(half, dtype=F32) / half)
    ang = jnp.arange(seq, dtype=F32)[:, None] * inv[None, :]
    cos = jnp.cos(ang)
    sin = jnp.sin(ang)
    return jnp.concatenate([cos, cos], axis=-1), jnp.concatenate([-sin, sin], axis=-1)


def _place(vals, lane0):
    return jnp.zeros((1, LANES), F32).at[0, lane0:lane0 + vals.shape[0]].set(vals.astype(F32))


def kernel(x, c, w_ada, b_ada, norm_pre, norm_post, w_ffn_in, w_ffn_out, w_in, b_forget, pool_w, pool_scale, conv_w, conv_b, dt_bias, a_log, d_skip, ssm_norm, w_ret_out, w_pool_out, w_ssm_out, w_fox_out, w_out):
    bsz, seq, d = x.shape
    depth = w_ada.shape[0]
    n = bsz * seq
    mods = _ada_call(c, w_ada, b_ada)
    cos_t, sin_t = _rotary_tables(seq)
    tril_b = jnp.asarray(np.tril(np.ones((SSD_CHUNK, SSD_CHUNK), np.float32)), BF16)
    pts = [int(p) for p in np.cumsum(IN_SIZES)[:-1]]
    for i in range(depth):
        mod = mods[i].reshape(bsz, 3 * N_SUBLAYERS, d)
        gpre, gpost = norm_pre[i], norm_post[i]
        x2d = _ffn_call(x.reshape(n, d), mod, gpre, gpost, w_ffn_in[i, 0].astype(BF16),
                        w_ffn_out[i, 0].astype(BF16), 0, seq)

        (w_rq, w_rk, w_rv, w_rg, w_pu, w_sz, w_xbc, w_dt, w_fq, w_fk, w_fv, w_ff, w_gl) = jnp.split(w_in[i], pts, axis=-1)
        w_sc = jnp.concatenate([w_dt, w_dt, w_ff, jnp.zeros((d, LANES - 2 * SSM_HEADS - FOX_HEADS), F32)], axis=-1)
        w_proj = jnp.concatenate([w_rq, w_rk, w_rv, w_pu, w_xbc, w_fq, w_fk, w_fv, w_sc], axis=-1).astype(BF16)
        w_gate = jnp.concatenate([w_rg, w_sz, w_gl], axis=-1).astype(BF16)
        sbias = _place(dt_bias[i], SC_DT) + _place(dt_bias[i], SC_ACUM) + _place(b_forget[i], SC_CUMF)
        alog = _place(a_log[i], SC_ACUM)
        dskip_x = jnp.repeat(d_skip[i].astype(F32), SSM_HEAD_DIM)[None, :]

        rq, rk, rv, yb, xbc, fq, fk, fv, scal = _proj_call(
            x2d.reshape(bsz, seq, d), mod, gpre, w_proj, cos_t, sin_t, pool_w[i].astype(BF16),
            pool_scale[i][None, :], conv_w[i], conv_b[i][None, :], sbias, alog, tril_b)
        scal_t = jnp.swapaxes(scal[:, :, :SC_ROWS_T], 1, 2)
        ret = _ret_call(rq, rk, rv)
        ssd = _ssd_call(xbc, scal, scal_t, dskip_x)
        fox = _fox_call(fq, fk, fv, scal, scal_t)
        x2d = _merge_call(x2d, mod, gpre, gpost, w_gate, ret, yb, ssd, fox, ssm_norm[i][None, :],
                          w_ret_out[i].astype(BF16), w_pool_out[i].astype(BF16),
                          w_ssm_out[i].astype(BF16), w_fox_out[i].astype(BF16),
                          w_out[i].astype(BF16), seq)
        x2d = _ffn_call(x2d, mod, gpre, gpost, w_ffn_in[i, 1].astype(BF16),
                        w_ffn_out[i, 1].astype(BF16), 2, seq)
        x = x2d.reshape(bsz, seq, d)
    return x
```

```python
import functools

import jax
import jax.numpy as jnp
import numpy as np
from jax import lax
from jax.experimental import pallas as pl
from jax.experimental.pallas import tpu as pltpu

F32 = jnp.float32
BF16 = jnp.bfloat16

D_MODEL = 1024
RET_HEADS = 4
RET_DK = 128
RET_DV = 128
RET_W = RET_HEADS * RET_DK
ROPE_BASE = 10000.0
POOL_WINDOWS = (2, 4, 8, 16)
POOL_GROUPS = 4
POOL_GROUP_DIM = 128
POOL_W = POOL_GROUPS * POOL_GROUP_DIM
SSM_HEADS = 16
SSM_HEAD_DIM = 64
SSM_INNER = SSM_HEADS * SSM_HEAD_DIM
SSM_GROUPS = 2
SSM_STATE = 128
SSM_CONV = 4
SSM_XBC = SSM_INNER + 2 * SSM_GROUPS * SSM_STATE
FOX_HEADS = 4
FOX_HEAD_DIM = 128
FOX_W = FOX_HEADS * FOX_HEAD_DIM
N_BRANCH = 4
D_FF = 2816
N_SUBLAYERS = 3
RMS_EPS = 1e-6
GN_EPS = 1e-5
IN_SIZES = (RET_W, RET_W, RET_W, RET_W, POOL_W, SSM_INNER, SSM_XBC, SSM_HEADS,
            FOX_W, FOX_W, FOX_W, FOX_HEADS, N_BRANCH * D_MODEL)

LANES = 128
VMEM_LIMIT = 56 * 1024 * 1024
ADA_TN = 1152
TOK_TILE = 512
FFN_CHUNK = D_FF // 2
SEQ_TILE = 512
RET_CHUNK = 256
SSD_CHUNK = 128
FOX_TILE = 512
NEG_BIG = -1e30

SC_DT = 0
SC_ACUM = 16
SC_CUMF = 32
FOX_SLAB = 2 * FOX_HEAD_DIM
FOX_PIECES = 3


def _sigmoid(v):
    return 1.0 / (1.0 + jnp.exp(-v))


def _silu(v):
    return v * _sigmoid(v)


def _rms(v, gain):
    return v * lax.rsqrt(jnp.mean(v * v, axis=-1, keepdims=True) + RMS_EPS) * gain


def _modulated(x, mod_ref, gpre_ref, sub):
    shift = mod_ref[0, 3 * sub:3 * sub + 1, :]
    scale = mod_ref[0, 3 * sub + 1:3 * sub + 2, :]
    return _rms(x, gpre_ref[sub:sub + 1, :]) * (1.0 + scale) + shift


def _dot(a, b):
    return jnp.dot(a, b, preferred_element_type=F32)


def _dot_nt(a, b):
    return lax.dot_general(a, b, (((1,), (1,)), ((), ())), preferred_element_type=F32)


def _dot_tn(a, b):
    return lax.dot_general(a, b, (((0,), (0,)), ((), ())), preferred_element_type=F32)


def _resident(shape):
    nd = len(shape)
    return pl.BlockSpec(shape, lambda *_: (0,) * nd, pipeline_mode=pl.Buffered(1))


def _params(n_axes):
    return pltpu.CompilerParams(dimension_semantics=("arbitrary",) * n_axes,
                                vmem_limit_bytes=VMEM_LIMIT)


def _ada_kernel(c_ref, w_ref, b_ref, o_ref):
    sc = _silu(c_ref[...]).astype(BF16)
    o_ref[0] = _dot(sc, w_ref[0].astype(BF16)) + b_ref[0]


def _ada_call(c, w_ada, b_ada):
    depth, d, width = w_ada.shape
    bsz = c.shape[0]
    return pl.pallas_call(
        _ada_kernel,
        grid=(depth, width // ADA_TN),
        in_specs=[pl.BlockSpec((bsz, d), lambda l, n: (0, 0)),
                  pl.BlockSpec((1, d, ADA_TN), lambda l, n: (l, 0, n)),
                  pl.BlockSpec((1, 1, ADA_TN), lambda l, n: (l, 0, n))],
        out_specs=pl.BlockSpec((1, bsz, ADA_TN), lambda l, n: (l, 0, n)),
        out_shape=jax.ShapeDtypeStruct((depth, bsz, width), F32),
        compiler_params=_params(2),
        name="adaln",
    )(c, w_ada, b_ada.reshape(depth, 1, width))


def _ffn_kernel(x_ref, mod_ref, gpre_ref, gpost_ref, win_ref, wout_ref, o_ref, *, sub):
    x = x_ref[...]
    hb = _modulated(x, mod_ref, gpre_ref, sub).astype(BF16)
    y = None
    for lo in range(0, D_FF, FFN_CHUNK):
        g = _dot(hb, win_ref[:, lo:lo + FFN_CHUNK])
        u = _dot(hb, win_ref[:, D_FF + lo:D_FF + lo + FFN_CHUNK])
        part = _dot((_silu(g) * u).astype(BF16), wout_ref[lo:lo + FFN_CHUNK, :])
        y = part if y is None else y + part
    gate = mod_ref[0, 3 * sub + 2:3 * sub + 3, :]
    o_ref[...] = x + (0.5 * gate) * _rms(y, gpost_ref[sub:sub + 1, :])


def _ffn_call(x2d, mod, gpre, gpost, w_in_b, w_out_b, sub, seq):
    n, d = x2d.shape
    tiles_per_seq = seq // TOK_TILE
    return pl.pallas_call(
        functools.partial(_ffn_kernel, sub=sub),
        grid=(n // TOK_TILE,),
        in_specs=[pl.BlockSpec((TOK_TILE, d), lambda i: (i, 0)),
                  pl.BlockSpec((1, 3 * N_SUBLAYERS, d), lambda i: (i // tiles_per_seq, 0, 0)),
                  _resident(gpre.shape), _resident(gpost.shape),
                  _resident(w_in_b.shape), _resident(w_out_b.shape)],
        out_specs=pl.BlockSpec((TOK_TILE, d), lambda i: (i, 0)),
        out_shape=jax.ShapeDtypeStruct((n, d), F32),
        compiler_params=_params(1),
        name="ffn",
    )(x2d, mod, gpre, gpost, w_in_b, w_out_b)


PJ_RQ = 0
PJ_RK = PJ_RQ + RET_W
PJ_RV = PJ_RK + RET_W
PJ_PU = PJ_RV + RET_W
PJ_XBC = PJ_PU + POOL_W
PJ_FQ = PJ_XBC + SSM_XBC
PJ_FK = PJ_FQ + FOX_W
PJ_FV = PJ_FK + FOX_W
PJ_SC = PJ_FV + FOX_W
PJ_WIDTH = PJ_SC + LANES
POOL_HALO = 16
CONV_HALO = 8


def _cumsum_rows(tril_b, s):
    p1 = s.astype(BF16)
    r1 = s - p1.astype(F32)
    p2 = r1.astype(BF16)
    p3 = (r1 - p2.astype(F32)).astype(BF16)
    return _dot(tril_b, p1) + _dot(tril_b, p2) + _dot(tril_b, p3)


def _proj_kernel(x_ref, mod_ref, gpre_ref, w_ref, cos_ref, sin_ref, poolw_ref, pscale_ref,
                 convw_ref, convb_ref, sbias_ref, alog_ref, tril_ref, selq_ref, selk_ref,
                 rq_ref, rk_ref, rv_ref, yb_ref, xbc_ref, fq_ref, fk_ref, fv_ref, scal_ref, scalT_ref,
                 pbuf, cbuf, carry):
    j = pl.program_id(1)
    tm = x_ref.shape[1]
    hb = _modulated(x_ref[0], mod_ref, gpre_ref, 1).astype(BF16)

    cos = cos_ref[...]
    sin = sin_ref[...]
    for base, ref, scl in ((PJ_RQ, rq_ref, None), (PJ_RK, rk_ref, RET_DK ** -0.5)):
        t = _dot(hb, w_ref[:, base:base + RET_W])
        for hd in range(RET_HEADS):
            th = t[:, hd * RET_DK:(hd + 1) * RET_DK]
            r = th * cos + pltpu.roll(th, RET_DK // 2, 1) * sin
            if scl is not None:
                r = r * scl
            ref[0, :, hd * RET_DK:(hd + 1) * RET_DK] = r.astype(BF16)
    rv_ref[0] = _dot(hb, w_ref[:, PJ_RV:PJ_RV + RET_W]).astype(BF16)

    pu = _dot(hb, w_ref[:, PJ_PU:PJ_PU + POOL_W])

    @pl.when(j == 0)
    def _():
        pbuf[0:POOL_HALO, :] = jnp.zeros((POOL_HALO, POOL_W), F32)
        cbuf[0:CONV_HALO, :] = jnp.zeros((CONV_HALO, SSM_XBC), F32)
        carry[...] = jnp.zeros_like(carry)

    @pl.when(j > 0)
    def _():
        pbuf[0:POOL_HALO, :] = pbuf[tm:tm + POOL_HALO, :]
        cbuf[0:CONV_HALO, :] = cbuf[tm:tm + CONV_HALO, :]

    pbuf[POOL_HALO:POOL_HALO + tm, :] = pu
    pos = j * tm + lax.broadcasted_iota(jnp.int32, (tm, POOL_GROUP_DIM), 0)
    for g, win in enumerate(POOL_WINDOWS):
        ls = slice(g * POOL_GROUP_DIM, (g + 1) * POOL_GROUP_DIM)
        cur = pu[:, ls]
        acc = cur
        for s in range(1, win):
            acc = acc + pbuf[POOL_HALO - s:POOL_HALO - s + tm, ls]
        count = jnp.minimum(pos + 1, win).astype(F32)
        pooled = acc / count - cur
        mixed = _dot(pooled.astype(BF16), poolw_ref[g]) * pscale_ref[:, ls]
        yb_ref[0, :, ls] = mixed.astype(BF16)

    xr = _dot(hb, w_ref[:, PJ_XBC:PJ_XBC + SSM_XBC])
    cbuf[CONV_HALO:CONV_HALO + tm, :] = xr
    conv = xr * convw_ref[SSM_CONV - 1:SSM_CONV, :] + convb_ref[...]
    for k in range(SSM_CONV - 1):
        off = CONV_HALO - (SSM_CONV - 1) + k
        conv = conv + cbuf[off:off + tm, :] * convw_ref[k:k + 1, :]
    xbc_ref[0] = _silu(conv).astype(BF16)

    fq = _dot(hb, w_ref[:, PJ_FQ:PJ_FQ + FOX_W]) * (FOX_HEAD_DIM ** -0.5)
    fk = _dot(hb, w_ref[:, PJ_FK:PJ_FK + FOX_W])
    fv = _dot(hb, w_ref[:, PJ_FV:PJ_FV + FOX_W])
    for hd in range(FOX_HEADS):
        src_l = slice(hd * FOX_HEAD_DIM, (hd + 1) * FOX_HEAD_DIM)
        dst_l = slice(hd * FOX_SLAB, hd * FOX_SLAB + FOX_HEAD_DIM)
        fq_ref[0, :, dst_l] = fq[:, src_l].astype(BF16)
        fk_ref[0, dst_l, :] = fk[:, src_l].T.astype(BF16)
        fv_ref[0, :, dst_l] = fv[:, src_l].astype(BF16)

    z = _dot(hb, w_ref[:, PJ_SC:PJ_SC + LANES]) + sbias_ref[...]
    tail = jnp.log1p(jnp.exp(-jnp.abs(z)))
    softplus = jnp.maximum(z, 0.0) + tail
    log_sig = jnp.minimum(z, 0.0) - tail
    lane = lax.broadcasted_iota(jnp.int32, (tm, LANES), 1)
    a_row = -jnp.exp(alog_ref[...])
    in_acum = (lane >= SC_ACUM) & (lane < SC_CUMF)
    in_cumf = (lane >= SC_CUMF) & (lane < SC_CUMF + FOX_HEADS)
    src = jnp.where(in_acum, softplus * a_row, jnp.where(in_cumf, log_sig, 0.0))
    tril_b = tril_ref[...]
    run = carry[...]
    lane_c = lax.broadcasted_iota(jnp.int32, (SSD_CHUNK, LANES), 1)
    for c0 in range(0, tm, SSD_CHUNK):
        rows = slice(c0, c0 + SSD_CHUNK)
        local = _cumsum_rows(tril_b, src[rows, :])
        total = local + run
        scal_ref[0, rows, :] = jnp.where(lane_c < SC_ACUM, softplus[rows, :],
                                         jnp.where(lane_c < SC_CUMF, local, total))
        run = total[SSD_CHUNK - 1:SSD_CHUNK, :]
    carry[...] = run
    sc_all = scal_ref[0]
    scalT_ref[0] = sc_all.T

    c1 = sc_all.astype(BF16)
    r1 = sc_all - c1.astype(F32)
    c2 = r1.astype(BF16)
    c3 = (r1 - c2.astype(F32)).astype(BF16)
    lane_h = lax.broadcasted_iota(jnp.int32, (tm, FOX_HEAD_DIM), 1)
    ones_q = jnp.where((lane_h >= FOX_PIECES) & (lane_h < 2 * FOX_PIECES), 1.0, 0.0)
    ones_k = jnp.where(lane_h < FOX_PIECES, 1.0, 0.0)
    ones_v = jnp.where(lane_h == 0, 1.0, 0.0).astype(BF16)
    bias_q = _dot(c1, selq_ref[0]) + _dot(c2, selq_ref[1]) + _dot(c3, selq_ref[2])
    bias_k = _dot(c1, selk_ref[0]) + _dot(c2, selk_ref[1]) + _dot(c3, selk_ref[2])
    for hd in range(FOX_HEADS):
        src_l = slice(hd * FOX_HEAD_DIM, (hd + 1) * FOX_HEAD_DIM)
        dst_l = slice(hd * FOX_SLAB + FOX_HEAD_DIM, (hd + 1) * FOX_SLAB)
        fq_ref[0, :, dst_l] = (bias_q[:, src_l] + ones_q).astype(BF16)
        fk_ref[0, dst_l, :] = (ones_k - bias_k[:, src_l]).T.astype(BF16)
        fv_ref[0, :, dst_l] = ones_v


def _fox_select_tables():
    selq = np.zeros((FOX_PIECES, LANES, FOX_W), np.float32)
    selk = np.zeros((FOX_PIECES, LANES, FOX_W), np.float32)
    for piece in range(FOX_PIECES):
        for hd in range(FOX_HEADS):
            selq[piece, SC_CUMF + hd, hd * FOX_HEAD_DIM + piece] = 1.0
            selk[piece, SC_CUMF + hd, hd * FOX_HEAD_DIM + FOX_PIECES + piece] = 1.0
    return jnp.asarray(selq, BF16), jnp.asarray(selk, BF16)


def _proj_call(x, mod, gpre, w_b, cos_t, sin_t, poolw_b, pscale, convw, convb, sbias, alog, tril_b):
    bsz, seq, d = x.shape
    tm = TOK_TILE
    selq, selk = _fox_select_tables()
    tok = lambda w: pl.BlockSpec((1, tm, w), lambda b, j: (b, j, 0))
    bf = lambda w: jax.ShapeDtypeStruct((bsz, seq, w), BF16)
    fox_w = FOX_HEADS * FOX_SLAB
    return pl.pallas_call(
        _proj_kernel,
        grid=(bsz, seq // tm),
        in_specs=[tok(d),
                  pl.BlockSpec((1, 3 * N_SUBLAYERS, d), lambda b, j: (b, 0, 0)),
                  _resident(gpre.shape), _resident(w_b.shape),
                  pl.BlockSpec((tm, RET_DK), lambda b, j: (j, 0)),
                  pl.BlockSpec((tm, RET_DK), lambda b, j: (j, 0)),
                  _resident(poolw_b.shape), _resident(pscale.shape),
                  _resident(convw.shape), _resident(convb.shape),
                  _resident(sbias.shape), _resident(alog.shape), _resident(tril_b.shape),
                  _resident(selq.shape), _resident(selk.shape)],
        out_specs=[tok(RET_W), tok(RET_W), tok(RET_W), tok(POOL_W), tok(SSM_XBC),
                   tok(fox_w), pl.BlockSpec((1, fox_w, tm), lambda b, j: (b, 0, j)), tok(fox_w), tok(LANES),
                   pl.BlockSpec((1, LANES, tm), lambda b, j: (b, 0, j))],
        out_shape=[bf(RET_W), bf(RET_W), bf(RET_W), bf(POOL_W), bf(SSM_XBC),
                   bf(fox_w), jax.ShapeDtypeStruct((bsz, fox_w, seq), BF16), bf(fox_w),
                   jax.ShapeDtypeStruct((bsz, seq, LANES), F32),
                   jax.ShapeDtypeStruct((bsz, LANES, seq), F32)],
        scratch_shapes=[pltpu.VMEM((POOL_HALO + tm, POOL_W), F32),
                        pltpu.VMEM((CONV_HALO + tm, SSM_XBC), F32),
                        pltpu.VMEM((1, LANES), F32)],
        compiler_params=_params(2),
        name="mixer_proj",
    )(x, mod, gpre, w_b, cos_t, sin_t, poolw_b, pscale, convw, convb, sbias, alog, tril_b, selq, selk)


def _ret_kernel(q_ref, k_ref, v_ref, idec_ref, qdec_ref, kdec_ref, o_ref, state, *, chunk_decay):
    @pl.when(pl.program_id(1) == 0)
    def _():
        state[...] = jnp.zeros_like(state)

    tile = q_ref.shape[1]
    for hd in range(RET_HEADS):
        ls = slice(hd * RET_DK, (hd + 1) * RET_DK)
        for c0 in range(0, tile, RET_CHUNK):
            rows = slice(c0, c0 + RET_CHUNK)
            q = q_ref[0, rows, ls]
            k = k_ref[0, rows, ls]
            v = v_ref[0, rows, ls]
            st = state[hd]
            scores = _dot_nt(q, k) * idec_ref[hd]
            o = _dot(scores.astype(BF16), v) + _dot(q, st.astype(BF16)) * qdec_ref[hd]
            kd = (k.astype(F32) * kdec_ref[hd]).astype(BF16)
            state[hd] = chunk_decay[hd] * st + _dot_tn(kd, v)
            dev = o - jnp.mean(o, axis=-1, keepdims=True)
            var = jnp.mean(dev * dev, axis=-1, keepdims=True)
            o_ref[0, rows, ls] = (dev * lax.rsqrt(var + GN_EPS)).astype(BF16)


def _ret_tables():
    log_gamma = np.log1p(-np.exp2(-5.0 - np.arange(RET_HEADS, dtype=np.float64)))
    idx = np.arange(RET_CHUNK, dtype=np.float64)
    rel = idx[:, None] - idx[None, :]
    intra = np.where(rel >= 0, np.exp(log_gamma[:, None, None] * np.maximum(rel, 0.0)), 0.0)
    q_decay = np.exp(log_gamma[:, None] * (idx + 1.0))
    k_decay = np.exp(log_gamma[:, None] * (RET_CHUNK - 1.0 - idx))
    widen = lambda t: np.broadcast_to(t[:, :, None], (RET_HEADS, RET_CHUNK, RET_DK))
    chunk_decay = tuple(float(v) for v in np.exp(log_gamma * RET_CHUNK))
    return (jnp.asarray(intra, F32), jnp.asarray(widen(q_decay), F32),
            jnp.asarray(widen(k_decay), F32), chunk_decay)


def _ret_call(rq, rk, rv):
    bsz, seq, w = rq.shape
    idec, qdec, kdec, chunk_decay = _ret_tables()
    tok = pl.BlockSpec((1, SEQ_TILE, w), lambda b, j: (b, j, 0))
    return pl.pallas_call(
        functools.partial(_ret_kernel, chunk_decay=chunk_decay),
        grid=(bsz, seq // SEQ_TILE),
        in_specs=[tok, tok, tok, _resident(idec.shape), _resident(qdec.shape), _resident(kdec.shape)],
        out_specs=tok,
        out_shape=jax.ShapeDtypeStruct((bsz, seq, w), BF16),
        scratch_shapes=[pltpu.VMEM((RET_HEADS, RET_DK, RET_DV), F32)],
        compiler_params=_params(2),
        name="retention",
    )(rq, rk, rv, idec, qdec, kdec)


HEADS_PER_GROUP = SSM_HEADS // SSM_GROUPS
GROUP_W = HEADS_PER_GROUP * SSM_HEAD_DIM


def _ssd_kernel(xbc_ref, scal_ref, scalT_ref, ea_sel_ref, dt_sel_ref, dskip_ref, o_ref, state):
    @pl.when(pl.program_id(1) == 0)
    def _():
        state[...] = jnp.zeros_like(state)

    tile = xbc_ref.shape[1]
    cl = SSD_CHUNK
    lane = lax.broadcasted_iota(jnp.int32, (cl, LANES), 1)
    row_i = lax.broadcasted_iota(jnp.int32, (cl, cl), 0)
    col_i = lax.broadcasted_iota(jnp.int32, (cl, cl), 1)
    causal = col_i <= row_i
    low_half = lane < SSM_HEAD_DIM
    ea_sel = ea_sel_ref[...]
    dt_sel = dt_sel_ref[...]
    for c0 in range(0, tile, cl):
        rows = slice(c0, c0 + cl)
        sc = scal_ref[0, rows, :]
        in_acum = (lane >= SC_ACUM) & (lane < SC_CUMF)
        acum = jnp.where(in_acum, sc, 0.0)
        ea = jnp.exp(acum)
        dec_end = jnp.exp(acum[cl - 1:cl, :] - acum)
        dt = jnp.where(lane < SC_ACUM, sc, 0.0)
        ea_hi = ea.astype(BF16)
        ea_lo = (ea - ea_hi.astype(F32)).astype(BF16)
        ea_x = _dot(ea_hi, ea_sel) + _dot(ea_lo, ea_sel)
        w_x = _dot(dec_end.astype(BF16), ea_sel) * _dot(dt.astype(BF16), dt_sel)
        xs_b = xbc_ref[0, rows, 0:SSM_INNER]
        xs = xs_b.astype(F32)
        xw_b = (xs * w_x).astype(BF16)
        for g in range(SSM_GROUPS):
            gs = slice(g * GROUP_W, (g + 1) * GROUP_W)
            bm = xbc_ref[0, rows, SSM_INNER + g * SSM_STATE:SSM_INNER + (g + 1) * SSM_STATE]
            cm_lo = SSM_INNER + SSM_GROUPS * SSM_STATE + g * SSM_STATE
            cm = xbc_ref[0, rows, cm_lo:cm_lo + SSM_STATE]
            cb = _dot_nt(cm, bm)
            st = state[g]
            y_inter = _dot(cm, st.astype(BF16)) * ea_x[:, gs]
            state[g] = st * ea_x[cl - 1:cl, gs] + _dot_tn(bm, xw_b[:, gs])
            for pair in range(HEADS_PER_GROUP // 2):
                mats = []
                for hh in (2 * pair, 2 * pair + 1):
                    hd = g * HEADS_PER_GROUP + hh
                    a_col = jnp.sum(jnp.where(lane == SC_ACUM + hd, sc, 0.0), axis=-1, keepdims=True)
                    a_row = scalT_ref[0, SC_ACUM + hd:SC_ACUM + hd + 1, rows]
                    dt_row = scalT_ref[0, SC_DT + hd:SC_DT + hd + 1, rows]
                    lmat = jnp.exp(jnp.where(causal, a_col - a_row, NEG_BIG))
                    mats.append((lmat * dt_row * cb).astype(BF16))
                lo = g * GROUP_W + pair * LANES
                x_pair = xs_b[:, lo:lo + LANES]
                zero = jnp.zeros_like(x_pair)
                rhs = jnp.concatenate([jnp.where(low_half, x_pair, zero),
                                       jnp.where(low_half, zero, x_pair)], axis=0)
                y_pair = _dot(jnp.concatenate(mats, axis=1), rhs)
                y_pair = y_pair + y_inter[:, pair * LANES:(pair + 1) * LANES]
                y_pair = y_pair + dskip_ref[:, lo:lo + LANES] * xs[:, lo:lo + LANES]
                o_ref[0, rows, lo:lo + LANES] = y_pair.astype(BF16)


def _ssd_tables():
    ea_sel = np.zeros((LANES, SSM_INNER), np.float32)
    dt_sel = np.zeros((LANES, SSM_INNER), np.float32)
    for hd in range(SSM_HEADS):
        ea_sel[SC_ACUM + hd, hd * SSM_HEAD_DIM:(hd + 1) * SSM_HEAD_DIM] = 1.0
        dt_sel[SC_DT + hd, hd * SSM_HEAD_DIM:(hd + 1) * SSM_HEAD_DIM] = 1.0
    return jnp.asarray(ea_sel, BF16), jnp.asarray(dt_sel, BF16)


def _ssd_call(xbc, scal, scal_t, dskip_x):
    bsz, seq, _ = xbc.shape
    ea_sel, dt_sel = _ssd_tables()
    return pl.pallas_call(
        _ssd_kernel,
        grid=(bsz, seq // SEQ_TILE),
        in_specs=[pl.BlockSpec((1, SEQ_TILE, SSM_XBC), lambda b, j: (b, j, 0)),
                  pl.BlockSpec((1, SEQ_TILE, LANES), lambda b, j: (b, j, 0)),
                  pl.BlockSpec((1, LANES, SEQ_TILE), lambda b, j: (b, 0, j)),
                  _resident(ea_sel.shape), _resident(dt_sel.shape), _resident(dskip_x.shape)],
        out_specs=pl.BlockSpec((1, SEQ_TILE, SSM_INNER), lambda b, j: (b, j, 0)),
        out_shape=jax.ShapeDtypeStruct((bsz, seq, SSM_INNER), BF16),
        scratch_shapes=[pltpu.VMEM((SSM_GROUPS, SSM_STATE, GROUP_W), F32)],
        compiler_params=_params(2),
        name="ssd",
    )(xbc, scal, scal_t, ea_sel, dt_sel, dskip_x)


FOX_HALVES = 2
FOX_KV = FOX_TILE // 2


def _fox_kernel(q_ref, k_ref, v_ref, o_ref, m_s, acc_s, s_a, s_b):
    i = pl.program_id(2)
    tq = q_ref.shape[1]
    th = tq // FOX_HALVES
    m_s[...] = jnp.full_like(m_s, NEG_BIG)
    acc_s[...] = jnp.zeros_like(acc_s)

    def scores(t):
        ks = pl.multiple_of(t * FOX_KV, FOX_KV)
        return _dot(q_ref[0], k_ref[0, :, pl.ds(ks, FOX_KV)])

    def consume(s_ref, t, diag):
        ks = pl.multiple_of(t * FOX_KV, FOX_KV)
        v = v_ref[0, pl.ds(ks, FOX_KV), :]
        for half in range(FOX_HALVES):
            rows = slice(half * th, (half + 1) * th)
            s = s_ref[rows, :]
            if diag is not None:
                row_i = half * th + lax.broadcasted_iota(jnp.int32, (th, FOX_KV), 0)
                col_i = diag * FOX_KV + lax.broadcasted_iota(jnp.int32, (th, FOX_KV), 1)
                s = jnp.where(col_i <= row_i, s, NEG_BIG)
            m_old = m_s[rows, :]
            m_new = jnp.maximum(m_old, jnp.max(s, axis=-1, keepdims=True))
            p = jnp.concatenate([jnp.exp(s[:, c0:c0 + LANES] - m_new)
                                 for c0 in range(0, FOX_KV, LANES)], axis=1)
            alpha = jnp.exp(m_old - m_new)
            pv = _dot(p.astype(BF16), v)
            for c0 in range(0, FOX_SLAB, LANES):
                acc_s[rows, c0:c0 + LANES] = alpha * acc_s[rows, c0:c0 + LANES] + pv[:, c0:c0 + LANES]
            m_s[rows, :] = m_new

    s_a[...] = scores(0)

    def body(t, carry):
        s_b[...] = scores(2 * t + 1)
        consume(s_a, 2 * t, None)
        s_a[...] = scores(2 * t + 2)
        consume(s_b, 2 * t + 1, None)
        return carry

    lax.fori_loop(0, i, body, 0)
    s_b[...] = scores(2 * i + 1)
    consume(s_a, 2 * i, 0)
    consume(s_b, 2 * i + 1, 1)
    acc = acc_s[...]
    o_ref[0] = (acc[:, :FOX_HEAD_DIM] / acc[:, FOX_HEAD_DIM:FOX_HEAD_DIM + 1]).astype(BF16)


def _fox_call(fq, fk, fv):
    bsz, seq, _ = fq.shape
    tq = FOX_TILE
    return pl.pallas_call(
        _fox_kernel,
        grid=(bsz, FOX_HEADS, seq // tq),
        in_specs=[pl.BlockSpec((1, tq, FOX_SLAB), lambda b, h, i: (b, i, h)),
                  pl.BlockSpec((1, FOX_SLAB, seq), lambda b, h, i: (b, h, 0)),
                  pl.BlockSpec((1, seq, FOX_SLAB), lambda b, h, i: (b, 0, h))],
        out_specs=pl.BlockSpec((1, tq, FOX_HEAD_DIM), lambda b, h, i: (b, i, h)),
        out_shape=jax.ShapeDtypeStruct((bsz, seq, FOX_W), BF16),
        scratch_shapes=[pltpu.VMEM((tq, LANES), F32), pltpu.VMEM((tq, FOX_SLAB), F32),
                        pltpu.VMEM((tq, FOX_KV), F32), pltpu.VMEM((tq, FOX_KV), F32)],
        compiler_params=_params(3),
        name="fox_attention",
    )(fq, fk, fv)


MG_RG = 0
MG_SZ = MG_RG + RET_W
MG_GL = MG_SZ + SSM_INNER
MG_WIDTH = MG_GL + N_BRANCH * D_MODEL


def _merge_kernel(x_ref, mod_ref, gpre_ref, gpost_ref, wg_ref, ret_ref, yb_ref, ssd_ref, fox_ref,
                  ssmnorm_ref, wro_ref, wpo_ref, wso_ref, wfo_ref, wout_ref, o_ref):
    x = x_ref[...]
    hb = _modulated(x, mod_ref, gpre_ref, 1).astype(BF16)
    rg = _dot(hb, wg_ref[:, MG_RG:MG_RG + RET_W])
    y_a = (_silu(rg) * ret_ref[...].astype(F32)).astype(BF16)
    sz = _dot(hb, wg_ref[:, MG_SZ:MG_SZ + SSM_INNER])
    y_c = _rms(ssd_ref[...].astype(F32) * _silu(sz), ssmnorm_ref[...]).astype(BF16)
    merged = None
    for br, (y, w_ref) in enumerate(((y_a, wro_ref), (yb_ref[...], wpo_ref),
                                     (y_c, wso_ref), (fox_ref[...], wfo_ref))):
        lo = MG_GL + br * D_MODEL
        gate = _sigmoid(_dot(hb, wg_ref[:, lo:lo + D_MODEL]))
        part = gate * _dot(y, w_ref[...])
        merged = part if merged is None else merged + part
    y_out = _dot(merged.astype(BF16), wout_ref[...])
    o_ref[...] = x + mod_ref[0, 5:6, :] * _rms(y_out, gpost_ref[1:2, :])


def _merge_call(x2d, mod, gpre, gpost, wg_b, ret, yb, ssd, fox, ssmnorm, wro, wpo, wso, wfo, wout, seq):
    n, d = x2d.shape
    tm = TOK_TILE
    tiles_per_seq = seq // tm
    tok = lambda w: pl.BlockSpec((tm, w), lambda i: (i, 0))
    return pl.pallas_call(
        _merge_kernel,
        grid=(n // tm,),
        in_specs=[tok(d),
                  pl.BlockSpec((1, 3 * N_SUBLAYERS, d), lambda i: (i // tiles_per_seq, 0, 0)),
                  _resident(gpre.shape), _resident(gpost.shape), _resident(wg_b.shape),
                  tok(RET_W), tok(POOL_W), tok(SSM_INNER), tok(FOX_W),
                  _resident(ssmnorm.shape), _resident(wro.shape), _resident(wpo.shape),
                  _resident(wso.shape), _resident(wfo.shape), _resident(wout.shape)],
        out_specs=tok(d),
        out_shape=jax.ShapeDtypeStruct((n, d), F32),
        compiler_params=_params(1),
        name="mixer_merge",
    )(x2d, mod, gpre, gpost, wg_b, ret.reshape(n, -1), yb.reshape(n, -1), ssd.reshape(n, -1),
      fox.reshape(n, -1), ssmnorm, wro, wpo, wso, wfo, wout)


def _rotary_tables(seq):
    half = RET_DK // 2
    inv = ROPE_BASE ** (-jnp.arange(half, dtype=F32) / half)
    ang = jnp.arange(seq, dtype=F32)[:, None] * inv[None, :]
    cos = jnp.cos(ang)
    sin = jnp.sin(ang)
    return jnp.concatenate([cos, cos], axis=-1), jnp.concatenate([-sin, sin], axis=-1)


def _place(vals, lane0):
    return jnp.zeros((1, LANES), F32).at[0, lane0:lane0 + vals.shape[0]].set(vals.astype(F32))


def kernel(x, c, w_ada, b_ada, norm_pre, norm_post, w_ffn_in, w_ffn_out, w_in, b_forget, pool_w, pool_scale, conv_w, conv_b, dt_bias, a_log, d_skip, ssm_norm, w_ret_out, w_pool_out, w_ssm_out, w_fox_out, w_out):
    bsz, seq, d = x.shape
    depth = w_ada.shape[0]
    n = bsz * seq
    mods = _ada_call(c, w_ada, b_ada)
    cos_t, sin_t = _rotary_tables(seq)
    tril_b = jnp.asarray(np.tril(np.ones((SSD_CHUNK, SSD_CHUNK), np.float32)), BF16)
    pts = [int(p) for p in np.cumsum(IN_SIZES)[:-1]]
    for i in range(depth):
        mod = mods[i].reshape(bsz, 3 * N_SUBLAYERS, d)
        gpre, gpost = norm_pre[i], norm_post[i]
        x2d = _ffn_call(x.reshape(n, d), mod, gpre, gpost, w_ffn_in[i, 0].astype(BF16),
                        w_ffn_out[i, 0].astype(BF16), 0, seq)

        (w_rq, w_rk, w_rv, w_rg, w_pu, w_sz, w_xbc, w_dt, w_fq, w_fk, w_fv, w_ff, w_gl) = jnp.split(w_in[i], pts, axis=-1)
        w_sc = jnp.concatenate([w_dt, w_dt, w_ff, jnp.zeros((d, LANES - 2 * SSM_HEADS - FOX_HEADS), F32)], axis=-1)
        w_proj = jnp.concatenate([w_rq, w_rk, w_rv, w_pu, w_xbc, w_fq, w_fk, w_fv, w_sc], axis=-1).astype(BF16)
        w_gate = jnp.concatenate([w_rg, w_sz, w_gl], axis=-1).astype(BF16)
        sbias = _place(dt_bias[i], SC_DT) + _place(dt_bias[i], SC_ACUM) + _place(b_forget[i], SC_CUMF)
        alog = _place(a_log[i], SC_ACUM)
        dskip_x = jnp.repeat(d_skip[i].astype(F32), SSM_HEAD_DIM)[None, :]

        rq, rk, rv, yb, xbc, fq, fk, fv, scal, scal_t = _proj_call(
            x2d.reshape(bsz, seq, d), mod, gpre, w_proj, cos_t, sin_t, pool_w[i].astype(BF16),
            pool_scale[i][None, :], conv_w[i], conv_b[i][None, :], sbias, alog, tril_b)
        ret = _ret_call(rq, rk, rv)
        ssd = _ssd_call(xbc, scal, scal_t, dskip_x)
        fox = _fox_call(fq, fk, fv)
        x2d = _merge_call(x2d, mod, gpre, gpost, w_gate, ret, yb, ssd, fox, ssm_norm[i][None, :],
                          w_ret_out[i].astype(BF16), w_pool_out[i].astype(BF16),
                          w_ssm_out[i].astype(BF16), w_fox_out[i].astype(BF16),
                          w_out[i].astype(BF16), seq)
        x2d = _ffn_call(x2d, mod, gpre, gpost, w_ffn_in[i, 1].astype(BF16),
                        w_ffn_out[i, 1].astype(BF16), 2, seq)
        x = x2d.reshape(bsz, seq, d)
    return x
```

```python
import functools

import jax
import jax.numpy as jnp
import numpy as np
from jax import lax
from jax.experimental import pallas as pl
from jax.experimental.pallas import tpu as pltpu

F32 = jnp.float32
BF16 = jnp.bfloat16

D_MODEL = 1024
RET_HEADS = 4
RET_DK = 128
RET_DV = 128
RET_W = RET_HEADS * RET_DK
ROPE_BASE = 10000.0
POOL_WINDOWS = (2, 4, 8, 16)
POOL_GROUPS = 4
POOL_GROUP_DIM = 128
POOL_W = POOL_GROUPS * POOL_GROUP_DIM
SSM_HEADS = 16
SSM_HEAD_DIM = 64
SSM_INNER = SSM_HEADS * SSM_HEAD_DIM
SSM_GROUPS = 2
SSM_STATE = 128
SSM_CONV = 4
SSM_XBC = SSM_INNER + 2 * SSM_GROUPS * SSM_STATE
FOX_HEADS = 4
FOX_HEAD_DIM = 128
FOX_W = FOX_HEADS * FOX_HEAD_DIM
N_BRANCH = 4
D_FF = 2816
N_SUBLAYERS = 3
RMS_EPS = 1e-6
GN_EPS = 1e-5
IN_SIZES = (RET_W, RET_W, RET_W, RET_W, POOL_W, SSM_INNER, SSM_XBC, SSM_HEADS,
            FOX_W, FOX_W, FOX_W, FOX_HEADS, N_BRANCH * D_MODEL)

LANES = 128
VMEM_LIMIT = 56 * 1024 * 1024
ADA_TN = 1152
TOK_TILE = 512
MXU_DIM = 256
FFN_CHUNK = 6 * MXU_DIM
SEQ_TILE = 512
RET_CHUNK = 256
SSD_CHUNK = 128
FOX_TILE = 1024
NEG_BIG = -1e30

SC_DT = 0
SC_ACUM = 16
SC_CUMF = 32
FOX_SLAB = 2 * FOX_HEAD_DIM
FOX_PIECES = 3


def _sigmoid(v):
    return 1.0 / (1.0 + jnp.exp(-v))


def _silu(v):
    return v * _sigmoid(v)


def _rms(v, gain):
    return v * lax.rsqrt(jnp.mean(v * v, axis=-1, keepdims=True) + RMS_EPS) * gain


def _modulated(x, mod_ref, gpre_ref, sub):
    shift = mod_ref[0, 3 * sub:3 * sub + 1, :]
    scale = mod_ref[0, 3 * sub + 1:3 * sub + 2, :]
    return _rms(x, gpre_ref[sub:sub + 1, :]) * (1.0 + scale) + shift


def _dot(a, b):
    return jnp.dot(a, b, preferred_element_type=F32)


def _dot_nt(a, b):
    return lax.dot_general(a, b, (((1,), (1,)), ((), ())), preferred_element_type=F32)


def _dot_tn(a, b):
    return lax.dot_general(a, b, (((0,), (0,)), ((), ())), preferred_element_type=F32)


def _resident(shape):
    nd = len(shape)
    return pl.BlockSpec(shape, lambda *_: (0,) * nd, pipeline_mode=pl.Buffered(1))


def _params(n_axes):
    return pltpu.CompilerParams(dimension_semantics=("arbitrary",) * n_axes,
                                vmem_limit_bytes=VMEM_LIMIT)


def _ada_kernel(c_ref, w_ref, b_ref, o_ref):
    sc = _silu(c_ref[...]).astype(BF16)
    o_ref[0] = _dot(sc, w_ref[0].astype(BF16)) + b_ref[0]


def _ada_call(c, w_ada, b_ada):
    depth, d, width = w_ada.shape
    bsz = c.shape[0]
    return pl.pallas_call(
        _ada_kernel,
        grid=(depth, width // ADA_TN),
        in_specs=[pl.BlockSpec((bsz, d), lambda l, n: (0, 0)),
                  pl.BlockSpec((1, d, ADA_TN), lambda l, n: (l, 0, n)),
                  pl.BlockSpec((1, 1, ADA_TN), lambda l, n: (l, 0, n))],
        out_specs=pl.BlockSpec((1, bsz, ADA_TN), lambda l, n: (l, 0, n)),
        out_shape=jax.ShapeDtypeStruct((depth, bsz, width), F32),
        compiler_params=_params(2),
        name="adaln",
    )(c, w_ada, b_ada.reshape(depth, 1, width))


def _ffn_kernel(x_ref, mod_ref, gpre_ref, gpost_ref, win_ref, wout_ref, o_ref, *, sub):
    x = x_ref[...]
    hb = _modulated(x, mod_ref, gpre_ref, sub).astype(BF16)
    y = None
    for lo in range(0, D_FF, FFN_CHUNK):
        hi = min(lo + FFN_CHUNK, D_FF)
        g = _dot(hb, win_ref[:, lo:hi])
        u = _dot(hb, win_ref[:, D_FF + lo:D_FF + hi])
        part = _dot((_silu(g) * u).astype(BF16), wout_ref[lo:hi, :])
        y = part if y is None else y + part
    gate = mod_ref[0, 3 * sub + 2:3 * sub + 3, :]
    o_ref[...] = x + (0.5 * gate) * _rms(y, gpost_ref[sub:sub + 1, :])


def _ffn_call(x2d, mod, gpre, gpost, w_in_b, w_out_b, sub, seq):
    n, d = x2d.shape
    tiles_per_seq = seq // TOK_TILE
    return pl.pallas_call(
        functools.partial(_ffn_kernel, sub=sub),
        grid=(n // TOK_TILE,),
        in_specs=[pl.BlockSpec((TOK_TILE, d), lambda i: (i, 0)),
                  pl.BlockSpec((1, 3 * N_SUBLAYERS, d), lambda i: (i // tiles_per_seq, 0, 0)),
                  _resident(gpre.shape), _resident(gpost.shape),
                  _resident(w_in_b.shape), _resident(w_out_b.shape)],
        out_specs=pl.BlockSpec((TOK_TILE, d), lambda i: (i, 0)),
        out_shape=jax.ShapeDtypeStruct((n, d), F32),
        compiler_params=_params(1),
        name="ffn",
    )(x2d, mod, gpre, gpost, w_in_b, w_out_b)


IN_OFFSETS = tuple(int(v) for v in np.cumsum((0,) + IN_SIZES[:-1]))
(IN_RQ, IN_RK, IN_RV, IN_RG, IN_PU, IN_SZ, IN_XBC, IN_DT, IN_FQ, IN_FK, IN_FV, IN_FF, IN_GL) = IN_OFFSETS
REGION_W = (RET_W + SSM_INNER + N_BRANCH * D_MODEL)


def _regroup_plan():
    shifts = sorted({off % LANES for off in IN_OFFSETS})
    kind_of = {s: k for k, s in enumerate(shifts)}
    scalar_kind, zero_kind = len(shifts), len(shifts) + 1
    place = np.zeros((len(shifts) + 2, 2 * LANES, LANES), np.float32)
    for s, k in kind_of.items():
        place[k, s + np.arange(LANES), np.arange(LANES)] = 1.0
    dt_lane, ff_lane = IN_DT % LANES, IN_FF % LANES
    assert dt_lane + SSM_HEADS <= LANES and ff_lane + FOX_HEADS <= LANES
    for hd in range(SSM_HEADS):
        place[scalar_kind, dt_lane + hd, SC_DT + hd] = 1.0
        place[scalar_kind, dt_lane + hd, SC_ACUM + hd] = 1.0
    for hd in range(FOX_HEADS):
        place[scalar_kind, LANES + ff_lane + hd, SC_CUMF + hd] = 1.0
    lo, hi, kind = [], [], []

    def span(off, width):
        for c in range(off, off + width, LANES):
            lo.append(c // LANES)
            hi.append(c // LANES + (1 if c % LANES else 0))
            kind.append(kind_of[c % LANES])

    proj = ((IN_RQ, RET_W), (IN_RK, RET_W), (IN_RV, RET_W), (IN_PU, POOL_W), (IN_XBC, SSM_XBC),
            (IN_FQ, FOX_W), (IN_FK, FOX_W), (IN_FV, FOX_W))
    for off, width in proj:
        span(off, width)
    lo.append(IN_DT // LANES); hi.append(IN_FF // LANES); kind.append(scalar_kind)
    while len(lo) < REGION_W // LANES:
        lo.append(0); hi.append(0); kind.append(zero_kind)
    for off, width in ((IN_RG, RET_W), (IN_SZ, SSM_INNER), (IN_GL, N_BRANCH * D_MODEL)):
        span(off, width)
    assert len(lo) == 2 * REGION_W // LANES
    as_i32 = lambda v: jnp.asarray(np.asarray(v, np.int32))
    need = place[:, LANES:, :].any(axis=2).astype(np.float32)[:, None, :]
    return as_i32(lo), as_i32(hi), as_i32(kind), jnp.asarray(place, BF16), jnp.asarray(need, F32)


def _regroup_kernel(lo_ref, hi_ref, kind_ref, wlo_ref, whi_ref, place_ref, need_ref, o_ref):
    del lo_ref, hi_ref
    kind = kind_ref[pl.program_id(0)]
    second = jnp.where(need_ref[kind] > 0.0, whi_ref[0], 0.0)
    pair = jnp.concatenate([wlo_ref[0], second], axis=1).astype(BF16)
    o_ref[...] = _dot(pair, place_ref[kind]).astype(BF16)


def _regroup_call(w_in, layer):
    _, d, _ = w_in.shape
    lo, hi, kind, place, need = _regroup_plan()
    n_blocks = lo.shape[0]
    grid_spec = pltpu.PrefetchScalarGridSpec(
        num_scalar_prefetch=3,
        grid=(n_blocks,),
        in_specs=[pl.BlockSpec((1, d, LANES), lambda o, lo, hi, kind: (layer, 0, lo[o])),
                  pl.BlockSpec((1, d, LANES), lambda o, lo, hi, kind: (layer, 0, hi[o])),
                  pl.BlockSpec(place.shape, lambda o, lo, hi, kind: (0, 0, 0)),
                  pl.BlockSpec(need.shape, lambda o, lo, hi, kind: (0, 0, 0))],
        out_specs=pl.BlockSpec((d, LANES), lambda o, lo, hi, kind: (0, o)))
    return pl.pallas_call(
        _regroup_kernel,
        grid_spec=grid_spec,
        out_shape=jax.ShapeDtypeStruct((d, n_blocks * LANES), BF16),
        compiler_params=_params(1),
        name="regroup_w_in",
    )(lo, hi, kind, w_in, w_in, place, need)


PJ_RQ = 0
PJ_RK = PJ_RQ + RET_W
PJ_RV = PJ_RK + RET_W
PJ_PU = PJ_RV + RET_W
PJ_XBC = PJ_PU + POOL_W
PJ_FQ = PJ_XBC + SSM_XBC
PJ_FK = PJ_FQ + FOX_W
PJ_FV = PJ_FK + FOX_W
PJ_SC = PJ_FV + FOX_W
PJ_WIDTH = PJ_SC + LANES
POOL_HALO = 16
CONV_HALO = 8


def _cumsum_rows(tril_b, s):
    p1 = s.astype(BF16)
    r1 = s - p1.astype(F32)
    p2 = r1.astype(BF16)
    p3 = (r1 - p2.astype(F32)).astype(BF16)
    return _dot(tril_b, p1) + _dot(tril_b, p2) + _dot(tril_b, p3)


def _proj_kernel(x_ref, mod_ref, gpre_ref, w_ref, cos_ref, sin_ref, poolw_ref, pscale_ref,
                 convw_ref, convb_ref, sbias_ref, alog_ref, tril_ref, selq_ref, selk_ref,
                 rq_ref, rk_ref, rv_ref, yb_ref, xbc_ref, fq_ref, fk_ref, fv_ref, scal_ref, scalT_ref,
                 pbuf, cbuf, carry):
    j = pl.program_id(1)
    tm = x_ref.shape[1]
    hb = _modulated(x_ref[0], mod_ref, gpre_ref, 1).astype(BF16)

    cos = cos_ref[...]
    sin = sin_ref[...]
    for base, ref, scl in ((PJ_RQ, rq_ref, None), (PJ_RK, rk_ref, RET_DK ** -0.5)):
        t = _dot(hb, w_ref[:, base:base + RET_W])
        for hd in range(RET_HEADS):
            th = t[:, hd * RET_DK:(hd + 1) * RET_DK]
            r = th * cos + pltpu.roll(th, RET_DK // 2, 1) * sin
            if scl is not None:
                r = r * scl
            ref[0, :, hd * RET_DK:(hd + 1) * RET_DK] = r.astype(BF16)
    rv_ref[0] = _dot(hb, w_ref[:, PJ_RV:PJ_RV + RET_W]).astype(BF16)

    pu = _dot(hb, w_ref[:, PJ_PU:PJ_PU + POOL_W])

    @pl.when(j == 0)
    def _():
        pbuf[0:POOL_HALO, :] = jnp.zeros((POOL_HALO, POOL_W), F32)
        cbuf[0:CONV_HALO, :] = jnp.zeros((CONV_HALO, SSM_XBC), F32)
        carry[...] = jnp.zeros_like(carry)

    @pl.when(j > 0)
    def _():
        pbuf[0:POOL_HALO, :] = pbuf[tm:tm + POOL_HALO, :]
        cbuf[0:CONV_HALO, :] = cbuf[tm:tm + CONV_HALO, :]

    pbuf[POOL_HALO:POOL_HALO + tm, :] = pu
    pos = j * tm + lax.broadcasted_iota(jnp.int32, (tm, POOL_GROUP_DIM), 0)
    for g, win in enumerate(POOL_WINDOWS):
        ls = slice(g * POOL_GROUP_DIM, (g + 1) * POOL_GROUP_DIM)
        cur = pu[:, ls]
        acc = cur
        for s in range(1, win):
            acc = acc + pbuf[POOL_HALO - s:POOL_HALO - s + tm, ls]
        count = jnp.minimum(pos + 1, win).astype(F32)
        pooled = acc / count - cur
        mixed = _dot(pooled.astype(BF16), poolw_ref[g]) * pscale_ref[:, ls]
        yb_ref[0, :, ls] = mixed.astype(BF16)

    xr = _dot(hb, w_ref[:, PJ_XBC:PJ_XBC + SSM_XBC])
    cbuf[CONV_HALO:CONV_HALO + tm, :] = xr
    conv = xr * convw_ref[SSM_CONV - 1:SSM_CONV, :] + convb_ref[...]
    for k in range(SSM_CONV - 1):
        off = CONV_HALO - (SSM_CONV - 1) + k
        conv = conv + cbuf[off:off + tm, :] * convw_ref[k:k + 1, :]
    xbc_ref[0] = _silu(conv).astype(BF16)

    fq = _dot(hb, w_ref[:, PJ_FQ:PJ_FQ + FOX_W]) * (FOX_HEAD_DIM ** -0.5)
    fk = _dot(hb, w_ref[:, PJ_FK:PJ_FK + FOX_W])
    fv = _dot(hb, w_ref[:, PJ_FV:PJ_FV + FOX_W])
    for hd in range(FOX_HEADS):
        src_l = slice(hd * FOX_HEAD_DIM, (hd + 1) * FOX_HEAD_DIM)
        dst_l = slice(hd * FOX_SLAB, hd * FOX_SLAB + FOX_HEAD_DIM)
        fq_ref[0, :, dst_l] = fq[:, src_l].astype(BF16)
        fk_ref[0, dst_l, :] = fk[:, src_l].T.astype(BF16)
        fv_ref[0, :, dst_l] = fv[:, src_l].astype(BF16)

    z = _dot(hb, w_ref[:, PJ_SC:PJ_SC + LANES]) + sbias_ref[...]
    tail = jnp.log1p(jnp.exp(-jnp.abs(z)))
    softplus = jnp.maximum(z, 0.0) + tail
    log_sig = jnp.minimum(z, 0.0) - tail
    lane = lax.broadcasted_iota(jnp.int32, (tm, LANES), 1)
    a_row = -jnp.exp(alog_ref[...])
    in_acum = (lane >= SC_ACUM) & (lane < SC_CUMF)
    in_cumf = (lane >= SC_CUMF) & (lane < SC_CUMF + FOX_HEADS)
    src = jnp.where(in_acum, softplus * a_row, jnp.where(in_cumf, log_sig, 0.0))
    tril_b = tril_ref[...]
    run = carry[...]
    lane_c = lax.broadcasted_iota(jnp.int32, (SSD_CHUNK, LANES), 1)
    for c0 in range(0, tm, SSD_CHUNK):
        rows = slice(c0, c0 + SSD_CHUNK)
        local = _cumsum_rows(tril_b, src[rows, :])
        total = local + run
        scal_ref[0, rows, :] = jnp.where(lane_c < SC_ACUM, softplus[rows, :],
                                         jnp.where(lane_c < SC_CUMF, local, total))
        run = total[SSD_CHUNK - 1:SSD_CHUNK, :]
    carry[...] = run
    sc_all = scal_ref[0]
    scalT_ref[0] = sc_all.T

    c1 = sc_all.astype(BF16)
    r1 = sc_all - c1.astype(F32)
    c2 = r1.astype(BF16)
    c3 = (r1 - c2.astype(F32)).astype(BF16)
    lane_h = lax.broadcasted_iota(jnp.int32, (tm, FOX_HEAD_DIM), 1)
    ones_q = jnp.where((lane_h >= FOX_PIECES) & (lane_h < 2 * FOX_PIECES), 1.0, 0.0)
    ones_k = jnp.where(lane_h < FOX_PIECES, 1.0, 0.0)
    ones_v = jnp.where(lane_h == 0, 1.0, 0.0).astype(BF16)
    bias_q = _dot(c1, selq_ref[0]) + _dot(c2, selq_ref[1]) + _dot(c3, selq_ref[2])
    bias_k = _dot(c1, selk_ref[0]) + _dot(c2, selk_ref[1]) + _dot(c3, selk_ref[2])
    for hd in range(FOX_HEADS):
        src_l = slice(hd * FOX_HEAD_DIM, (hd + 1) * FOX_HEAD_DIM)
        dst_l = slice(hd * FOX_SLAB + FOX_HEAD_DIM, (hd + 1) * FOX_SLAB)
        fq_ref[0, :, dst_l] = (bias_q[:, src_l] + ones_q).astype(BF16)
        fk_ref[0, dst_l, :] = (ones_k - bias_k[:, src_l]).T.astype(BF16)
        fv_ref[0, :, dst_l] = ones_v


def _fox_select_tables():
    selq = np.zeros((FOX_PIECES, LANES, FOX_W), np.float32)
    selk = np.zeros((FOX_PIECES, LANES, FOX_W), np.float32)
    for piece in range(FOX_PIECES):
        for hd in range(FOX_HEADS):
            selq[piece, SC_CUMF + hd, hd * FOX_HEAD_DIM + piece] = 1.0
            selk[piece, SC_CUMF + hd, hd * FOX_HEAD_DIM + FOX_PIECES + piece] = 1.0
    return jnp.asarray(selq, BF16), jnp.asarray(selk, BF16)


def _proj_call(x, mod, gpre, w_b, cos_t, sin_t, poolw_b, pscale, convw, convb, sbias, alog, tril_b):
    bsz, seq, d = x.shape
    tm = TOK_TILE
    selq, selk = _fox_select_tables()
    tok = lambda w: pl.BlockSpec((1, tm, w), lambda b, j: (b, j, 0))
    bf = lambda w: jax.ShapeDtypeStruct((bsz, seq, w), BF16)
    fox_w = FOX_HEADS * FOX_SLAB
    return pl.pallas_call(
        _proj_kernel,
        grid=(bsz, seq // tm),
        in_specs=[tok(d),
                  pl.BlockSpec((1, 3 * N_SUBLAYERS, d), lambda b, j: (b, 0, 0)),
                  _resident(gpre.shape),
                  pl.BlockSpec((d, REGION_W), lambda b, j: (0, 0), pipeline_mode=pl.Buffered(1)),
                  pl.BlockSpec((tm, RET_DK), lambda b, j: (j, 0)),
                  pl.BlockSpec((tm, RET_DK), lambda b, j: (j, 0)),
                  _resident(poolw_b.shape), _resident(pscale.shape),
                  _resident(convw.shape), _resident(convb.shape),
                  _resident(sbias.shape), _resident(alog.shape), _resident(tril_b.shape),
                  _resident(selq.shape), _resident(selk.shape)],
        out_specs=[tok(RET_W), tok(RET_W), tok(RET_W), tok(POOL_W), tok(SSM_XBC),
                   tok(fox_w), pl.BlockSpec((1, fox_w, tm), lambda b, j: (b, 0, j)), tok(fox_w), tok(LANES),
                   pl.BlockSpec((1, LANES, tm), lambda b, j: (b, 0, j))],
        out_shape=[bf(RET_W), bf(RET_W), bf(RET_W), bf(POOL_W), bf(SSM_XBC),
                   bf(fox_w), jax.ShapeDtypeStruct((bsz, fox_w, seq), BF16), bf(fox_w),
                   jax.ShapeDtypeStruct((bsz, seq, LANES), F32),
                   jax.ShapeDtypeStruct((bsz, LANES, seq), F32)],
        scratch_shapes=[pltpu.VMEM((POOL_HALO + tm, POOL_W), F32),
                        pltpu.VMEM((CONV_HALO + tm, SSM_XBC), F32),
                        pltpu.VMEM((1, LANES), F32)],
        compiler_params=_params(2),
        name="mixer_proj",
    )(x, mod, gpre, w_b, cos_t, sin_t, poolw_b, pscale, convw, convb, sbias, alog, tril_b, selq, selk)


def _ret_kernel(q_ref, k_ref, v_ref, idec_ref, qdec_ref, kdec_ref, o_ref, state, *, chunk_decay):
    @pl.when(pl.program_id(1) == 0)
    def _():
        state[...] = jnp.zeros_like(state)

    tile = q_ref.shape[1]
    for hd in range(RET_HEADS):
        ls = slice(hd * RET_DK, (hd + 1) * RET_DK)
        for c0 in range(0, tile, RET_CHUNK):
            rows = slice(c0, c0 + RET_CHUNK)
            q = q_ref[0, rows, ls]
            k = k_ref[0, rows, ls]
            v = v_ref[0, rows, ls]
            st = state[hd]
            scores = _dot_nt(q, k) * idec_ref[hd]
            o = _dot(scores.astype(BF16), v) + _dot(q, st.astype(BF16)) * qdec_ref[hd]
            kd = (k.astype(F32) * kdec_ref[hd]).astype(BF16)
            state[hd] = chunk_decay[hd] * st + _dot_tn(kd, v)
            dev = o - jnp.mean(o, axis=-1, keepdims=True)
            var = jnp.mean(dev * dev, axis=-1, keepdims=True)
            o_ref[0, rows, ls] = (dev * lax.rsqrt(var + GN_EPS)).astype(BF16)


def _ret_tables():
    log_gamma = np.log1p(-np.exp2(-5.0 - np.arange(RET_HEADS, dtype=np.float64)))
    idx = np.arange(RET_CHUNK, dtype=np.float64)
    rel = idx[:, None] - idx[None, :]
    intra = np.where(rel >= 0, np.exp(log_gamma[:, None, None] * np.maximum(rel, 0.0)), 0.0)
    q_decay = np.exp(log_gamma[:, None] * (idx + 1.0))
    k_decay = np.exp(log_gamma[:, None] * (RET_CHUNK - 1.0 - idx))
    widen = lambda t: np.broadcast_to(t[:, :, None], (RET_HEADS, RET_CHUNK, RET_DK))
    chunk_decay = tuple(float(v) for v in np.exp(log_gamma * RET_CHUNK))
    return (jnp.asarray(intra, F32), jnp.asarray(widen(q_decay), F32),
            jnp.asarray(widen(k_decay), F32), chunk_decay)


def _ret_call(rq, rk, rv):
    bsz, seq, w = rq.shape
    idec, qdec, kdec, chunk_decay = _ret_tables()
    tok = pl.BlockSpec((1, SEQ_TILE, w), lambda b, j: (b, j, 0))
    return pl.pallas_call(
        functools.partial(_ret_kernel, chunk_decay=chunk_decay),
        grid=(bsz, seq // SEQ_TILE),
        in_specs=[tok, tok, tok, _resident(idec.shape), _resident(qdec.shape), _resident(kdec.shape)],
        out_specs=tok,
        out_shape=jax.ShapeDtypeStruct((bsz, seq, w), BF16),
        scratch_shapes=[pltpu.VMEM((RET_HEADS, RET_DK, RET_DV), F32)],
        compiler_params=_params(2),
        name="retention",
    )(rq, rk, rv, idec, qdec, kdec)


HEADS_PER_GROUP = SSM_HEADS // SSM_GROUPS
GROUP_W = HEADS_PER_GROUP * SSM_HEAD_DIM


def _ssd_kernel(xbc_ref, scal_ref, scalT_ref, ea_sel_ref, dt_sel_ref, dskip_ref, o_ref, state):
    @pl.when(pl.program_id(1) == 0)
    def _():
        state[...] = jnp.zeros_like(state)

    tile = xbc_ref.shape[1]
    cl = SSD_CHUNK
    lane = lax.broadcasted_iota(jnp.int32, (cl, LANES), 1)
    row_i = lax.broadcasted_iota(jnp.int32, (cl, cl), 0)
    col_i = lax.broadcasted_iota(jnp.int32, (cl, cl), 1)
    causal = col_i <= row_i
    low_half = lane < SSM_HEAD_DIM
    ea_sel = ea_sel_ref[...]
    dt_sel = dt_sel_ref[...]
    for c0 in range(0, tile, cl):
        rows = slice(c0, c0 + cl)
        sc = scal_ref[0, rows, :]
        in_acum = (lane >= SC_ACUM) & (lane < SC_CUMF)
        acum = jnp.where(in_acum, sc, 0.0)
        ea = jnp.exp(acum)
        dec_end = jnp.exp(acum[cl - 1:cl, :] - acum)
        dt = jnp.where(lane < SC_ACUM, sc, 0.0)
        ea_hi = ea.astype(BF16)
        ea_lo = (ea - ea_hi.astype(F32)).astype(BF16)
        ea_x = _dot(ea_hi, ea_sel) + _dot(ea_lo, ea_sel)
        w_x = _dot(dec_end.astype(BF16), ea_sel) * _dot(dt.astype(BF16), dt_sel)
        xs_b = xbc_ref[0, rows, 0:SSM_INNER]
        xs = xs_b.astype(F32)
        xw_b = (xs * w_x).astype(BF16)
        for g in range(SSM_GROUPS):
            gs = slice(g * GROUP_W, (g + 1) * GROUP_W)
            bm = xbc_ref[0, rows, SSM_INNER + g * SSM_STATE:SSM_INNER + (g + 1) * SSM_STATE]
            cm_lo = SSM_INNER + SSM_GROUPS * SSM_STATE + g * SSM_STATE
            cm = xbc_ref[0, rows, cm_lo:cm_lo + SSM_STATE]
            cb = _dot_nt(cm, bm)
            st = state[g]
            y_inter = _dot(cm, st.astype(BF16)) * ea_x[:, gs]
            state[g] = st * ea_x[cl - 1:cl, gs] + _dot_tn(bm, xw_b[:, gs])
            for pair in range(HEADS_PER_GROUP // 2):
                mats = []
                for hh in (2 * pair, 2 * pair + 1):
                    hd = g * HEADS_PER_GROUP + hh
                    a_col = jnp.sum(jnp.where(lane == SC_ACUM + hd, sc, 0.0), axis=-1, keepdims=True)
                    a_row = scalT_ref[0, SC_ACUM + hd:SC_ACUM + hd + 1, rows]
                    dt_row = scalT_ref[0, SC_DT + hd:SC_DT + hd + 1, rows]
                    lmat = jnp.exp(jnp.where(causal, a_col - a_row, NEG_BIG))
                    mats.append((lmat * dt_row * cb).astype(BF16))
                lo = g * GROUP_W + pair * LANES
                x_pair = xs_b[:, lo:lo + LANES]
                zero = jnp.zeros_like(x_pair)
                rhs = jnp.concatenate([jnp.where(low_half, x_pair, zero),
                                       jnp.where(low_half, zero, x_pair)], axis=0)
                y_pair = _dot(jnp.concatenate(mats, axis=1), rhs)
                y_pair = y_pair + y_inter[:, pair * LANES:(pair + 1) * LANES]
                y_pair = y_pair + dskip_ref[:, lo:lo + LANES] * xs[:, lo:lo + LANES]
                o_ref[0, rows, lo:lo + LANES] = y_pair.astype(BF16)


def _ssd_tables():
    ea_sel = np.zeros((LANES, SSM_INNER), np.float32)
    dt_sel = np.zeros((LANES, SSM_INNER), np.float32)
    for hd in range(SSM_HEADS):
        ea_sel[SC_ACUM + hd, hd * SSM_HEAD_DIM:(hd + 1) * SSM_HEAD_DIM] = 1.0
        dt_sel[SC_DT + hd, hd * SSM_HEAD_DIM:(hd + 1) * SSM_HEAD_DIM] = 1.0
    return jnp.asarray(ea_sel, BF16), jnp.asarray(dt_sel, BF16)


def _ssd_call(xbc, scal, scal_t, dskip_x):
    bsz, seq, _ = xbc.shape
    ea_sel, dt_sel = _ssd_tables()
    return pl.pallas_call(
        _ssd_kernel,
        grid=(bsz, seq // SEQ_TILE),
        in_specs=[pl.BlockSpec((1, SEQ_TILE, SSM_XBC), lambda b, j: (b, j, 0)),
                  pl.BlockSpec((1, SEQ_TILE, LANES), lambda b, j: (b, j, 0)),
                  pl.BlockSpec((1, LANES, SEQ_TILE), lambda b, j: (b, 0, j)),
                  _resident(ea_sel.shape), _resident(dt_sel.shape), _resident(dskip_x.shape)],
        out_specs=pl.BlockSpec((1, SEQ_TILE, SSM_INNER), lambda b, j: (b, j, 0)),
        out_shape=jax.ShapeDtypeStruct((bsz, seq, SSM_INNER), BF16),
        scratch_shapes=[pltpu.VMEM((SSM_GROUPS, SSM_STATE, GROUP_W), F32)],
        compiler_params=_params(2),
        name="ssd",
    )(xbc, scal, scal_t, ea_sel, dt_sel, dskip_x)


FOX_HALVES = 2
FOX_KV = FOX_TILE // 2


def _fox_kernel(q_ref, k_ref, v_ref, o_ref, m_s, acc_s, s_a, s_b):
    i = pl.program_id(2)
    tq = q_ref.shape[1]
    th = tq // FOX_HALVES
    m_s[...] = jnp.full_like(m_s, NEG_BIG)
    acc_s[...] = jnp.zeros_like(acc_s)

    def scores(s_ref, t, row0=0):
        ks = pl.multiple_of(t * FOX_KV, FOX_KV)
        s_ref[row0:, :] = _dot(q_ref[0, row0:, :], k_ref[0, :, pl.ds(ks, FOX_KV)])

    def consume(s_ref, t, diag):
        ks = pl.multiple_of(t * FOX_KV, FOX_KV)
        v = v_ref[0, pl.ds(ks, FOX_KV), :]
        for half in range(FOX_HALVES):
            rows = slice(half * th, (half + 1) * th)
            if diag is not None and diag * FOX_KV > (half + 1) * th - 1:
                continue
            s = s_ref[rows, :]
            if diag is not None and (diag + 1) * FOX_KV - 1 > half * th:
                row_i = half * th + lax.broadcasted_iota(jnp.int32, (th, FOX_KV), 0)
                col_i = diag * FOX_KV + lax.broadcasted_iota(jnp.int32, (th, FOX_KV), 1)
                s = jnp.where(col_i <= row_i, s, NEG_BIG)
            m_old = m_s[rows, :]
            m_new = jnp.maximum(m_old, jnp.max(s, axis=-1, keepdims=True))
            p = jnp.concatenate([jnp.exp(s[:, c0:c0 + LANES] - m_new)
                                 for c0 in range(0, FOX_KV, LANES)], axis=1)
            alpha = jnp.exp(m_old - m_new)
            pv = _dot(p.astype(BF16), v)
            for c0 in range(0, FOX_SLAB, LANES):
                acc_s[rows, c0:c0 + LANES] = alpha * acc_s[rows, c0:c0 + LANES] + pv[:, c0:c0 + LANES]
            m_s[rows, :] = m_new

    scores(s_a, 0)

    def body(t, carry):
        scores(s_b, 2 * t + 1)
        consume(s_a, 2 * t, None)
        scores(s_a, 2 * t + 2)
        consume(s_b, 2 * t + 1, None)
        return carry

    lax.fori_loop(0, i, body, 0)
    scores(s_b, 2 * i + 1, row0=FOX_KV)
    consume(s_a, 2 * i, 0)
    consume(s_b, 2 * i + 1, 1)
    acc = acc_s[...]
    o_ref[0] = (acc[:, :FOX_HEAD_DIM] / acc[:, FOX_HEAD_DIM:FOX_HEAD_DIM + 1]).astype(BF16)


def _fox_call(fq, fk, fv):
    bsz, seq, _ = fq.shape
    tq = FOX_TILE
    return pl.pallas_call(
        _fox_kernel,
        grid=(bsz, FOX_HEADS, seq // tq),
        in_specs=[pl.BlockSpec((1, tq, FOX_SLAB), lambda b, h, i: (b, i, h)),
                  pl.BlockSpec((1, FOX_SLAB, seq), lambda b, h, i: (b, h, 0)),
                  pl.BlockSpec((1, seq, FOX_SLAB), lambda b, h, i: (b, 0, h))],
        out_specs=pl.BlockSpec((1, tq, FOX_HEAD_DIM), lambda b, h, i: (b, i, h)),
        out_shape=jax.ShapeDtypeStruct((bsz, seq, FOX_W), BF16),
        scratch_shapes=[pltpu.VMEM((tq, LANES), F32), pltpu.VMEM((tq, FOX_SLAB), F32),
                        pltpu.VMEM((tq, FOX_KV), F32), pltpu.VMEM((tq, FOX_KV), F32)],
        compiler_params=_params(3),
        name="fox_attention",
    )(fq, fk, fv)


MG_RG = 0
MG_SZ = MG_RG + RET_W
MG_GL = MG_SZ + SSM_INNER
MG_WIDTH = MG_GL + N_BRANCH * D_MODEL


def _merge_kernel(x_ref, mod_ref, gpre_ref, gpost_ref, wg_ref, ret_ref, yb_ref, ssd_ref, fox_ref,
                  ssmnorm_ref, wro_ref, wpo_ref, wso_ref, wfo_ref, wout_ref, o_ref):
    x = x_ref[...]
    hb = _modulated(x, mod_ref, gpre_ref, 1).astype(BF16)
    rg = _dot(hb, wg_ref[:, MG_RG:MG_RG + RET_W])
    y_a = (_silu(rg) * ret_ref[...].astype(F32)).astype(BF16)
    sz = _dot(hb, wg_ref[:, MG_SZ:MG_SZ + SSM_INNER])
    y_c = _rms(ssd_ref[...].astype(F32) * _silu(sz), ssmnorm_ref[...]).astype(BF16)
    merged = None
    for br, (y, w_ref) in enumerate(((y_a, wro_ref), (yb_ref[...], wpo_ref),
                                     (y_c, wso_ref), (fox_ref[...], wfo_ref))):
        lo = MG_GL + br * D_MODEL
        gate = _sigmoid(_dot(hb, wg_ref[:, lo:lo + D_MODEL]))
        part = gate * _dot(y, w_ref[...])
        merged = part if merged is None else merged + part
    y_out = _dot(merged.astype(BF16), wout_ref[...])
    o_ref[...] = x + mod_ref[0, 5:6, :] * _rms(y_out, gpost_ref[1:2, :])


def _merge_call(x2d, mod, gpre, gpost, wg_b, ret, yb, ssd, fox, ssmnorm, wro, wpo, wso, wfo, wout, seq):
    n, d = x2d.shape
    tm = TOK_TILE
    tiles_per_seq = seq // tm
    tok = lambda w: pl.BlockSpec((tm, w), lambda i: (i, 0))
    return pl.pallas_call(
        _merge_kernel,
        grid=(n // tm,),
        in_specs=[tok(d),
                  pl.BlockSpec((1, 3 * N_SUBLAYERS, d), lambda i: (i // tiles_per_seq, 0, 0)),
                  _resident(gpre.shape), _resident(gpost.shape),
                  pl.BlockSpec((d, REGION_W), lambda i: (0, 1), pipeline_mode=pl.Buffered(1)),
                  tok(RET_W), tok(POOL_W), tok(SSM_INNER), tok(FOX_W),
                  _resident(ssmnorm.shape), _resident(wro.shape), _resident(wpo.shape),
                  _resident(wso.shape), _resident(wfo.shape), _resident(wout.shape)],
        out_specs=tok(d),
        out_shape=jax.ShapeDtypeStruct((n, d), F32),
        compiler_params=_params(1),
        name="mixer_merge",
    )(x2d, mod, gpre, gpost, wg_b, ret.reshape(n, -1), yb.reshape(n, -1), ssd.reshape(n, -1),
      fox.reshape(n, -1), ssmnorm, wro, wpo, wso, wfo, wout)


def _rotary_tables(seq):
    half = RET_DK // 2
    inv = ROPE_BASE ** (-jnp.arange(half, dtype=F32) / half)
    ang = jnp.arange(seq, dtype=F32)[:, None] * inv[None, :]
    cos = jnp.cos(ang)
    sin = jnp.sin(ang)
    return jnp.concatenate([cos, cos], axis=-1), jnp.concatenate([-sin, sin], axis=-1)


def _place(vals, lane0):
    return jnp.zeros((1, LANES), F32).at[0, lane0:lane0 + vals.shape[0]].set(vals.astype(F32))


def kernel(x, c, w_ada, b_ada, norm_pre, norm_post, w_ffn_in, w_ffn_out, w_in, b_forget, pool_w, pool_scale, conv_w, conv_b, dt_bias, a_log, d_skip, ssm_norm, w_ret_out, w_pool_out, w_ssm_out, w_fox_out, w_out):
    bsz, seq, d = x.shape
    depth = w_ada.shape[0]
    n = bsz * seq
    mods = _ada_call(c, w_ada, b_ada)
    cos_t, sin_t = _rotary_tables(seq)
    tril_b = jnp.asarray(np.tril(np.ones((SSD_CHUNK, SSD_CHUNK), np.float32)), BF16)
    for i in range(depth):
        mod = mods[i].reshape(bsz, 3 * N_SUBLAYERS, d)
        gpre, gpost = norm_pre[i], norm_post[i]
        x2d = _ffn_call(x.reshape(n, d), mod, gpre, gpost, w_ffn_in[i, 0].astype(BF16),
                        w_ffn_out[i, 0].astype(BF16), 0, seq)

        w_regrouped = _regroup_call(w_in, i)
        w_proj = w_gate = w_regrouped
        sbias = _place(dt_bias[i], SC_DT) + _place(dt_bias[i], SC_ACUM) + _place(b_forget[i], SC_CUMF)
        alog = _place(a_log[i], SC_ACUM)
        dskip_x = jnp.repeat(d_skip[i].astype(F32), SSM_HEAD_DIM)[None, :]

        rq, rk, rv, yb, xbc, fq, fk, fv, scal, scal_t = _proj_call(
            x2d.reshape(bsz, seq, d), mod, gpre, w_proj, cos_t, sin_t, pool_w[i].astype(BF16),
            pool_scale[i][None, :], conv_w[i], conv_b[i][None, :], sbias, alog, tril_b)
        ret = _ret_call(rq, rk, rv)
        ssd = _ssd_call(xbc, scal, scal_t, dskip_x)
        fox = _fox_call(fq, fk, fv)
        x2d = _merge_call(x2d, mod, gpre, gpost, w_gate, ret, yb, ssd, fox, ssm_norm[i][None, :],
                          w_ret_out[i].astype(BF16), w_pool_out[i].astype(BF16),
                          w_ssm_out[i].astype(BF16), w_fox_out[i].astype(BF16),
                          w_out[i].astype(BF16), seq)
        x2d = _ffn_call(x2d, mod, gpre, gpost, w_ffn_in[i, 1].astype(BF16),
                        w_ffn_out[i, 1].astype(BF16), 2, seq)
        x = x2d.reshape(bsz, seq, d)
    return x
```

```python
import functools

import jax
import jax.numpy as jnp
import numpy as np
from jax import lax
from jax.experimental import pallas as pl
from jax.experimental.pallas import tpu as pltpu

F32 = jnp.float32
BF16 = jnp.bfloat16

D_MODEL = 1024
RET_HEADS = 4
RET_DK = 128
RET_DV = 128
RET_W = RET_HEADS * RET_DK
ROPE_BASE = 10000.0
POOL_WINDOWS = (2, 4, 8, 16)
POOL_GROUPS = 4
POOL_GROUP_DIM = 128
POOL_W = POOL_GROUPS * POOL_GROUP_DIM
SSM_HEADS = 16
SSM_HEAD_DIM = 64
SSM_INNER = SSM_HEADS * SSM_HEAD_DIM
SSM_GROUPS = 2
SSM_STATE = 128
SSM_CONV = 4
SSM_XBC = SSM_INNER + 2 * SSM_GROUPS * SSM_STATE
FOX_HEADS = 4
FOX_HEAD_DIM = 128
FOX_W = FOX_HEADS * FOX_HEAD_DIM
N_BRANCH = 4
D_FF = 2816
N_SUBLAYERS = 3
RMS_EPS = 1e-6
GN_EPS = 1e-5
IN_SIZES = (RET_W, RET_W, RET_W, RET_W, POOL_W, SSM_INNER, SSM_XBC, SSM_HEADS,
            FOX_W, FOX_W, FOX_W, FOX_HEADS, N_BRANCH * D_MODEL)

LANES = 128
VMEM_LIMIT = 56 * 1024 * 1024
ADA_TN = 1152
TOK_TILE = 512
MXU_DIM = 256
FFN_CHUNK = 6 * MXU_DIM
SEQ_TILE = 512
RET_CHUNK = 256
SSD_CHUNK = 128
FOX_TILE = 1024
LOG2_E = float(np.log2(np.e))
NEG_BIG = -1e30

SC_DT = 0
SC_ACUM = 16
SC_CUMF = 32
FOX_SLAB = 2 * FOX_HEAD_DIM
FOX_PIECES = 3


def _sigmoid(v):
    return 1.0 / (1.0 + jnp.exp(-v))


def _silu(v):
    return v * _sigmoid(v)


def _rms(v, gain):
    return v * lax.rsqrt(jnp.mean(v * v, axis=-1, keepdims=True) + RMS_EPS) * gain


def _modulated(x, mod_ref, gpre_ref, sub):
    shift = mod_ref[0, 3 * sub:3 * sub + 1, :]
    scale = mod_ref[0, 3 * sub + 1:3 * sub + 2, :]
    return _rms(x, gpre_ref[sub:sub + 1, :]) * (1.0 + scale) + shift


def _dot(a, b):
    return jnp.dot(a, b, preferred_element_type=F32)


def _dot_nt(a, b):
    return lax.dot_general(a, b, (((1,), (1,)), ((), ())), preferred_element_type=F32)


def _dot_tn(a, b):
    return lax.dot_general(a, b, (((0,), (0,)), ((), ())), preferred_element_type=F32)


def _resident(shape):
    nd = len(shape)
    return pl.BlockSpec(shape, lambda *_: (0,) * nd, pipeline_mode=pl.Buffered(1))


def _params(n_axes):
    return pltpu.CompilerParams(dimension_semantics=("arbitrary",) * n_axes,
                                vmem_limit_bytes=VMEM_LIMIT)


def _ada_kernel(c_ref, w_ref, b_ref, o_ref):
    sc = _silu(c_ref[...]).astype(BF16)
    o_ref[0] = _dot(sc, w_ref[0].astype(BF16)) + b_ref[0]


def _ada_call(c, w_ada, b_ada):
    depth, d, width = w_ada.shape
    bsz = c.shape[0]
    return pl.pallas_call(
        _ada_kernel,
        grid=(depth, width // ADA_TN),
        in_specs=[pl.BlockSpec((bsz, d), lambda l, n: (0, 0)),
                  pl.BlockSpec((1, d, ADA_TN), lambda l, n: (l, 0, n)),
                  pl.BlockSpec((1, 1, ADA_TN), lambda l, n: (l, 0, n))],
        out_specs=pl.BlockSpec((1, bsz, ADA_TN), lambda l, n: (l, 0, n)),
        out_shape=jax.ShapeDtypeStruct((depth, bsz, width), F32),
        compiler_params=_params(2),
        name="adaln",
    )(c, w_ada, b_ada.reshape(depth, 1, width))


def _ffn_kernel(x_ref, mod_ref, gpre_ref, gpost_ref, win_ref, wout_ref, o_ref, *, sub):
    x = x_ref[...]
    hb = _modulated(x, mod_ref, gpre_ref, sub).astype(BF16)
    y = None
    for lo in range(0, D_FF, FFN_CHUNK):
        hi = min(lo + FFN_CHUNK, D_FF)
        g = _dot(hb, win_ref[:, lo:hi])
        u = _dot(hb, win_ref[:, D_FF + lo:D_FF + hi])
        part = _dot((_silu(g) * u).astype(BF16), wout_ref[lo:hi, :])
        y = part if y is None else y + part
    gate = mod_ref[0, 3 * sub + 2:3 * sub + 3, :]
    o_ref[...] = x + (0.5 * gate) * _rms(y, gpost_ref[sub:sub + 1, :])


def _ffn_call(x2d, mod, gpre, gpost, w_in_b, w_out_b, sub, seq):
    n, d = x2d.shape
    tiles_per_seq = seq // TOK_TILE
    return pl.pallas_call(
        functools.partial(_ffn_kernel, sub=sub),
        grid=(n // TOK_TILE,),
        in_specs=[pl.BlockSpec((TOK_TILE, d), lambda i: (i, 0)),
                  pl.BlockSpec((1, 3 * N_SUBLAYERS, d), lambda i: (i // tiles_per_seq, 0, 0)),
                  _resident(gpre.shape), _resident(gpost.shape),
                  _resident(w_in_b.shape), _resident(w_out_b.shape)],
        out_specs=pl.BlockSpec((TOK_TILE, d), lambda i: (i, 0)),
        out_shape=jax.ShapeDtypeStruct((n, d), F32),
        compiler_params=_params(1),
        name="ffn",
    )(x2d, mod, gpre, gpost, w_in_b, w_out_b)


IN_OFFSETS = tuple(int(v) for v in np.cumsum((0,) + IN_SIZES[:-1]))
(IN_RQ, IN_RK, IN_RV, IN_RG, IN_PU, IN_SZ, IN_XBC, IN_DT, IN_FQ, IN_FK, IN_FV, IN_FF, IN_GL) = IN_OFFSETS
REGION_W = (RET_W + SSM_INNER + N_BRANCH * D_MODEL)


def _regroup_plan():
    shifts = sorted({off % LANES for off in IN_OFFSETS})
    kind_of = {s: k for k, s in enumerate(shifts)}
    scalar_kind, zero_kind = len(shifts), len(shifts) + 1
    place = np.zeros((len(shifts) + 2, 2 * LANES, LANES), np.float32)
    for s, k in kind_of.items():
        place[k, s + np.arange(LANES), np.arange(LANES)] = 1.0
    dt_lane, ff_lane = IN_DT % LANES, IN_FF % LANES
    assert dt_lane + SSM_HEADS <= LANES and ff_lane + FOX_HEADS <= LANES
    for hd in range(SSM_HEADS):
        place[scalar_kind, dt_lane + hd, SC_DT + hd] = 1.0
        place[scalar_kind, dt_lane + hd, SC_ACUM + hd] = 1.0
    for hd in range(FOX_HEADS):
        place[scalar_kind, LANES + ff_lane + hd, SC_CUMF + hd] = 1.0
    lo, hi, kind = [], [], []

    def span(off, width):
        for c in range(off, off + width, LANES):
            lo.append(c // LANES)
            hi.append(c // LANES + (1 if c % LANES else 0))
            kind.append(kind_of[c % LANES])

    proj = ((IN_RQ, RET_W), (IN_RK, RET_W), (IN_RV, RET_W), (IN_PU, POOL_W), (IN_XBC, SSM_XBC),
            (IN_FQ, FOX_W), (IN_FK, FOX_W), (IN_FV, FOX_W))
    for off, width in proj:
        span(off, width)
    lo.append(IN_DT // LANES); hi.append(IN_FF // LANES); kind.append(scalar_kind)
    while len(lo) < REGION_W // LANES:
        lo.append(0); hi.append(0); kind.append(zero_kind)
    for off, width in ((IN_RG, RET_W), (IN_SZ, SSM_INNER), (IN_GL, N_BRANCH * D_MODEL)):
        span(off, width)
    assert len(lo) == 2 * REGION_W // LANES
    as_i32 = lambda v: jnp.asarray(np.asarray(v, np.int32))
    need = place[:, LANES:, :].any(axis=2).astype(np.float32)[:, None, :]
    return as_i32(lo), as_i32(hi), as_i32(kind), jnp.asarray(place, BF16), jnp.asarray(need, F32)


def _regroup_kernel(lo_ref, hi_ref, kind_ref, wlo_ref, whi_ref, place_ref, need_ref, o_ref):
    del lo_ref, hi_ref
    kind = kind_ref[pl.program_id(0)]
    second = whi_ref[0]
    second = jnp.where(need_ref[kind] > 0.0, second, jnp.zeros_like(second))
    pair = jnp.concatenate([wlo_ref[0], second], axis=1).astype(BF16)
    o_ref[...] = _dot(pair, place_ref[kind]).astype(BF16)


def _regroup_call(w_in, layer):
    _, d, _ = w_in.shape
    lo, hi, kind, place, need = _regroup_plan()
    n_blocks = lo.shape[0]
    grid_spec = pltpu.PrefetchScalarGridSpec(
        num_scalar_prefetch=3,
        grid=(n_blocks,),
        in_specs=[pl.BlockSpec((1, d, LANES), lambda o, lo, hi, kind: (layer, 0, lo[o])),
                  pl.BlockSpec((1, d, LANES), lambda o, lo, hi, kind: (layer, 0, hi[o])),
                  pl.BlockSpec(place.shape, lambda o, lo, hi, kind: (0, 0, 0)),
                  pl.BlockSpec(need.shape, lambda o, lo, hi, kind: (0, 0, 0))],
        out_specs=pl.BlockSpec((d, LANES), lambda o, lo, hi, kind: (0, o)))
    return pl.pallas_call(
        _regroup_kernel,
        grid_spec=grid_spec,
        out_shape=jax.ShapeDtypeStruct((d, n_blocks * LANES), BF16),
        compiler_params=_params(1),
        name="regroup_w_in",
    )(lo, hi, kind, w_in, w_in, place, need)


PJ_RQ = 0
PJ_RK = PJ_RQ + RET_W
PJ_RV = PJ_RK + RET_W
PJ_PU = PJ_RV + RET_W
PJ_XBC = PJ_PU + POOL_W
PJ_FQ = PJ_XBC + SSM_XBC
PJ_FK = PJ_FQ + FOX_W
PJ_FV = PJ_FK + FOX_W
PJ_SC = PJ_FV + FOX_W
PJ_WIDTH = PJ_SC + LANES
POOL_HALO = 16
CONV_HALO = 8


def _cumsum_rows(tril_b, s):
    p1 = s.astype(BF16)
    r1 = s - p1.astype(F32)
    p2 = r1.astype(BF16)
    p3 = (r1 - p2.astype(F32)).astype(BF16)
    return _dot(tril_b, p1) + _dot(tril_b, p2) + _dot(tril_b, p3)


def _proj_kernel(x_ref, mod_ref, gpre_ref, w_ref, cos_ref, sin_ref, poolw_ref, pscale_ref,
                 convw_ref, convb_ref, sbias_ref, alog_ref, tril_ref, selq_ref, selk_ref,
                 rq_ref, rk_ref, rv_ref, yb_ref, xbc_ref, fq_ref, fk_ref, fv_ref, scal_ref, scalT_ref,
                 pbuf, cbuf, carry):
    j = pl.program_id(1)
    tm = x_ref.shape[1]
    hb = _modulated(x_ref[0], mod_ref, gpre_ref, 1).astype(BF16)

    cos = cos_ref[...]
    sin = sin_ref[...]
    for base, ref, scl in ((PJ_RQ, rq_ref, None), (PJ_RK, rk_ref, RET_DK ** -0.5)):
        t = _dot(hb, w_ref[:, base:base + RET_W])
        for hd in range(RET_HEADS):
            th = t[:, hd * RET_DK:(hd + 1) * RET_DK]
            r = th * cos + pltpu.roll(th, RET_DK // 2, 1) * sin
            if scl is not None:
                r = r * scl
            ref[0, :, hd * RET_DK:(hd + 1) * RET_DK] = r.astype(BF16)
    rv_ref[0] = _dot(hb, w_ref[:, PJ_RV:PJ_RV + RET_W]).astype(BF16)

    pu = _dot(hb, w_ref[:, PJ_PU:PJ_PU + POOL_W])

    @pl.when(j == 0)
    def _():
        pbuf[0:POOL_HALO, :] = jnp.zeros((POOL_HALO, POOL_W), F32)
        cbuf[0:CONV_HALO, :] = jnp.zeros((CONV_HALO, SSM_XBC), F32)
        carry[...] = jnp.zeros_like(carry)

    @pl.when(j > 0)
    def _():
        pbuf[0:POOL_HALO, :] = pbuf[tm:tm + POOL_HALO, :]
        cbuf[0:CONV_HALO, :] = cbuf[tm:tm + CONV_HALO, :]

    pbuf[POOL_HALO:POOL_HALO + tm, :] = pu
    pos = j * tm + lax.broadcasted_iota(jnp.int32, (tm, POOL_GROUP_DIM), 0)
    for g, win in enumerate(POOL_WINDOWS):
        ls = slice(g * POOL_GROUP_DIM, (g + 1) * POOL_GROUP_DIM)
        cur = pu[:, ls]
        acc = cur
        for s in range(1, win):
            acc = acc + pbuf[POOL_HALO - s:POOL_HALO - s + tm, ls]
        count = jnp.minimum(pos + 1, win).astype(F32)
        pooled = acc / count - cur
        mixed = _dot(pooled.astype(BF16), poolw_ref[g]) * pscale_ref[:, ls]
        yb_ref[0, :, ls] = mixed.astype(BF16)

    xr = _dot(hb, w_ref[:, PJ_XBC:PJ_XBC + SSM_XBC])
    cbuf[CONV_HALO:CONV_HALO + tm, :] = xr
    conv = xr * convw_ref[SSM_CONV - 1:SSM_CONV, :] + convb_ref[...]
    for k in range(SSM_CONV - 1):
        off = CONV_HALO - (SSM_CONV - 1) + k
        conv = conv + cbuf[off:off + tm, :] * convw_ref[k:k + 1, :]
    xbc_ref[0] = _silu(conv).astype(BF16)

    fq = _dot(hb, w_ref[:, PJ_FQ:PJ_FQ + FOX_W]) * (FOX_HEAD_DIM ** -0.5 * LOG2_E)
    fk = _dot(hb, w_ref[:, PJ_FK:PJ_FK + FOX_W])
    fv = _dot(hb, w_ref[:, PJ_FV:PJ_FV + FOX_W])
    for hd in range(FOX_HEADS):
        src_l = slice(hd * FOX_HEAD_DIM, (hd + 1) * FOX_HEAD_DIM)
        dst_l = slice(hd * FOX_SLAB, hd * FOX_SLAB + FOX_HEAD_DIM)
        fq_ref[0, :, dst_l] = fq[:, src_l].astype(BF16)
        fk_ref[0, dst_l, :] = fk[:, src_l].T.astype(BF16)
        fv_ref[0, :, dst_l] = fv[:, src_l].astype(BF16)

    z = _dot(hb, w_ref[:, PJ_SC:PJ_SC + LANES]) + sbias_ref[...]
    tail = jnp.log1p(jnp.exp(-jnp.abs(z)))
    softplus = jnp.maximum(z, 0.0) + tail
    log_sig = jnp.minimum(z, 0.0) - tail
    lane = lax.broadcasted_iota(jnp.int32, (tm, LANES), 1)
    a_row = -jnp.exp(alog_ref[...])
    in_acum = (lane >= SC_ACUM) & (lane < SC_CUMF)
    in_cumf = (lane >= SC_CUMF) & (lane < SC_CUMF + FOX_HEADS)
    src = jnp.where(in_acum, softplus * a_row, jnp.where(in_cumf, log_sig, 0.0))
    tril_b = tril_ref[...]
    run = carry[...]
    lane_c = lax.broadcasted_iota(jnp.int32, (SSD_CHUNK, LANES), 1)
    for c0 in range(0, tm, SSD_CHUNK):
        rows = slice(c0, c0 + SSD_CHUNK)
        local = _cumsum_rows(tril_b, src[rows, :])
        total = local + run
        scal_ref[0, rows, :] = jnp.where(lane_c < SC_ACUM, softplus[rows, :],
                                         jnp.where(lane_c < SC_CUMF, local, total))
        run = total[SSD_CHUNK - 1:SSD_CHUNK, :]
    carry[...] = run
    sc_all = scal_ref[0]
    scalT_ref[0] = sc_all.T

    bias = sc_all * LOG2_E
    c1 = bias.astype(BF16)
    r1 = bias - c1.astype(F32)
    c2 = r1.astype(BF16)
    c3 = (r1 - c2.astype(F32)).astype(BF16)
    lane_h = lax.broadcasted_iota(jnp.int32, (tm, FOX_HEAD_DIM), 1)
    ones_q = jnp.where((lane_h >= FOX_PIECES) & (lane_h < 2 * FOX_PIECES), 1.0, 0.0)
    ones_k = jnp.where(lane_h < FOX_PIECES, 1.0, 0.0)
    ones_v = jnp.where(lane_h == 0, 1.0, 0.0).astype(BF16)
    bias_q = _dot(c1, selq_ref[0]) + _dot(c2, selq_ref[1]) + _dot(c3, selq_ref[2])
    bias_k = _dot(c1, selk_ref[0]) + _dot(c2, selk_ref[1]) + _dot(c3, selk_ref[2])
    for hd in range(FOX_HEADS):
        src_l = slice(hd * FOX_HEAD_DIM, (hd + 1) * FOX_HEAD_DIM)
        dst_l = slice(hd * FOX_SLAB + FOX_HEAD_DIM, (hd + 1) * FOX_SLAB)
        fq_ref[0, :, dst_l] = (bias_q[:, src_l] + ones_q).astype(BF16)
        fk_ref[0, dst_l, :] = (ones_k - bias_k[:, src_l]).T.astype(BF16)
        fv_ref[0, :, dst_l] = ones_v


def _fox_select_tables():
    selq = np.zeros((FOX_PIECES, LANES, FOX_W), np.float32)
    selk = np.zeros((FOX_PIECES, LANES, FOX_W), np.float32)
    for piece in range(FOX_PIECES):
        for hd in range(FOX_HEADS):
            selq[piece, SC_CUMF + hd, hd * FOX_HEAD_DIM + piece] = 1.0
            selk[piece, SC_CUMF + hd, hd * FOX_HEAD_DIM + FOX_PIECES + piece] = 1.0
    return jnp.asarray(selq, BF16), jnp.asarray(selk, BF16)


def _proj_call(x, mod, gpre, w_b, cos_t, sin_t, poolw_b, pscale, convw, convb, sbias, alog, tril_b):
    bsz, seq, d = x.shape
    tm = TOK_TILE
    selq, selk = _fox_select_tables()
    tok = lambda w: pl.BlockSpec((1, tm, w), lambda b, j: (b, j, 0))
    bf = lambda w: jax.ShapeDtypeStruct((bsz, seq, w), BF16)
    fox_w = FOX_HEADS * FOX_SLAB
    return pl.pallas_call(
        _proj_kernel,
        grid=(bsz, seq // tm),
        in_specs=[tok(d),
                  pl.BlockSpec((1, 3 * N_SUBLAYERS, d), lambda b, j: (b, 0, 0)),
                  _resident(gpre.shape),
                  pl.BlockSpec((d, REGION_W), lambda b, j: (0, 0), pipeline_mode=pl.Buffered(1)),
                  pl.BlockSpec((tm, RET_DK), lambda b, j: (j, 0)),
                  pl.BlockSpec((tm, RET_DK), lambda b, j: (j, 0)),
                  _resident(poolw_b.shape), _resident(pscale.shape),
                  _resident(convw.shape), _resident(convb.shape),
                  _resident(sbias.shape), _resident(alog.shape), _resident(tril_b.shape),
                  _resident(selq.shape), _resident(selk.shape)],
        out_specs=[tok(RET_W), tok(RET_W), tok(RET_W), tok(POOL_W), tok(SSM_XBC),
                   tok(fox_w), pl.BlockSpec((1, fox_w, tm), lambda b, j: (b, 0, j)), tok(fox_w), tok(LANES),
                   pl.BlockSpec((1, LANES, tm), lambda b, j: (b, 0, j))],
        out_shape=[bf(RET_W), bf(RET_W), bf(RET_W), bf(POOL_W), bf(SSM_XBC),
                   bf(fox_w), jax.ShapeDtypeStruct((bsz, fox_w, seq), BF16), bf(fox_w),
                   jax.ShapeDtypeStruct((bsz, seq, LANES), F32),
                   jax.ShapeDtypeStruct((bsz, LANES, seq), F32)],
        scratch_shapes=[pltpu.VMEM((POOL_HALO + tm, POOL_W), F32),
                        pltpu.VMEM((CONV_HALO + tm, SSM_XBC), F32),
                        pltpu.VMEM((1, LANES), F32)],
        compiler_params=_params(2),
        name="mixer_proj",
    )(x, mod, gpre, w_b, cos_t, sin_t, poolw_b, pscale, convw, convb, sbias, alog, tril_b, selq, selk)


def _ret_kernel(q_ref, k_ref, v_ref, idec_ref, qdec_ref, kdec_ref, o_ref, state, *, chunk_decay):
    @pl.when(pl.program_id(1) == 0)
    def _():
        state[...] = jnp.zeros_like(state)

    tile = q_ref.shape[1]
    for hd in range(RET_HEADS):
        ls = slice(hd * RET_DK, (hd + 1) * RET_DK)
        for c0 in range(0, tile, RET_CHUNK):
            rows = slice(c0, c0 + RET_CHUNK)
            q = q_ref[0, rows, ls]
            k = k_ref[0, rows, ls]
            v = v_ref[0, rows, ls]
            st = state[hd]
            scores = _dot_nt(q, k) * idec_ref[hd]
            o = _dot(scores.astype(BF16), v) + _dot(q, st.astype(BF16)) * qdec_ref[hd]
            kd = (k.astype(F32) * kdec_ref[hd]).astype(BF16)
            state[hd] = chunk_decay[hd] * st + _dot_tn(kd, v)
            dev = o - jnp.mean(o, axis=-1, keepdims=True)
            var = jnp.mean(dev * dev, axis=-1, keepdims=True)
            o_ref[0, rows, ls] = (dev * lax.rsqrt(var + GN_EPS)).astype(BF16)


def _ret_tables():
    log_gamma = np.log1p(-np.exp2(-5.0 - np.arange(RET_HEADS, dtype=np.float64)))
    idx = np.arange(RET_CHUNK, dtype=np.float64)
    rel = idx[:, None] - idx[None, :]
    intra = np.where(rel >= 0, np.exp(log_gamma[:, None, None] * np.maximum(rel, 0.0)), 0.0)
    q_decay = np.exp(log_gamma[:, None] * (idx + 1.0))
    k_decay = np.exp(log_gamma[:, None] * (RET_CHUNK - 1.0 - idx))
    widen = lambda t: np.broadcast_to(t[:, :, None], (RET_HEADS, RET_CHUNK, RET_DK))
    chunk_decay = tuple(float(v) for v in np.exp(log_gamma * RET_CHUNK))
    return (jnp.asarray(intra, F32), jnp.asarray(widen(q_decay), F32),
            jnp.asarray(widen(k_decay), F32), chunk_decay)


def _ret_call(rq, rk, rv):
    bsz, seq, w = rq.shape
    idec, qdec, kdec, chunk_decay = _ret_tables()
    tok = pl.BlockSpec((1, SEQ_TILE, w), lambda b, j: (b, j, 0))
    return pl.pallas_call(
        functools.partial(_ret_kernel, chunk_decay=chunk_decay),
        grid=(bsz, seq // SEQ_TILE),
        in_specs=[tok, tok, tok, _resident(idec.shape), _resident(qdec.shape), _resident(kdec.shape)],
        out_specs=tok,
        out_shape=jax.ShapeDtypeStruct((bsz, seq, w), BF16),
        scratch_shapes=[pltpu.VMEM((RET_HEADS, RET_DK, RET_DV), F32)],
        compiler_params=_params(2),
        name="retention",
    )(rq, rk, rv, idec, qdec, kdec)


HEADS_PER_GROUP = SSM_HEADS // SSM_GROUPS
GROUP_W = HEADS_PER_GROUP * SSM_HEAD_DIM


def _ssd_kernel(xbc_ref, scal_ref, scalT_ref, ea_sel_ref, dt_sel_ref, dskip_ref, o_ref, state):
    @pl.when(pl.program_id(1) == 0)
    def _():
        state[...] = jnp.zeros_like(state)

    tile = xbc_ref.shape[1]
    cl = SSD_CHUNK
    lane = lax.broadcasted_iota(jnp.int32, (cl, LANES), 1)
    row_i = lax.broadcasted_iota(jnp.int32, (cl, cl), 0)
    col_i = lax.broadcasted_iota(jnp.int32, (cl, cl), 1)
    causal = col_i <= row_i
    low_half = lane < SSM_HEAD_DIM
    ea_sel = ea_sel_ref[...]
    dt_sel = dt_sel_ref[...]
    for c0 in range(0, tile, cl):
        rows = slice(c0, c0 + cl)
        sc = scal_ref[0, rows, :]
        sc2 = sc * LOG2_E
        in_acum = (lane >= SC_ACUM) & (lane < SC_CUMF)
        acum = jnp.where(in_acum, sc, 0.0)
        ea = jnp.exp(acum)
        dec_end = jnp.exp(acum[cl - 1:cl, :] - acum)
        dt = jnp.where(lane < SC_ACUM, sc, 0.0)
        ea_hi = ea.astype(BF16)
        ea_lo = (ea - ea_hi.astype(F32)).astype(BF16)
        ea_x = _dot(ea_hi, ea_sel) + _dot(ea_lo, ea_sel)
        w_x = _dot(dec_end.astype(BF16), ea_sel) * _dot(dt.astype(BF16), dt_sel)
        xs_b = xbc_ref[0, rows, 0:SSM_INNER]
        xs = xs_b.astype(F32)
        xw_b = (xs * w_x).astype(BF16)
        for g in range(SSM_GROUPS):
            gs = slice(g * GROUP_W, (g + 1) * GROUP_W)
            bm = xbc_ref[0, rows, SSM_INNER + g * SSM_STATE:SSM_INNER + (g + 1) * SSM_STATE]
            cm_lo = SSM_INNER + SSM_GROUPS * SSM_STATE + g * SSM_STATE
            cm = xbc_ref[0, rows, cm_lo:cm_lo + SSM_STATE]
            cb = _dot_nt(cm, bm)
            st = state[g]
            y_inter = _dot(cm, st.astype(BF16)) * ea_x[:, gs]
            state[g] = st * ea_x[cl - 1:cl, gs] + _dot_tn(bm, xw_b[:, gs])
            for pair in range(HEADS_PER_GROUP // 2):
                mats = []
                for hh in (2 * pair, 2 * pair + 1):
                    hd = g * HEADS_PER_GROUP + hh
                    a_col = jnp.broadcast_to(sc2[:, SC_ACUM + hd:SC_ACUM + hd + 1], (cl, cl))
                    a_row = scalT_ref[0, SC_ACUM + hd:SC_ACUM + hd + 1, rows] * LOG2_E
                    dt_row = scalT_ref[0, SC_DT + hd:SC_DT + hd + 1, rows]
                    lmat = jnp.exp2(jnp.where(causal, a_col - (a_row - jnp.log2(dt_row)), NEG_BIG))
                    mats.append((lmat * cb).astype(BF16))
                lo = g * GROUP_W + pair * LANES
                x_pair = xs_b[:, lo:lo + LANES]
                zero = jnp.zeros_like(x_pair)
                rhs = jnp.concatenate([jnp.where(low_half, x_pair, zero),
                                       jnp.where(low_half, zero, x_pair)], axis=0)
                y_pair = _dot(jnp.concatenate(mats, axis=1), rhs)
                y_pair = y_pair + y_inter[:, pair * LANES:(pair + 1) * LANES]
                y_pair = y_pair + dskip_ref[:, lo:lo + LANES] * xs[:, lo:lo + LANES]
                o_ref[0, rows, lo:lo + LANES] = y_pair.astype(BF16)


def _ssd_tables():
    ea_sel = np.zeros((LANES, SSM_INNER), np.float32)
    dt_sel = np.zeros((LANES, SSM_INNER), np.float32)
    for hd in range(SSM_HEADS):
        ea_sel[SC_ACUM + hd, hd * SSM_HEAD_DIM:(hd + 1) * SSM_HEAD_DIM] = 1.0
        dt_sel[SC_DT + hd, hd * SSM_HEAD_DIM:(hd + 1) * SSM_HEAD_DIM] = 1.0
    return jnp.asarray(ea_sel, BF16), jnp.asarray(dt_sel, BF16)


def _ssd_call(xbc, scal, scal_t, dskip_x):
    bsz, seq, _ = xbc.shape
    ea_sel, dt_sel = _ssd_tables()
    return pl.pallas_call(
        _ssd_kernel,
        grid=(bsz, seq // SEQ_TILE),
        in_specs=[pl.BlockSpec((1, SEQ_TILE, SSM_XBC), lambda b, j: (b, j, 0)),
                  pl.BlockSpec((1, SEQ_TILE, LANES), lambda b, j: (b, j, 0)),
                  pl.BlockSpec((1, LANES, SEQ_TILE), lambda b, j: (b, 0, j)),
                  _resident(ea_sel.shape), _resident(dt_sel.shape), _resident(dskip_x.shape)],
        out_specs=pl.BlockSpec((1, SEQ_TILE, SSM_INNER), lambda b, j: (b, j, 0)),
        out_shape=jax.ShapeDtypeStruct((bsz, seq, SSM_INNER), BF16),
        scratch_shapes=[pltpu.VMEM((SSM_GROUPS, SSM_STATE, GROUP_W), F32)],
        compiler_params=_params(2),
        name="ssd",
    )(xbc, scal, scal_t, ea_sel, dt_sel, dskip_x)


FOX_HALVES = 2
FOX_KV = FOX_TILE // 2


def _fox_kernel(q_ref, k_ref, v_ref, o_ref, m_s, acc_s, s_a, s_b):
    tq = FOX_TILE
    th = tq // FOX_HALVES
    n_q = q_ref.shape[1] // tq

    def scores(s_ref, i, t, row0=0):
        s_ref[row0:, :] = _dot(q_ref[0, i * tq + row0:(i + 1) * tq, :],
                               k_ref[0, :, t * FOX_KV:(t + 1) * FOX_KV])

    def consume(s_ref, slot, t, diag):
        v = v_ref[0, t * FOX_KV:(t + 1) * FOX_KV, :]
        for half in range(FOX_HALVES):
            rows = slice(half * th, (half + 1) * th)
            if diag is not None and diag * FOX_KV > (half + 1) * th - 1:
                continue
            s = s_ref[rows, :]
            if diag is not None and (diag + 1) * FOX_KV - 1 > half * th:
                row_i = half * th + lax.broadcasted_iota(jnp.int32, (th, FOX_KV), 0)
                col_i = diag * FOX_KV + lax.broadcasted_iota(jnp.int32, (th, FOX_KV), 1)
                s = jnp.where(col_i <= row_i, s, NEG_BIG)
            m_old = m_s[slot, rows, :]
            m_new = jnp.maximum(m_old, jnp.max(s, axis=-1, keepdims=True))
            p = jnp.concatenate([jnp.exp2(s[:, c0:c0 + LANES] - m_new)
                                 for c0 in range(0, FOX_KV, LANES)], axis=1)
            alpha = jnp.exp2(m_old - m_new)
            pv = _dot(p.astype(BF16), v)
            for c0 in range(0, FOX_SLAB, LANES):
                acc_s[slot, rows, c0:c0 + LANES] = (alpha * acc_s[slot, rows, c0:c0 + LANES]
                                                    + pv[:, c0:c0 + LANES])
            m_s[slot, rows, :] = m_new

    scores(s_a, 0, 0)
    for i in range(n_q):
        slot = i % 2
        m_s[slot] = jnp.full((tq, LANES), NEG_BIG, F32)
        acc_s[slot] = jnp.zeros((tq, FOX_SLAB), F32)
        for t in range(i):
            scores(s_b, i, 2 * t + 1)
            consume(s_a, slot, 2 * t, None)
            scores(s_a, i, 2 * t + 2)
            consume(s_b, slot, 2 * t + 1, None)
        scores(s_b, i, 2 * i + 1, row0=FOX_KV)
        consume(s_a, slot, 2 * i, 0)
        if i + 1 < n_q:
            scores(s_a, i + 1, 0)
        consume(s_b, slot, 2 * i + 1, 1)
        acc = acc_s[slot]
        o_ref[0, i * tq:(i + 1) * tq, :] = (acc[:, :FOX_HEAD_DIM]
                                            / acc[:, FOX_HEAD_DIM:FOX_HEAD_DIM + 1]).astype(BF16)


def _fox_call(fq, fk, fv):
    bsz, seq, _ = fq.shape
    tq = FOX_TILE
    return pl.pallas_call(
        _fox_kernel,
        grid=(bsz, FOX_HEADS),
        in_specs=[pl.BlockSpec((1, seq, FOX_SLAB), lambda b, h: (b, 0, h)),
                  pl.BlockSpec((1, FOX_SLAB, seq), lambda b, h: (b, h, 0)),
                  pl.BlockSpec((1, seq, FOX_SLAB), lambda b, h: (b, 0, h))],
        out_specs=pl.BlockSpec((1, seq, FOX_HEAD_DIM), lambda b, h: (b, 0, h)),
        out_shape=jax.ShapeDtypeStruct((bsz, seq, FOX_W), BF16),
        scratch_shapes=[pltpu.VMEM((2, tq, LANES), F32), pltpu.VMEM((2, tq, FOX_SLAB), F32),
                        pltpu.VMEM((tq, FOX_KV), F32), pltpu.VMEM((tq, FOX_KV), F32)],
        compiler_params=_params(2),
        name="fox_attention",
    )(fq, fk, fv)


MG_RG = 0
MG_SZ = MG_RG + RET_W
MG_GL = MG_SZ + SSM_INNER
MG_WIDTH = MG_GL + N_BRANCH * D_MODEL


def _merge_kernel(x_ref, mod_ref, gpre_ref, gpost_ref, wg_ref, ret_ref, yb_ref, ssd_ref, fox_ref,
                  ssmnorm_ref, wro_ref, wpo_ref, wso_ref, wfo_ref, wout_ref, o_ref):
    x = x_ref[...]
    hb = _modulated(x, mod_ref, gpre_ref, 1).astype(BF16)
    rg = _dot(hb, wg_ref[:, MG_RG:MG_RG + RET_W])
    y_a = (_silu(rg) * ret_ref[...].astype(F32)).astype(BF16)
    sz = _dot(hb, wg_ref[:, MG_SZ:MG_SZ + SSM_INNER])
    y_c = _rms(ssd_ref[...].astype(F32) * _silu(sz), ssmnorm_ref[...]).astype(BF16)
    merged = None
    for br, (y, w_ref) in enumerate(((y_a, wro_ref), (yb_ref[...], wpo_ref),
                                     (y_c, wso_ref), (fox_ref[...], wfo_ref))):
        lo = MG_GL + br * D_MODEL
        gate = _sigmoid(_dot(hb, wg_ref[:, lo:lo + D_MODEL]))
        part = gate * _dot(y, w_ref[...])
        merged = part if merged is None else merged + part
    y_out = _dot(merged.astype(BF16), wout_ref[...])
    o_ref[...] = x + mod_ref[0, 5:6, :] * _rms(y_out, gpost_ref[1:2, :])


def _merge_call(x2d, mod, gpre, gpost, wg_b, ret, yb, ssd, fox, ssmnorm, wro, wpo, wso, wfo, wout, seq):
    n, d = x2d.shape
    tm = TOK_TILE
    tiles_per_seq = seq // tm
    tok = lambda w: pl.BlockSpec((tm, w), lambda i: (i, 0))
    return pl.pallas_call(
        _merge_kernel,
        grid=(n // tm,),
        in_specs=[tok(d),
                  pl.BlockSpec((1, 3 * N_SUBLAYERS, d), lambda i: (i // tiles_per_seq, 0, 0)),
                  _resident(gpre.shape), _resident(gpost.shape),
                  pl.BlockSpec((d, REGION_W), lambda i: (0, 1), pipeline_mode=pl.Buffered(1)),
                  tok(RET_W), tok(POOL_W), tok(SSM_INNER), tok(FOX_W),
                  _resident(ssmnorm.shape), _resident(wro.shape), _resident(wpo.shape),
                  _resident(wso.shape), _resident(wfo.shape), _resident(wout.shape)],
        out_specs=tok(d),
        out_shape=jax.ShapeDtypeStruct((n, d), F32),
        compiler_params=_params(1),
        name="mixer_merge",
    )(x2d, mod, gpre, gpost, wg_b, ret.reshape(n, -1), yb.reshape(n, -1), ssd.reshape(n, -1),
      fox.reshape(n, -1), ssmnorm, wro, wpo, wso, wfo, wout)


def _rotary_tables(seq):
    half = RET_DK // 2
    inv = ROPE_BASE ** (-jnp.arange(half, dtype=F32) / half)
    ang = jnp.arange(seq, dtype=F32)[:, None] * inv[None, :]
    cos = jnp.cos(ang)
    sin = jnp.sin(ang)
    return jnp.concatenate([cos, cos], axis=-1), jnp.concatenate([-sin, sin], axis=-1)


def _place(vals, lane0):
    return jnp.zeros((1, LANES), F32).at[0, lane0:lane0 + vals.shape[0]].set(vals.astype(F32))


def kernel(x, c, w_ada, b_ada, norm_pre, norm_post, w_ffn_in, w_ffn_out, w_in, b_forget, pool_w, pool_scale, conv_w, conv_b, dt_bias, a_log, d_skip, ssm_norm, w_ret_out, w_pool_out, w_ssm_out, w_fox_out, w_out):
    bsz, seq, d = x.shape
    depth = w_ada.shape[0]
    n = bsz * seq
    mods = _ada_call(c, w_ada, b_ada)
    cos_t, sin_t = _rotary_tables(seq)
    tril_b = jnp.asarray(np.tril(np.ones((SSD_CHUNK, SSD_CHUNK), np.float32)), BF16)
    w_in_b = w_in.astype(BF16)
    for i in range(depth):
        mod = mods[i].reshape(bsz, 3 * N_SUBLAYERS, d)
        gpre, gpost = norm_pre[i], norm_post[i]
        x2d = _ffn_call(x.reshape(n, d), mod, gpre, gpost, w_ffn_in[i, 0].astype(BF16),
                        w_ffn_out[i, 0].astype(BF16), 0, seq)

        w_regrouped = _regroup_call(w_in_b, i)
        w_proj = w_gate = w_regrouped
        sbias = _place(dt_bias[i], SC_DT) + _place(dt_bias[i], SC_ACUM) + _place(b_forget[i], SC_CUMF)
        alog = _place(a_log[i], SC_ACUM)
        dskip_x = jnp.repeat(d_skip[i].astype(F32), SSM_HEAD_DIM)[None, :]

        rq, rk, rv, yb, xbc, fq, fk, fv, scal, scal_t = _proj_call(
            x2d.reshape(bsz, seq, d), mod, gpre, w_proj, cos_t, sin_t, pool_w[i].astype(BF16),
            pool_scale[i][None, :], conv_w[i], conv_b[i][None, :], sbias, alog, tril_b)
        ret = _ret_call(rq, rk, rv)
        ssd = _ssd_call(xbc, scal, scal_t, dskip_x)
        fox = _fox_call(fq, fk, fv)
        x2d = _merge_call(x2d, mod, gpre, gpost, w_gate, ret, yb, ssd, fox, ssm_norm[i][None, :],
                          w_ret_out[i].astype(BF16), w_pool_out[i].astype(BF16),
                          w_ssm_out[i].astype(BF16), w_fox_out[i].astype(BF16),
                          w_out[i].astype(BF16), seq)
        x2d = _ffn_call(x2d, mod, gpre, gpost, w_ffn_in[i, 1].astype(BF16),
                        w_ffn_out[i, 1].astype(BF16), 2, seq)
        x = x2d.reshape(bsz, seq, d)
    return x
```

```python
import functools

import jax
import jax.numpy as jnp
import numpy as np
from jax import lax
from jax.experimental import pallas as pl
from jax.experimental.pallas import tpu as pltpu

F32 = jnp.float32
BF16 = jnp.bfloat16

D_MODEL = 1024
RET_HEADS = 4
RET_DK = 128
RET_DV = 128
RET_W = RET_HEADS * RET_DK
ROPE_BASE = 10000.0
POOL_WINDOWS = (2, 4, 8, 16)
POOL_GROUPS = 4
POOL_GROUP_DIM = 128
POOL_W = POOL_GROUPS * POOL_GROUP_DIM
SSM_HEADS = 16
SSM_HEAD_DIM = 64
SSM_INNER = SSM_HEADS * SSM_HEAD_DIM
SSM_GROUPS = 2
SSM_STATE = 128
SSM_CONV = 4
SSM_XBC = SSM_INNER + 2 * SSM_GROUPS * SSM_STATE
FOX_HEADS = 4
FOX_HEAD_DIM = 128
FOX_W = FOX_HEADS * FOX_HEAD_DIM
N_BRANCH = 4
D_FF = 2816
N_SUBLAYERS = 3
RMS_EPS = 1e-6
GN_EPS = 1e-5
IN_SIZES = (RET_W, RET_W, RET_W, RET_W, POOL_W, SSM_INNER, SSM_XBC, SSM_HEADS,
            FOX_W, FOX_W, FOX_W, FOX_HEADS, N_BRANCH * D_MODEL)

LANES = 128
VMEM_LIMIT = 56 * 1024 * 1024
ADA_TN = 1152
TOK_TILE = 512
MXU_DIM = 256
FFN_CHUNK = 6 * MXU_DIM
SEQ_TILE = 512
RET_CHUNK = 256
SSD_CHUNK = 128
FOX_TILE = 1024
LOG2_E = float(np.log2(np.e))
NEG_BIG = -1e30

SC_DT = 0
SC_ACUM = 16
SC_CUMF = 32
FOX_SLAB = 2 * FOX_HEAD_DIM
FOX_PIECES = 3


def _sigmoid(v):
    return 1.0 / (1.0 + jnp.exp(-v))


def _silu(v):
    return v * (1.0 / (1.0 + jnp.exp2(v * (-LOG2_E))))


def _rms(v, gain):
    return v * lax.rsqrt(jnp.mean(v * v, axis=-1, keepdims=True) + RMS_EPS) * gain


def _modulated(x, mod_ref, gpre_ref, sub):
    shift = mod_ref[0, 3 * sub:3 * sub + 1, :]
    scale = mod_ref[0, 3 * sub + 1:3 * sub + 2, :]
    return _rms(x, gpre_ref[sub:sub + 1, :]) * (1.0 + scale) + shift


def _dot(a, b):
    return jnp.dot(a, b, preferred_element_type=F32)


def _dot_nt(a, b):
    return lax.dot_general(a, b, (((1,), (1,)), ((), ())), preferred_element_type=F32)


def _dot_tn(a, b):
    return lax.dot_general(a, b, (((0,), (0,)), ((), ())), preferred_element_type=F32)


def _resident(shape):
    nd = len(shape)
    return pl.BlockSpec(shape, lambda *_: (0,) * nd, pipeline_mode=pl.Buffered(1))


def _params(n_axes):
    return pltpu.CompilerParams(dimension_semantics=("arbitrary",) * n_axes,
                                vmem_limit_bytes=VMEM_LIMIT)


def _ada_kernel(c_ref, w_ref, b_ref, o_ref):
    sc = _silu(c_ref[...]).astype(BF16)
    o_ref[0] = _dot(sc, w_ref[0].astype(BF16)) + b_ref[0]


def _ada_call(c, w_ada, b_ada):
    depth, d, width = w_ada.shape
    bsz = c.shape[0]
    return pl.pallas_call(
        _ada_kernel,
        grid=(depth, width // ADA_TN),
        in_specs=[pl.BlockSpec((bsz, d), lambda l, n: (0, 0)),
                  pl.BlockSpec((1, d, ADA_TN), lambda l, n: (l, 0, n)),
                  pl.BlockSpec((1, 1, ADA_TN), lambda l, n: (l, 0, n))],
        out_specs=pl.BlockSpec((1, bsz, ADA_TN), lambda l, n: (l, 0, n)),
        out_shape=jax.ShapeDtypeStruct((depth, bsz, width), F32),
        compiler_params=_params(2),
        name="adaln",
    )(c, w_ada, b_ada.reshape(depth, 1, width))


def _ffn_kernel(x_ref, mod_ref, gpre_ref, gpost_ref, win_ref, wout_ref, o_ref, *, sub):
    x = x_ref[...]
    hb = _modulated(x, mod_ref, gpre_ref, sub).astype(BF16)
    y = None
    for lo in range(0, D_FF, FFN_CHUNK):
        hi = min(lo + FFN_CHUNK, D_FF)
        g = _dot(hb, win_ref[0, 0, :, lo:hi])
        u = _dot(hb, win_ref[0, 0, :, D_FF + lo:D_FF + hi])
        part = _dot((_silu(g) * u).astype(BF16), wout_ref[0, 0, lo:hi, :])
        y = part if y is None else y + part
    gate = mod_ref[0, 3 * sub + 2:3 * sub + 3, :]
    o_ref[...] = x + (0.5 * gate) * _rms(y, gpost_ref[sub:sub + 1, :])


def _ffn_call(x2d, mod, gpre, gpost, w_in_b, w_out_b, layer, which, seq):
    n, d = x2d.shape
    tiles_per_seq = seq // TOK_TILE
    sub = 2 * which
    pick = lambda w: pl.BlockSpec((1, 1) + w.shape[2:], lambda i: (layer, which, 0, 0),
                                  pipeline_mode=pl.Buffered(1))
    return pl.pallas_call(
        functools.partial(_ffn_kernel, sub=sub),
        grid=(n // TOK_TILE,),
        in_specs=[pl.BlockSpec((TOK_TILE, d), lambda i: (i, 0)),
                  pl.BlockSpec((1, 3 * N_SUBLAYERS, d), lambda i: (i // tiles_per_seq, 0, 0)),
                  _resident(gpre.shape), _resident(gpost.shape),
                  pick(w_in_b), pick(w_out_b)],
        out_specs=pl.BlockSpec((TOK_TILE, d), lambda i: (i, 0)),
        out_shape=jax.ShapeDtypeStruct((n, d), F32),
        compiler_params=_params(1),
        name="ffn",
    )(x2d, mod, gpre, gpost, w_in_b, w_out_b)


IN_OFFSETS = tuple(int(v) for v in np.cumsum((0,) + IN_SIZES[:-1]))
(IN_RQ, IN_RK, IN_RV, IN_RG, IN_PU, IN_SZ, IN_XBC, IN_DT, IN_FQ, IN_FK, IN_FV, IN_FF, IN_GL) = IN_OFFSETS
REGION_W = (RET_W + SSM_INNER + N_BRANCH * D_MODEL)


def _regroup_plan():
    shifts = sorted({off % LANES for off in IN_OFFSETS})
    kind_of = {s: k for k, s in enumerate(shifts)}
    scalar_kind, zero_kind = len(shifts), len(shifts) + 1
    place = np.zeros((len(shifts) + 2, 2 * LANES, LANES), np.float32)
    for s, k in kind_of.items():
        place[k, s + np.arange(LANES), np.arange(LANES)] = 1.0
    dt_lane, ff_lane = IN_DT % LANES, IN_FF % LANES
    assert dt_lane + SSM_HEADS <= LANES and ff_lane + FOX_HEADS <= LANES
    for hd in range(SSM_HEADS):
        place[scalar_kind, dt_lane + hd, SC_DT + hd] = 1.0
        place[scalar_kind, dt_lane + hd, SC_ACUM + hd] = 1.0
    for hd in range(FOX_HEADS):
        place[scalar_kind, LANES + ff_lane + hd, SC_CUMF + hd] = 1.0
    lo, hi, kind = [], [], []

    def span(off, width):
        for c in range(off, off + width, LANES):
            lo.append(c // LANES)
            hi.append(c // LANES + (1 if c % LANES else 0))
            kind.append(kind_of[c % LANES])

    proj = ((IN_RQ, RET_W), (IN_RK, RET_W), (IN_RV, RET_W), (IN_PU, POOL_W), (IN_XBC, SSM_XBC),
            (IN_FQ, FOX_W), (IN_FK, FOX_W), (IN_FV, FOX_W))
    for off, width in proj:
        span(off, width)
    lo.append(IN_DT // LANES); hi.append(IN_FF // LANES); kind.append(scalar_kind)
    while len(lo) < REGION_W // LANES:
        lo.append(0); hi.append(0); kind.append(zero_kind)
    for off, width in ((IN_RG, RET_W), (IN_SZ, SSM_INNER), (IN_GL, N_BRANCH * D_MODEL)):
        span(off, width)
    assert len(lo) == 2 * REGION_W // LANES
    as_i32 = lambda v: jnp.asarray(np.asarray(v, np.int32))
    need = place[:, LANES:, :].any(axis=2).astype(np.float32)[:, None, :]
    return as_i32(lo), as_i32(hi), as_i32(kind), jnp.asarray(place, BF16), jnp.asarray(need, F32)


def _regroup_kernel(lo_ref, hi_ref, kind_ref, wlo_ref, whi_ref, place_ref, need_ref, o_ref):
    del lo_ref, hi_ref
    kind = kind_ref[pl.program_id(0)]
    second = whi_ref[0]
    second = jnp.where(need_ref[kind] > 0.0, second, jnp.zeros_like(second))
    pair = jnp.concatenate([wlo_ref[0], second], axis=1).astype(BF16)
    o_ref[...] = _dot(pair, place_ref[kind]).astype(BF16)


def _regroup_call(w_in, layer):
    _, d, _ = w_in.shape
    lo, hi, kind, place, need = _regroup_plan()
    n_blocks = lo.shape[0]
    grid_spec = pltpu.PrefetchScalarGridSpec(
        num_scalar_prefetch=3,
        grid=(n_blocks,),
        in_specs=[pl.BlockSpec((1, d, LANES), lambda o, lo, hi, kind: (layer, 0, lo[o])),
                  pl.BlockSpec((1, d, LANES), lambda o, lo, hi, kind: (layer, 0, hi[o])),
                  pl.BlockSpec(place.shape, lambda o, lo, hi, kind: (0, 0, 0)),
                  pl.BlockSpec(need.shape, lambda o, lo, hi, kind: (0, 0, 0))],
        out_specs=pl.BlockSpec((d, LANES), lambda o, lo, hi, kind: (0, o)))
    return pl.pallas_call(
        _regroup_kernel,
        grid_spec=grid_spec,
        out_shape=jax.ShapeDtypeStruct((d, n_blocks * LANES), BF16),
        compiler_params=_params(1),
        name="regroup_w_in",
    )(lo, hi, kind, w_in, w_in, place, need)


PJ_RQ = 0
PJ_RK = PJ_RQ + RET_W
PJ_RV = PJ_RK + RET_W
PJ_PU = PJ_RV + RET_W
PJ_XBC = PJ_PU + POOL_W
PJ_FQ = PJ_XBC + SSM_XBC
PJ_FK = PJ_FQ + FOX_W
PJ_FV = PJ_FK + FOX_W
PJ_SC = PJ_FV + FOX_W
PJ_WIDTH = PJ_SC + LANES
POOL_HALO = 16
CONV_COLS = SSM_XBC // 3
CONV_HALO = 8


def _cumsum_rows(tril_b, s):
    p1 = s.astype(BF16)
    r1 = s - p1.astype(F32)
    p2 = r1.astype(BF16)
    p3 = (r1 - p2.astype(F32)).astype(BF16)
    return _dot(tril_b, p1) + _dot(tril_b, p2) + _dot(tril_b, p3)


def _proj_kernel(x_ref, mod_ref, gpre_ref, w_ref, cos_ref, sin_ref, poolw_ref, pscale_ref,
                 convw_ref, convb_ref, sbias_ref, alog_ref, tril_ref, selq_ref, selk_ref,
                 rq_ref, rk_ref, rv_ref, yb_ref, xbc_ref, fq_ref, fk_ref, fv_ref, scal_ref, scalT_ref,
                 pbuf, cbuf, carry):
    j = pl.program_id(1)
    tm = x_ref.shape[1]

    @pl.when(j == 0)
    def _():
        pbuf[0:POOL_HALO, :] = jnp.zeros((POOL_HALO, POOL_W), F32)
        cbuf[0:CONV_HALO, :] = jnp.zeros((CONV_HALO, SSM_XBC), F32)
        carry[...] = jnp.zeros_like(carry)

    @pl.when(j > 0)
    def _():
        pbuf[0:POOL_HALO, :] = pbuf[tm:tm + POOL_HALO, :]
        cbuf[0:CONV_HALO, :] = cbuf[tm:tm + CONV_HALO, :]

    hb = _modulated(x_ref[0], mod_ref, gpre_ref, 1).astype(BF16)
    cos = cos_ref[...]
    sin = sin_ref[...]
    pos = j * tm + lax.broadcasted_iota(jnp.int32, (tm, POOL_GROUP_DIM), 0)

    def rotary(base, ref, scl):
        cos_s, sin_s = (cos, sin) if scl is None else (cos * scl, sin * scl)
        t = _dot(hb, w_ref[:, base:base + RET_W])
        for hd in range(RET_HEADS):
            th = t[:, hd * RET_DK:(hd + 1) * RET_DK]
            r = th * cos_s + pltpu.roll(th, RET_DK // 2, 1) * sin_s
            ref[0, :, hd * RET_DK:(hd + 1) * RET_DK] = r.astype(BF16)

    def pool_mm():
        pu = _dot(hb, w_ref[:, PJ_PU:PJ_PU + POOL_W])
        pbuf[POOL_HALO:POOL_HALO + tm, :] = pu
        return pu

    def pool_ep(pu, g):
        win = POOL_WINDOWS[g]
        ls = slice(g * POOL_GROUP_DIM, (g + 1) * POOL_GROUP_DIM)
        cur = pu[:, ls]
        acc = pbuf[:, ls]
        span = 1
        while span < win:
            acc = acc + pltpu.roll(acc, span, 0)
            span *= 2
        count = jnp.minimum(pos + 1, win).astype(F32)
        pooled = acc[POOL_HALO:, :] / count - cur
        mixed = _dot(pooled.astype(BF16), poolw_ref[g]) * pscale_ref[:, ls]
        yb_ref[0, :, ls] = mixed.astype(BF16)

    def conv_mm(c0):
        cs = slice(c0, c0 + CONV_COLS)
        xr = _dot(hb, w_ref[:, PJ_XBC + c0:PJ_XBC + c0 + CONV_COLS])
        cbuf[CONV_HALO:CONV_HALO + tm, cs] = xr
        return xr

    def conv_ep(xr, c0):
        cs = slice(c0, c0 + CONV_COLS)
        conv = xr * convw_ref[SSM_CONV - 1:SSM_CONV, cs] + convb_ref[:, cs]
        for k in range(SSM_CONV - 1):
            off = CONV_HALO - (SSM_CONV - 1) + k
            conv = conv + cbuf[off:off + tm, cs] * convw_ref[k:k + 1, cs]
        xbc_ref[0, :, cs] = _silu(conv).astype(BF16)

    def fox_mm(base, ref, scl, transposed):
        f = _dot(hb, w_ref[:, base:base + FOX_W])
        if scl is not None:
            f = f * scl
        for hd in range(FOX_HEADS):
            src_l = slice(hd * FOX_HEAD_DIM, (hd + 1) * FOX_HEAD_DIM)
            dst_l = slice(hd * FOX_SLAB, hd * FOX_SLAB + FOX_HEAD_DIM)
            if transposed:
                ref[0, dst_l, :] = f[:, src_l].T.astype(BF16)
            else:
                ref[0, :, dst_l] = f[:, src_l].astype(BF16)

    pu = pool_mm()
    xr0 = conv_mm(0)
    xr1 = conv_mm(CONV_COLS)
    xr2 = conv_mm(2 * CONV_COLS)
    fox_mm(PJ_FQ, fq_ref, FOX_HEAD_DIM ** -0.5 * LOG2_E, False)
    for g in range(POOL_GROUPS):
        pool_ep(pu, g)
    fox_mm(PJ_FK, fk_ref, None, True)
    conv_ep(xr0, 0)
    fox_mm(PJ_FV, fv_ref, None, False)
    conv_ep(xr1, CONV_COLS)
    rv_ref[0] = _dot(hb, w_ref[:, PJ_RV:PJ_RV + RET_W]).astype(BF16)
    conv_ep(xr2, 2 * CONV_COLS)
    rotary(PJ_RQ, rq_ref, None)
    rotary(PJ_RK, rk_ref, RET_DK ** -0.5)


    z = _dot(hb, w_ref[:, PJ_SC:PJ_SC + LANES]) + sbias_ref[...]
    tail = jnp.log1p(jnp.exp(-jnp.abs(z)))
    softplus = jnp.maximum(z, 0.0) + tail
    log_sig = jnp.minimum(z, 0.0) - tail
    lane = lax.broadcasted_iota(jnp.int32, (tm, LANES), 1)
    a_row = -jnp.exp(alog_ref[...])
    in_acum = (lane >= SC_ACUM) & (lane < SC_CUMF)
    in_cumf = (lane >= SC_CUMF) & (lane < SC_CUMF + FOX_HEADS)
    src = jnp.where(in_acum, softplus * a_row, jnp.where(in_cumf, log_sig, 0.0))
    tril_b = tril_ref[...]
    run = carry[...]
    lane_c = lax.broadcasted_iota(jnp.int32, (SSD_CHUNK, LANES), 1)
    for c0 in range(0, tm, SSD_CHUNK):
        rows = slice(c0, c0 + SSD_CHUNK)
        local = _cumsum_rows(tril_b, src[rows, :])
        total = local + run
        scal_ref[0, rows, :] = jnp.where(lane_c < SC_ACUM, softplus[rows, :],
                                         jnp.where(lane_c < SC_CUMF, local, total))
        run = total[SSD_CHUNK - 1:SSD_CHUNK, :]
    carry[...] = run
    sc_all = scal_ref[0]
    scalT_ref[0] = sc_all.T

    bias = sc_all * LOG2_E
    c1 = bias.astype(BF16)
    r1 = bias - c1.astype(F32)
    c2 = r1.astype(BF16)
    c3 = (r1 - c2.astype(F32)).astype(BF16)
    lane_h = lax.broadcasted_iota(jnp.int32, (tm, FOX_HEAD_DIM), 1)
    ones_q = jnp.where((lane_h >= FOX_PIECES) & (lane_h < 2 * FOX_PIECES), 1.0, 0.0)
    ones_k = jnp.where(lane_h < FOX_PIECES, 1.0, 0.0)
    ones_v = jnp.where(lane_h == 0, 1.0, 0.0).astype(BF16)
    bias_q = _dot(c1, selq_ref[0]) + _dot(c2, selq_ref[1]) + _dot(c3, selq_ref[2])
    bias_k = _dot(c1, selk_ref[0]) + _dot(c2, selk_ref[1]) + _dot(c3, selk_ref[2])
    for hd in range(FOX_HEADS):
        src_l = slice(hd * FOX_HEAD_DIM, (hd + 1) * FOX_HEAD_DIM)
        dst_l = slice(hd * FOX_SLAB + FOX_HEAD_DIM, (hd + 1) * FOX_SLAB)
        fq_ref[0, :, dst_l] = (bias_q[:, src_l] + ones_q).astype(BF16)
        fk_ref[0, dst_l, :] = (ones_k - bias_k[:, src_l]).T.astype(BF16)
        fv_ref[0, :, dst_l] = ones_v


def _fox_select_tables():
    selq = np.zeros((FOX_PIECES, LANES, FOX_W), np.float32)
    selk = np.zeros((FOX_PIECES, LANES, FOX_W), np.float32)
    for piece in range(FOX_PIECES):
        for hd in range(FOX_HEADS):
            selq[piece, SC_CUMF + hd, hd * FOX_HEAD_DIM + piece] = 1.0
            selk[piece, SC_CUMF + hd, hd * FOX_HEAD_DIM + FOX_PIECES + piece] = 1.0
    return jnp.asarray(selq, BF16), jnp.asarray(selk, BF16)


def _proj_call(x, mod, gpre, w_b, cos_t, sin_t, poolw_b, pscale, convw, convb, sbias, alog, tril_b):
    bsz, seq, d = x.shape
    tm = TOK_TILE
    selq, selk = _fox_select_tables()
    tok = lambda w: pl.BlockSpec((1, tm, w), lambda b, j: (b, j, 0))
    bf = lambda w: jax.ShapeDtypeStruct((bsz, seq, w), BF16)
    fox_w = FOX_HEADS * FOX_SLAB
    return pl.pallas_call(
        _proj_kernel,
        grid=(bsz, seq // tm),
        in_specs=[tok(d),
                  pl.BlockSpec((1, 3 * N_SUBLAYERS, d), lambda b, j: (b, 0, 0)),
                  _resident(gpre.shape),
                  pl.BlockSpec((d, REGION_W), lambda b, j: (0, 0), pipeline_mode=pl.Buffered(1)),
                  pl.BlockSpec((tm, RET_DK), lambda b, j: (j, 0)),
                  pl.BlockSpec((tm, RET_DK), lambda b, j: (j, 0)),
                  _resident(poolw_b.shape), _resident(pscale.shape),
                  _resident(convw.shape), _resident(convb.shape),
                  _resident(sbias.shape), _resident(alog.shape), _resident(tril_b.shape),
                  _resident(selq.shape), _resident(selk.shape)],
        out_specs=[tok(RET_W), tok(RET_W), tok(RET_W), tok(POOL_W), tok(SSM_XBC),
                   tok(fox_w), pl.BlockSpec((1, fox_w, tm), lambda b, j: (b, 0, j)), tok(fox_w), tok(LANES),
                   pl.BlockSpec((1, LANES, tm), lambda b, j: (b, 0, j))],
        out_shape=[bf(RET_W), bf(RET_W), bf(RET_W), bf(POOL_W), bf(SSM_XBC),
                   bf(fox_w), jax.ShapeDtypeStruct((bsz, fox_w, seq), BF16), bf(fox_w),
                   jax.ShapeDtypeStruct((bsz, seq, LANES), F32),
                   jax.ShapeDtypeStruct((bsz, LANES, seq), F32)],
        scratch_shapes=[pltpu.VMEM((POOL_HALO + tm, POOL_W), F32),
                        pltpu.VMEM((CONV_HALO + tm, SSM_XBC), F32),
                        pltpu.VMEM((1, LANES), F32)],
        compiler_params=_params(2),
        name="mixer_proj",
    )(x, mod, gpre, w_b, cos_t, sin_t, poolw_b, pscale, convw, convb, sbias, alog, tril_b, selq, selk)


def _ret_kernel(q_ref, k_ref, v_ref, idec_ref, qdec_ref, kdec_ref, o_ref, state, *, chunk_decay):
    @pl.when(pl.program_id(1) == 0)
    def _():
        state[...] = jnp.zeros_like(state)

    tile = q_ref.shape[1]
    for hd in range(RET_HEADS):
        ls = slice(hd * RET_DK, (hd + 1) * RET_DK)
        for c0 in range(0, tile, RET_CHUNK):
            rows = slice(c0, c0 + RET_CHUNK)
            q = q_ref[0, rows, ls]
            k = k_ref[0, rows, ls]
            v = v_ref[0, rows, ls]
            st = state[hd]
            scores = _dot_nt(q, k) * idec_ref[hd]
            o = _dot(scores.astype(BF16), v) + _dot(q, st.astype(BF16)) * qdec_ref[hd]
            kd = (k.astype(F32) * kdec_ref[hd]).astype(BF16)
            state[hd] = chunk_decay[hd] * st + _dot_tn(kd, v)
            dev = o - jnp.mean(o, axis=-1, keepdims=True)
            var = jnp.mean(dev * dev, axis=-1, keepdims=True)
            o_ref[0, rows, ls] = (dev * lax.rsqrt(var + GN_EPS)).astype(BF16)


def _ret_tables():
    log_gamma = np.log1p(-np.exp2(-5.0 - np.arange(RET_HEADS, dtype=np.float64)))
    idx = np.arange(RET_CHUNK, dtype=np.float64)
    rel = idx[:, None] - idx[None, :]
    intra = np.where(rel >= 0, np.exp(log_gamma[:, None, None] * np.maximum(rel, 0.0)), 0.0)
    q_decay = np.exp(log_gamma[:, None] * (idx + 1.0))
    k_decay = np.exp(log_gamma[:, None] * (RET_CHUNK - 1.0 - idx))
    widen = lambda t: np.broadcast_to(t[:, :, None], (RET_HEADS, RET_CHUNK, RET_DK))
    chunk_decay = tuple(float(v) for v in np.exp(log_gamma * RET_CHUNK))
    return (jnp.asarray(intra, F32), jnp.asarray(widen(q_decay), F32),
            jnp.asarray(widen(k_decay), F32), chunk_decay)


def _ret_call(rq, rk, rv):
    bsz, seq, w = rq.shape
    idec, qdec, kdec, chunk_decay = _ret_tables()
    tok = pl.BlockSpec((1, SEQ_TILE, w), lambda b, j: (b, j, 0))
    return pl.pallas_call(
        functools.partial(_ret_kernel, chunk_decay=chunk_decay),
        grid=(bsz, seq // SEQ_TILE),
        in_specs=[tok, tok, tok, _resident(idec.shape), _resident(qdec.shape), _resident(kdec.shape)],
        out_specs=tok,
        out_shape=jax.ShapeDtypeStruct((bsz, seq, w), BF16),
        scratch_shapes=[pltpu.VMEM((RET_HEADS, RET_DK, RET_DV), F32)],
        compiler_params=_params(2),
        name="retention",
    )(rq, rk, rv, idec, qdec, kdec)


HEADS_PER_GROUP = SSM_HEADS // SSM_GROUPS
GROUP_W = HEADS_PER_GROUP * SSM_HEAD_DIM


def _ssd_kernel(xbc_ref, scal_ref, scalT_ref, ea_sel_ref, dt_sel_ref, dskip_ref, o_ref, state):
    @pl.when(pl.program_id(1) == 0)
    def _():
        state[...] = jnp.zeros_like(state)

    tile = xbc_ref.shape[1]
    cl = SSD_CHUNK
    lane = lax.broadcasted_iota(jnp.int32, (cl, LANES), 1)
    row_i = lax.broadcasted_iota(jnp.int32, (cl, cl), 0)
    col_i = lax.broadcasted_iota(jnp.int32, (cl, cl), 1)
    causal = col_i <= row_i
    low_half = lane < SSM_HEAD_DIM
    ea_sel = ea_sel_ref[...]
    dt_sel = dt_sel_ref[...]
    for c0 in range(0, tile, cl):
        rows = slice(c0, c0 + cl)
        sc = scal_ref[0, rows, :]
        sc2 = sc * LOG2_E
        in_acum = (lane >= SC_ACUM) & (lane < SC_CUMF)
        acum = jnp.where(in_acum, sc, 0.0)
        ea = jnp.exp(acum)
        dec_end = jnp.exp(acum[cl - 1:cl, :] - acum)
        dt = jnp.where(lane < SC_ACUM, sc, 0.0)
        ea_hi = ea.astype(BF16)
        ea_lo = (ea - ea_hi.astype(F32)).astype(BF16)
        ea_x = _dot(ea_hi, ea_sel) + _dot(ea_lo, ea_sel)
        w_x = _dot(dec_end.astype(BF16), ea_sel) * _dot(dt.astype(BF16), dt_sel)
        xs_b = xbc_ref[0, rows, 0:SSM_INNER]
        xs = xs_b.astype(F32)
        xw_b = (xs * w_x).astype(BF16)
        for g in range(SSM_GROUPS):
            gs = slice(g * GROUP_W, (g + 1) * GROUP_W)
            bm = xbc_ref[0, rows, SSM_INNER + g * SSM_STATE:SSM_INNER + (g + 1) * SSM_STATE]
            cm_lo = SSM_INNER + SSM_GROUPS * SSM_STATE + g * SSM_STATE
            cm = xbc_ref[0, rows, cm_lo:cm_lo + SSM_STATE]
            cb = _dot_nt(cm, bm)
            st = state[g]
            y_inter = _dot(cm, st.astype(BF16)) * ea_x[:, gs]
            state[g] = st * ea_x[cl - 1:cl, gs] + _dot_tn(bm, xw_b[:, gs])
            for pair in range(HEADS_PER_GROUP // 2):
                mats = []
                for hh in (2 * pair, 2 * pair + 1):
                    hd = g * HEADS_PER_GROUP + hh
                    a_col = jnp.broadcast_to(sc2[:, SC_ACUM + hd:SC_ACUM + hd + 1], (cl, cl))
                    a_row = scalT_ref[0, SC_ACUM + hd:SC_ACUM + hd + 1, rows] * LOG2_E
                    dt_row = scalT_ref[0, SC_DT + hd:SC_DT + hd + 1, rows]
                    lmat = jnp.exp2(jnp.where(causal, a_col - (a_row - jnp.log2(dt_row)), NEG_BIG))
                    mats.append((lmat * cb).astype(BF16))
                lo = g * GROUP_W + pair * LANES
                x_pair = xs_b[:, lo:lo + LANES]
                zero = jnp.zeros_like(x_pair)
                rhs = jnp.concatenate([jnp.where(low_half, x_pair, zero),
                                       jnp.where(low_half, zero, x_pair)], axis=0)
                y_pair = _dot(jnp.concatenate(mats, axis=1), rhs)
                y_pair = y_pair + y_inter[:, pair * LANES:(pair + 1) * LANES]
                y_pair = y_pair + dskip_ref[:, lo:lo + LANES] * xs[:, lo:lo + LANES]
                o_ref[0, rows, lo:lo + LANES] = y_pair.astype(BF16)


def _ssd_tables():
    ea_sel = np.zeros((LANES, SSM_INNER), np.float32)
    dt_sel = np.zeros((LANES, SSM_INNER), np.float32)
    for hd in range(SSM_HEADS):
        ea_sel[SC_ACUM + hd, hd * SSM_HEAD_DIM:(hd + 1) * SSM_HEAD_DIM] = 1.0
        dt_sel[SC_DT + hd, hd * SSM_HEAD_DIM:(hd + 1) * SSM_HEAD_DIM] = 1.0
    return jnp.asarray(ea_sel, BF16), jnp.asarray(dt_sel, BF16)


def _ssd_call(xbc, scal, scal_t, dskip_x):
    bsz, seq, _ = xbc.shape
    ea_sel, dt_sel = _ssd_tables()
    return pl.pallas_call(
        _ssd_kernel,
        grid=(bsz, seq // SEQ_TILE),
        in_specs=[pl.BlockSpec((1, SEQ_TILE, SSM_XBC), lambda b, j: (b, j, 0)),
                  pl.BlockSpec((1, SEQ_TILE, LANES), lambda b, j: (b, j, 0)),
                  pl.BlockSpec((1, LANES, SEQ_TILE), lambda b, j: (b, 0, j)),
                  _resident(ea_sel.shape), _resident(dt_sel.shape), _resident(dskip_x.shape)],
        out_specs=pl.BlockSpec((1, SEQ_TILE, SSM_INNER), lambda b, j: (b, j, 0)),
        out_shape=jax.ShapeDtypeStruct((bsz, seq, SSM_INNER), BF16),
        scratch_shapes=[pltpu.VMEM((SSM_GROUPS, SSM_STATE, GROUP_W), F32)],
        compiler_params=_params(2),
        name="ssd",
    )(xbc, scal, scal_t, ea_sel, dt_sel, dskip_x)


FOX_HALVES = 2
FOX_KV = FOX_TILE // 2


def _fox_kernel(q_ref, k_ref, v_ref, o_ref, m_s, acc_s, s_a, s_b):
    tq = FOX_TILE
    th = tq // FOX_HALVES
    n_q = q_ref.shape[1] // tq

    def scores(s_ref, i, t, row0=0):
        s_ref[row0:, :] = _dot(q_ref[0, i * tq + row0:(i + 1) * tq, :],
                               k_ref[0, :, t * FOX_KV:(t + 1) * FOX_KV])

    def consume(s_ref, slot, t, diag):
        v = v_ref[0, t * FOX_KV:(t + 1) * FOX_KV, :]
        for half in range(FOX_HALVES):
            rows = slice(half * th, (half + 1) * th)
            if diag is not None and diag * FOX_KV > (half + 1) * th - 1:
                continue
            s = s_ref[rows, :]
            if diag is not None and (diag + 1) * FOX_KV - 1 > half * th:
                row_i = half * th + lax.broadcasted_iota(jnp.int32, (th, FOX_KV), 0)
                col_i = diag * FOX_KV + lax.broadcasted_iota(jnp.int32, (th, FOX_KV), 1)
                s = jnp.where(col_i <= row_i, s, NEG_BIG)
            m_old = m_s[slot, rows, :]
            m_new = jnp.maximum(m_old, jnp.max(s, axis=-1, keepdims=True))
            p = jnp.concatenate([jnp.exp2(s[:, c0:c0 + LANES] - m_new)
                                 for c0 in range(0, FOX_KV, LANES)], axis=1)
            alpha = jnp.exp2(m_old - m_new)
            pv = _dot(p.astype(BF16), v)
            for c0 in range(0, FOX_SLAB, LANES):
                acc_s[slot, rows, c0:c0 + LANES] = (alpha * acc_s[slot, rows, c0:c0 + LANES]
                                                    + pv[:, c0:c0 + LANES])
            m_s[slot, rows, :] = m_new

    scores(s_a, 0, 0)
    for i in range(n_q):
        slot = i % 2
        m_s[slot] = jnp.full((tq, LANES), NEG_BIG, F32)
        acc_s[slot] = jnp.zeros((tq, FOX_SLAB), F32)
        for t in range(i):
            scores(s_b, i, 2 * t + 1)
            consume(s_a, slot, 2 * t, None)
            scores(s_a, i, 2 * t + 2)
            consume(s_b, slot, 2 * t + 1, None)
        scores(s_b, i, 2 * i + 1, row0=FOX_KV)
        consume(s_a, slot, 2 * i, 0)
        if i + 1 < n_q:
            scores(s_a, i + 1, 0)
        consume(s_b, slot, 2 * i + 1, 1)
        acc = acc_s[slot]
        o_ref[0, i * tq:(i + 1) * tq, :] = (acc[:, :FOX_HEAD_DIM]
                                            / acc[:, FOX_HEAD_DIM:FOX_HEAD_DIM + 1]).astype(BF16)


def _fox_call(fq, fk, fv):
    bsz, seq, _ = fq.shape
    tq = FOX_TILE
    return pl.pallas_call(
        _fox_kernel,
        grid=(bsz, FOX_HEADS),
        in_specs=[pl.BlockSpec((1, seq, FOX_SLAB), lambda b, h: (b, 0, h)),
                  pl.BlockSpec((1, FOX_SLAB, seq), lambda b, h: (b, h, 0)),
                  pl.BlockSpec((1, seq, FOX_SLAB), lambda b, h: (b, 0, h))],
        out_specs=pl.BlockSpec((1, seq, FOX_HEAD_DIM), lambda b, h: (b, 0, h)),
        out_shape=jax.ShapeDtypeStruct((bsz, seq, FOX_W), BF16),
        scratch_shapes=[pltpu.VMEM((2, tq, LANES), F32), pltpu.VMEM((2, tq, FOX_SLAB), F32),
                        pltpu.VMEM((tq, FOX_KV), F32), pltpu.VMEM((tq, FOX_KV), F32)],
        compiler_params=_params(2),
        name="fox_attention",
    )(fq, fk, fv)


MG_RG = 0
MG_SZ = MG_RG + RET_W
MG_GL = MG_SZ + SSM_INNER
MG_WIDTH = MG_GL + N_BRANCH * D_MODEL


def _merge_kernel(x_ref, mod_ref, gpre_ref, gpost_ref, wg_ref, ret_ref, yb_ref, ssd_ref, fox_ref,
                  ssmnorm_ref, wro_ref, wpo_ref, wso_ref, wfo_ref, wout_ref, o_ref):
    x = x_ref[...]
    hb = _modulated(x, mod_ref, gpre_ref, 1).astype(BF16)
    rg = _dot(hb, wg_ref[:, MG_RG:MG_RG + RET_W])
    y_a = (_silu(rg) * ret_ref[...].astype(F32)).astype(BF16)
    sz = _dot(hb, wg_ref[:, MG_SZ:MG_SZ + SSM_INNER])
    y_c = _rms(ssd_ref[...].astype(F32) * _silu(sz), ssmnorm_ref[...]).astype(BF16)
    merged = None
    for br, (y, w_ref) in enumerate(((y_a, wro_ref), (yb_ref[...], wpo_ref),
                                     (y_c, wso_ref), (fox_ref[...], wfo_ref))):
        lo = MG_GL + br * D_MODEL
        gate = _sigmoid(_dot(hb, wg_ref[:, lo:lo + D_MODEL]))
        part = gate * _dot(y, w_ref[...])
        merged = part if merged is None else merged + part
    y_out = _dot(merged.astype(BF16), wout_ref[...])
    o_ref[...] = x + mod_ref[0, 5:6, :] * _rms(y_out, gpost_ref[1:2, :])


def _merge_call(x2d, mod, gpre, gpost, wg_b, ret, yb, ssd, fox, ssmnorm, wro, wpo, wso, wfo, wout, seq):
    n, d = x2d.shape
    tm = TOK_TILE
    tiles_per_seq = seq // tm
    tok = lambda w: pl.BlockSpec((tm, w), lambda i: (i, 0))
    return pl.pallas_call(
        _merge_kernel,
        grid=(n // tm,),
        in_specs=[tok(d),
                  pl.BlockSpec((1, 3 * N_SUBLAYERS, d), lambda i: (i // tiles_per_seq, 0, 0)),
                  _resident(gpre.shape), _resident(gpost.shape),
                  pl.BlockSpec((d, REGION_W), lambda i: (0, 1), pipeline_mode=pl.Buffered(1)),
                  tok(RET_W), tok(POOL_W), tok(SSM_INNER), tok(FOX_W),
                  _resident(ssmnorm.shape), _resident(wro.shape), _resident(wpo.shape),
                  _resident(wso.shape), _resident(wfo.shape), _resident(wout.shape)],
        out_specs=tok(d),
        out_shape=jax.ShapeDtypeStruct((n, d), F32),
        compiler_params=_params(1),
        name="mixer_merge",
    )(x2d, mod, gpre, gpost, wg_b, ret.reshape(n, -1), yb.reshape(n, -1), ssd.reshape(n, -1),
      fox.reshape(n, -1), ssmnorm, wro, wpo, wso, wfo, wout)


def _rotary_tables(seq):
    half = RET_DK // 2
    inv = ROPE_BASE ** (-jnp.arange(half, dtype=F32) / half)
    ang = jnp.arange(seq, dtype=F32)[:, None] * inv[None, :]
    cos = jnp.cos(ang)
    sin = jnp.sin(ang)
    return jnp.concatenate([cos, cos], axis=-1), jnp.concatenate([-sin, sin], axis=-1)


def _place(vals, lane0):
    return jnp.zeros((1, LANES), F32).at[0, lane0:lane0 + vals.shape[0]].set(vals.astype(F32))


def kernel(x, c, w_ada, b_ada, norm_pre, norm_post, w_ffn_in, w_ffn_out, w_in, b_forget, pool_w, pool_scale, conv_w, conv_b, dt_bias, a_log, d_skip, ssm_norm, w_ret_out, w_pool_out, w_ssm_out, w_fox_out, w_out):
    bsz, seq, d = x.shape
    depth = w_ada.shape[0]
    n = bsz * seq
    mods = _ada_call(c, w_ada, b_ada)
    cos_t, sin_t = _rotary_tables(seq)
    tril_b = jnp.asarray(np.tril(np.ones((SSD_CHUNK, SSD_CHUNK), np.float32)), BF16)
    w_in_b = w_in.astype(BF16)
    w_ffn_in_b = w_ffn_in.astype(BF16)
    w_ffn_out_b = w_ffn_out.astype(BF16)
    for i in range(depth):
        mod = mods[i].reshape(bsz, 3 * N_SUBLAYERS, d)
        gpre, gpost = norm_pre[i], norm_post[i]
        x2d = _ffn_call(x.reshape(n, d), mod, gpre, gpost, w_ffn_in_b, w_ffn_out_b, i, 0, seq)

        w_regrouped = _regroup_call(w_in_b, i)
        w_proj = w_gate = w_regrouped
        sbias = _place(dt_bias[i], SC_DT) + _place(dt_bias[i], SC_ACUM) + _place(b_forget[i], SC_CUMF)
        alog = _place(a_log[i], SC_ACUM)
        dskip_x = jnp.repeat(d_skip[i].astype(F32), SSM_HEAD_DIM)[None, :]

        rq, rk, rv, yb, xbc, fq, fk, fv, scal, scal_t = _proj_call(
            x2d.reshape(bsz, seq, d), mod, gpre, w_proj, cos_t, sin_t, pool_w[i].astype(BF16),
            pool_scale[i][None, :], conv_w[i], conv_b[i][None, :], sbias, alog, tril_b)
        ret = _ret_call(rq, rk, rv)
        ssd = _ssd_call(xbc, scal, scal_t, dskip_x)
        fox = _fox_call(fq, fk, fv)
        x2d = _merge_call(x2d, mod, gpre, gpost, w_gate, ret, yb, ssd, fox, ssm_norm[i][None, :],
                          w_ret_out[i].astype(BF16), w_pool_out[i].astype(BF16),
                          w_ssm_out[i].astype(BF16), w_fox_out[i].astype(BF16),
                          w_out[i].astype(BF16), seq)
        x2d = _ffn_call(x2d, mod, gpre, gpost, w_ffn_in_b, w_ffn_out_b, i, 1, seq)
        x = x2d.reshape(bsz, seq, d)
    return x
```

```python
import functools

import jax
import jax.numpy as jnp
import numpy as np
from jax import lax
from jax.experimental import pallas as pl
from jax.experimental.pallas import tpu as pltpu

F32 = jnp.float32
BF16 = jnp.bfloat16

D_MODEL = 1024
RET_HEADS = 4
RET_DK = 128
RET_DV = 128
RET_W = RET_HEADS * RET_DK
ROPE_BASE = 10000.0
POOL_WINDOWS = (2, 4, 8, 16)
POOL_GROUPS = 4
POOL_GROUP_DIM = 128
POOL_W = POOL_GROUPS * POOL_GROUP_DIM
SSM_HEADS = 16
SSM_HEAD_DIM = 64
SSM_INNER = SSM_HEADS * SSM_HEAD_DIM
SSM_GROUPS = 2
SSM_STATE = 128
SSM_CONV = 4
SSM_XBC = SSM_INNER + 2 * SSM_GROUPS * SSM_STATE
FOX_HEADS = 4
FOX_HEAD_DIM = 128
FOX_W = FOX_HEADS * FOX_HEAD_DIM
N_BRANCH = 4
D_FF = 2816
N_SUBLAYERS = 3
RMS_EPS = 1e-6
GN_EPS = 1e-5
IN_SIZES = (RET_W, RET_W, RET_W, RET_W, POOL_W, SSM_INNER, SSM_XBC, SSM_HEADS,
            FOX_W, FOX_W, FOX_W, FOX_HEADS, N_BRANCH * D_MODEL)

LANES = 128
VMEM_LIMIT = 56 * 1024 * 1024
ADA_TN = 1152
TOK_TILE = 512
MXU_DIM = 256
FFN_CHUNK = 6 * MXU_DIM
RET_CHUNK = 256
SSD_CHUNK = 128
FOX_TILE = 1024
LOG2_E = float(np.log2(np.e))
NEG_BIG = -1e30

SC_DT = 0
SC_ACUM = 16
SC_CUMF = 32
FOX_SLAB = 2 * FOX_HEAD_DIM
FOX_PIECES = 3


def _sigmoid(v):
    return 1.0 / (1.0 + jnp.exp(-v))


def _silu(v):
    return v * (1.0 / (1.0 + jnp.exp2(v * (-LOG2_E))))


def _rms(v, gain):
    return v * lax.rsqrt(jnp.mean(v * v, axis=-1, keepdims=True) + RMS_EPS) * gain


def _modulated(x, mod_ref, gpre_ref, sub):
    shift = mod_ref[0, 3 * sub:3 * sub + 1, :]
    scale = mod_ref[0, 3 * sub + 1:3 * sub + 2, :]
    return _rms(x, gpre_ref[sub:sub + 1, :]) * (1.0 + scale) + shift


def _dot(a, b):
    return jnp.dot(a, b, preferred_element_type=F32)


def _dot_nt(a, b):
    return lax.dot_general(a, b, (((1,), (1,)), ((), ())), preferred_element_type=F32)


def _dot_tn(a, b):
    return lax.dot_general(a, b, (((0,), (0,)), ((), ())), preferred_element_type=F32)


def _resident(shape):
    nd = len(shape)
    return pl.BlockSpec(shape, lambda *_: (0,) * nd, pipeline_mode=pl.Buffered(1))


def _params(n_axes):
    return pltpu.CompilerParams(dimension_semantics=("arbitrary",) * n_axes,
                                vmem_limit_bytes=VMEM_LIMIT)


def _ada_kernel(c_ref, w_ref, b_ref, o_ref):
    sc = _silu(c_ref[...]).astype(BF16)
    o_ref[0] = _dot(sc, w_ref[0].astype(BF16)) + b_ref[0]


def _ada_call(c, w_ada, b_ada):
    depth, d, width = w_ada.shape
    bsz = c.shape[0]
    return pl.pallas_call(
        _ada_kernel,
        grid=(depth, width // ADA_TN),
        in_specs=[pl.BlockSpec((bsz, d), lambda l, n: (0, 0)),
                  pl.BlockSpec((1, d, ADA_TN), lambda l, n: (l, 0, n)),
                  pl.BlockSpec((1, 1, ADA_TN), lambda l, n: (l, 0, n))],
        out_specs=pl.BlockSpec((1, bsz, ADA_TN), lambda l, n: (l, 0, n)),
        out_shape=jax.ShapeDtypeStruct((depth, bsz, width), F32),
        compiler_params=_params(2),
        name="adaln",
    )(c, w_ada, b_ada.reshape(depth, 1, width))


def _ffn_kernel(x_ref, mod_ref, gpre_ref, gpost_ref, win_ref, wout_ref, o_ref, *, sub):
    x = x_ref[...]
    hb = _modulated(x, mod_ref, gpre_ref, sub).astype(BF16)
    y = None
    for lo in range(0, D_FF, FFN_CHUNK):
        hi = min(lo + FFN_CHUNK, D_FF)
        g = _dot(hb, win_ref[0, 0, :, lo:hi])
        u = _dot(hb, win_ref[0, 0, :, D_FF + lo:D_FF + hi])
        part = _dot((_silu(g) * u).astype(BF16), wout_ref[0, 0, lo:hi, :])
        y = part if y is None else y + part
    gate = mod_ref[0, 3 * sub + 2:3 * sub + 3, :]
    o_ref[...] = x + (0.5 * gate) * _rms(y, gpost_ref[sub:sub + 1, :])


def _ffn_call(x2d, mod, gpre, gpost, w_in_b, w_out_b, layer, which, seq):
    n, d = x2d.shape
    tiles_per_seq = seq // TOK_TILE
    sub = 2 * which
    pick = lambda w: pl.BlockSpec((1, 1) + w.shape[2:], lambda i: (layer, which, 0, 0),
                                  pipeline_mode=pl.Buffered(1))
    return pl.pallas_call(
        functools.partial(_ffn_kernel, sub=sub),
        grid=(n // TOK_TILE,),
        in_specs=[pl.BlockSpec((TOK_TILE, d), lambda i: (i, 0)),
                  pl.BlockSpec((1, 3 * N_SUBLAYERS, d), lambda i: (i // tiles_per_seq, 0, 0)),
                  _resident(gpre.shape), _resident(gpost.shape),
                  pick(w_in_b), pick(w_out_b)],
        out_specs=pl.BlockSpec((TOK_TILE, d), lambda i: (i, 0)),
        out_shape=jax.ShapeDtypeStruct((n, d), F32),
        compiler_params=_params(1),
        name="ffn",
    )(x2d, mod, gpre, gpost, w_in_b, w_out_b)


IN_OFFSETS = tuple(int(v) for v in np.cumsum((0,) + IN_SIZES[:-1]))
(IN_RQ, IN_RK, IN_RV, IN_RG, IN_PU, IN_SZ, IN_XBC, IN_DT, IN_FQ, IN_FK, IN_FV, IN_FF, IN_GL) = IN_OFFSETS
REGION_W = (RET_W + SSM_INNER + N_BRANCH * D_MODEL)


def _regroup_plan():
    shifts = sorted({off % LANES for off in IN_OFFSETS})
    kind_of = {s: k for k, s in enumerate(shifts)}
    scalar_kind, zero_kind = len(shifts), len(shifts) + 1
    place = np.zeros((len(shifts) + 2, 2 * LANES, LANES), np.float32)
    for s, k in kind_of.items():
        place[k, s + np.arange(LANES), np.arange(LANES)] = 1.0
    dt_lane, ff_lane = IN_DT % LANES, IN_FF % LANES
    assert dt_lane + SSM_HEADS <= LANES and ff_lane + FOX_HEADS <= LANES
    for hd in range(SSM_HEADS):
        place[scalar_kind, dt_lane + hd, SC_DT + hd] = 1.0
        place[scalar_kind, dt_lane + hd, SC_ACUM + hd] = 1.0
    for hd in range(FOX_HEADS):
        place[scalar_kind, LANES + ff_lane + hd, SC_CUMF + hd] = 1.0
    lo, hi, kind = [], [], []

    def span(off, width):
        for c in range(off, off + width, LANES):
            lo.append(c // LANES)
            hi.append(c // LANES + (1 if c % LANES else 0))
            kind.append(kind_of[c % LANES])

    proj = ((IN_RQ, RET_W), (IN_RK, RET_W), (IN_RV, RET_W), (IN_PU, POOL_W), (IN_XBC, SSM_XBC),
            (IN_FQ, FOX_W), (IN_FK, FOX_W), (IN_FV, FOX_W))
    for off, width in proj:
        span(off, width)
    lo.append(IN_DT // LANES); hi.append(IN_FF // LANES); kind.append(scalar_kind)
    while len(lo) < REGION_W // LANES:
        lo.append(0); hi.append(0); kind.append(zero_kind)
    for off, width in ((IN_RG, RET_W), (IN_SZ, SSM_INNER), (IN_GL, N_BRANCH * D_MODEL)):
        span(off, width)
    assert len(lo) == 2 * REGION_W // LANES
    as_i32 = lambda v: jnp.asarray(np.asarray(v, np.int32))
    need = place[:, LANES:, :].any(axis=2).astype(np.float32)[:, None, :]
    return as_i32(lo), as_i32(hi), as_i32(kind), jnp.asarray(place, BF16), jnp.asarray(need, F32)


def _regroup_kernel(lo_ref, hi_ref, kind_ref, wlo_ref, whi_ref, place_ref, need_ref, o_ref):
    del lo_ref, hi_ref
    kind = kind_ref[pl.program_id(0)]
    second = whi_ref[0]
    second = jnp.where(need_ref[kind] > 0.0, second, jnp.zeros_like(second))
    pair = jnp.concatenate([wlo_ref[0], second], axis=1).astype(BF16)
    o_ref[...] = _dot(pair, place_ref[kind]).astype(BF16)


def _regroup_call(w_in, layer):
    _, d, _ = w_in.shape
    lo, hi, kind, place, need = _regroup_plan()
    n_blocks = lo.shape[0]
    grid_spec = pltpu.PrefetchScalarGridSpec(
        num_scalar_prefetch=3,
        grid=(n_blocks,),
        in_specs=[pl.BlockSpec((1, d, LANES), lambda o, lo, hi, kind: (layer, 0, lo[o])),
                  pl.BlockSpec((1, d, LANES), lambda o, lo, hi, kind: (layer, 0, hi[o])),
                  pl.BlockSpec(place.shape, lambda o, lo, hi, kind: (0, 0, 0)),
                  pl.BlockSpec(need.shape, lambda o, lo, hi, kind: (0, 0, 0))],
        out_specs=pl.BlockSpec((d, LANES), lambda o, lo, hi, kind: (0, o)))
    return pl.pallas_call(
        _regroup_kernel,
        grid_spec=grid_spec,
        out_shape=jax.ShapeDtypeStruct((d, n_blocks * LANES), BF16),
        compiler_params=_params(1),
        name="regroup_w_in",
    )(lo, hi, kind, w_in, w_in, place, need)


PJ_RQ = 0
PJ_RK = PJ_RQ + RET_W
PJ_RV = PJ_RK + RET_W
PJ_PU = PJ_RV + RET_W
PJ_XBC = PJ_PU + POOL_W
PJ_FQ = PJ_XBC + SSM_XBC
PJ_FK = PJ_FQ + FOX_W
PJ_FV = PJ_FK + FOX_W
PJ_SC = PJ_FV + FOX_W
PJ_WIDTH = PJ_SC + LANES
POOL_HALO = 16
CONV_COLS = SSM_XBC // 3
CONV_HALO = 8


def _cumsum_rows(tril_b, s):
    p1 = s.astype(BF16)
    r1 = s - p1.astype(F32)
    p2 = r1.astype(BF16)
    p3 = (r1 - p2.astype(F32)).astype(BF16)
    return _dot(tril_b, p1) + _dot(tril_b, p2) + _dot(tril_b, p3)


def _proj_kernel(x_ref, mod_ref, gpre_ref, w_ref, cos_ref, sin_ref, poolw_ref, pscale_ref,
                 convw_ref, convb_ref, sbias_ref, alog_ref, tril_ref, selq_ref, selk_ref,
                 ea_sel_ref, dt_sel_ref, dskip_ref, idec_ref, qdec_ref, kdec_ref,
                 yb_ref, fq_ref, fk_ref, fv_ref, ret_ref, ssd_ref,
                 pbuf, cbuf, carry, rq_ref, rk_ref, rv_ref, xbc_ref, scal_ref, scalT_ref,
                 ret_state, ssd_state, *, chunk_decay):
    j = pl.program_id(1)
    tm = x_ref.shape[1]

    @pl.when(j == 0)
    def _():
        pbuf[0:POOL_HALO, :] = jnp.zeros((POOL_HALO, POOL_W), F32)
        cbuf[0:CONV_HALO, :] = jnp.zeros((CONV_HALO, SSM_XBC), F32)
        carry[...] = jnp.zeros_like(carry)
        ret_state[...] = jnp.zeros_like(ret_state)
        ssd_state[...] = jnp.zeros_like(ssd_state)

    @pl.when(j > 0)
    def _():
        pbuf[0:POOL_HALO, :] = pbuf[tm:tm + POOL_HALO, :]
        cbuf[0:CONV_HALO, :] = cbuf[tm:tm + CONV_HALO, :]

    hb = _modulated(x_ref[0], mod_ref, gpre_ref, 1).astype(BF16)
    cos = cos_ref[...]
    sin = sin_ref[...]
    pos = j * tm + lax.broadcasted_iota(jnp.int32, (tm, POOL_GROUP_DIM), 0)

    def rotary(base, ref, scl):
        cos_s, sin_s = (cos, sin) if scl is None else (cos * scl, sin * scl)
        t = _dot(hb, w_ref[:, base:base + RET_W])
        for hd in range(RET_HEADS):
            th = t[:, hd * RET_DK:(hd + 1) * RET_DK]
            r = th * cos_s + pltpu.roll(th, RET_DK // 2, 1) * sin_s
            ref[0, :, hd * RET_DK:(hd + 1) * RET_DK] = r.astype(BF16)

    def pool_mm():
        pu = _dot(hb, w_ref[:, PJ_PU:PJ_PU + POOL_W])
        pbuf[POOL_HALO:POOL_HALO + tm, :] = pu
        return pu

    def pool_ep(pu, g):
        win = POOL_WINDOWS[g]
        ls = slice(g * POOL_GROUP_DIM, (g + 1) * POOL_GROUP_DIM)
        cur = pu[:, ls]
        acc = pbuf[:, ls]
        span = 1
        while span < win:
            acc = acc + pltpu.roll(acc, span, 0)
            span *= 2
        count = jnp.minimum(pos + 1, win).astype(F32)
        pooled = acc[POOL_HALO:, :] / count - cur
        mixed = _dot(pooled.astype(BF16), poolw_ref[g]) * pscale_ref[:, ls]
        yb_ref[0, :, ls] = mixed.astype(BF16)

    def conv_mm(c0):
        cs = slice(c0, c0 + CONV_COLS)
        xr = _dot(hb, w_ref[:, PJ_XBC + c0:PJ_XBC + c0 + CONV_COLS])
        cbuf[CONV_HALO:CONV_HALO + tm, cs] = xr
        return xr

    def conv_ep(xr, c0):
        cs = slice(c0, c0 + CONV_COLS)
        conv = xr * convw_ref[SSM_CONV - 1:SSM_CONV, cs] + convb_ref[:, cs]
        for k in range(SSM_CONV - 1):
            off = CONV_HALO - (SSM_CONV - 1) + k
            conv = conv + cbuf[off:off + tm, cs] * convw_ref[k:k + 1, cs]
        xbc_ref[0, :, cs] = _silu(conv).astype(BF16)

    def fox_mm(base, ref, scl, transposed):
        f = _dot(hb, w_ref[:, base:base + FOX_W])
        if scl is not None:
            f = f * scl
        for hd in range(FOX_HEADS):
            src_l = slice(hd * FOX_HEAD_DIM, (hd + 1) * FOX_HEAD_DIM)
            dst_l = slice(hd * FOX_SLAB, hd * FOX_SLAB + FOX_HEAD_DIM)
            if transposed:
                ref[0, dst_l, :] = f[:, src_l].T.astype(BF16)
            else:
                ref[0, :, dst_l] = f[:, src_l].astype(BF16)

    pu = pool_mm()
    xr0 = conv_mm(0)
    xr1 = conv_mm(CONV_COLS)
    xr2 = conv_mm(2 * CONV_COLS)
    fox_mm(PJ_FQ, fq_ref, FOX_HEAD_DIM ** -0.5 * LOG2_E, False)
    for g in range(POOL_GROUPS):
        pool_ep(pu, g)
    fox_mm(PJ_FK, fk_ref, None, True)
    conv_ep(xr0, 0)
    fox_mm(PJ_FV, fv_ref, None, False)
    conv_ep(xr1, CONV_COLS)
    rv_ref[0] = _dot(hb, w_ref[:, PJ_RV:PJ_RV + RET_W]).astype(BF16)
    conv_ep(xr2, 2 * CONV_COLS)
    rotary(PJ_RQ, rq_ref, None)
    rotary(PJ_RK, rk_ref, RET_DK ** -0.5)


    z = _dot(hb, w_ref[:, PJ_SC:PJ_SC + LANES]) + sbias_ref[...]
    tail = jnp.log1p(jnp.exp(-jnp.abs(z)))
    softplus = jnp.maximum(z, 0.0) + tail
    log_sig = jnp.minimum(z, 0.0) - tail
    lane = lax.broadcasted_iota(jnp.int32, (tm, LANES), 1)
    a_row = -jnp.exp(alog_ref[...])
    in_acum = (lane >= SC_ACUM) & (lane < SC_CUMF)
    in_cumf = (lane >= SC_CUMF) & (lane < SC_CUMF + FOX_HEADS)
    src = jnp.where(in_acum, softplus * a_row, jnp.where(in_cumf, log_sig, 0.0))
    tril_b = tril_ref[...]
    run = carry[...]
    lane_c = lax.broadcasted_iota(jnp.int32, (SSD_CHUNK, LANES), 1)
    for c0 in range(0, tm, SSD_CHUNK):
        rows = slice(c0, c0 + SSD_CHUNK)
        local = _cumsum_rows(tril_b, src[rows, :])
        total = local + run
        scal_ref[0, rows, :] = jnp.where(lane_c < SC_ACUM, softplus[rows, :],
                                         jnp.where(lane_c < SC_CUMF, local, total))
        run = total[SSD_CHUNK - 1:SSD_CHUNK, :]
    carry[...] = run
    sc_all = scal_ref[0]
    scalT_ref[0] = sc_all.T

    bias = sc_all * LOG2_E
    c1 = bias.astype(BF16)
    r1 = bias - c1.astype(F32)
    c2 = r1.astype(BF16)
    c3 = (r1 - c2.astype(F32)).astype(BF16)
    lane_h = lax.broadcasted_iota(jnp.int32, (tm, FOX_HEAD_DIM), 1)
    ones_q = jnp.where((lane_h >= FOX_PIECES) & (lane_h < 2 * FOX_PIECES), 1.0, 0.0)
    ones_k = jnp.where(lane_h < FOX_PIECES, 1.0, 0.0)
    ones_v = jnp.where(lane_h == 0, 1.0, 0.0).astype(BF16)
    bias_q = _dot(c1, selq_ref[0]) + _dot(c2, selq_ref[1]) + _dot(c3, selq_ref[2])
    bias_k = _dot(c1, selk_ref[0]) + _dot(c2, selk_ref[1]) + _dot(c3, selk_ref[2])
    for hd in range(FOX_HEADS):
        src_l = slice(hd * FOX_HEAD_DIM, (hd + 1) * FOX_HEAD_DIM)
        dst_l = slice(hd * FOX_SLAB + FOX_HEAD_DIM, (hd + 1) * FOX_SLAB)
        fq_ref[0, :, dst_l] = (bias_q[:, src_l] + ones_q).astype(BF16)
        fk_ref[0, dst_l, :] = (ones_k - bias_k[:, src_l]).T.astype(BF16)
        fv_ref[0, :, dst_l] = ones_v

    _ret_body(rq_ref, rk_ref, rv_ref, idec_ref, qdec_ref, kdec_ref, ret_ref, ret_state, chunk_decay)
    _ssd_body(xbc_ref, scal_ref, scalT_ref, ea_sel_ref, dt_sel_ref, dskip_ref, ssd_ref, ssd_state)


def _fox_select_tables():
    selq = np.zeros((FOX_PIECES, LANES, FOX_W), np.float32)
    selk = np.zeros((FOX_PIECES, LANES, FOX_W), np.float32)
    for piece in range(FOX_PIECES):
        for hd in range(FOX_HEADS):
            selq[piece, SC_CUMF + hd, hd * FOX_HEAD_DIM + piece] = 1.0
            selk[piece, SC_CUMF + hd, hd * FOX_HEAD_DIM + FOX_PIECES + piece] = 1.0
    return jnp.asarray(selq, BF16), jnp.asarray(selk, BF16)


def _proj_call(x, mod, gpre, w_b, cos_t, sin_t, poolw_b, pscale, convw, convb, sbias, alog, tril_b, dskip_x):
    bsz, seq, d = x.shape
    tm = TOK_TILE
    selq, selk = _fox_select_tables()
    ea_sel, dt_sel = _ssd_tables()
    idec, qdec, kdec, chunk_decay = _ret_tables()
    tok = lambda w: pl.BlockSpec((1, tm, w), lambda b, j: (b, j, 0))
    bf = lambda w: jax.ShapeDtypeStruct((bsz, seq, w), BF16)
    fox_w = FOX_HEADS * FOX_SLAB
    consts = (poolw_b, pscale, convw, convb, sbias, alog, tril_b, selq, selk,
              ea_sel, dt_sel, dskip_x, idec, qdec, kdec)
    return pl.pallas_call(
        functools.partial(_proj_kernel, chunk_decay=chunk_decay),
        grid=(bsz, seq // tm),
        in_specs=[tok(d),
                  pl.BlockSpec((1, 3 * N_SUBLAYERS, d), lambda b, j: (b, 0, 0)),
                  _resident(gpre.shape),
                  pl.BlockSpec((d, REGION_W), lambda b, j: (0, 0), pipeline_mode=pl.Buffered(1)),
                  pl.BlockSpec((tm, RET_DK), lambda b, j: (j, 0)),
                  pl.BlockSpec((tm, RET_DK), lambda b, j: (j, 0))] + [_resident(c.shape) for c in consts],
        out_specs=[tok(POOL_W), tok(fox_w), pl.BlockSpec((1, fox_w, tm), lambda b, j: (b, 0, j)), tok(fox_w),
                   tok(RET_W), tok(SSM_INNER)],
        out_shape=[bf(POOL_W), bf(fox_w), jax.ShapeDtypeStruct((bsz, fox_w, seq), BF16), bf(fox_w),
                   bf(RET_W), bf(SSM_INNER)],
        scratch_shapes=[pltpu.VMEM((POOL_HALO + tm, POOL_W), F32),
                        pltpu.VMEM((CONV_HALO + tm, SSM_XBC), F32),
                        pltpu.VMEM((1, LANES), F32),
                        pltpu.VMEM((1, tm, RET_W), BF16), pltpu.VMEM((1, tm, RET_W), BF16),
                        pltpu.VMEM((1, tm, RET_W), BF16), pltpu.VMEM((1, tm, SSM_XBC), BF16),
                        pltpu.VMEM((1, tm, LANES), F32), pltpu.VMEM((1, LANES, tm), F32),
                        pltpu.VMEM((RET_HEADS, RET_DK, RET_DV), F32),
                        pltpu.VMEM((SSM_GROUPS, SSM_STATE, GROUP_W), F32)],
        compiler_params=_params(2),
        name="mixer_proj",
    )(x, mod, gpre, w_b, cos_t, sin_t, *consts)


def _ret_body(q_ref, k_ref, v_ref, idec_ref, qdec_ref, kdec_ref, o_ref, state, chunk_decay):
    tile = q_ref.shape[1]
    for hd in range(RET_HEADS):
        ls = slice(hd * RET_DK, (hd + 1) * RET_DK)
        for c0 in range(0, tile, RET_CHUNK):
            rows = slice(c0, c0 + RET_CHUNK)
            q = q_ref[0, rows, ls]
            k = k_ref[0, rows, ls]
            v = v_ref[0, rows, ls]
            st = state[hd]
            scores = _dot_nt(q, k) * idec_ref[hd]
            o = _dot(scores.astype(BF16), v) + _dot(q, st.astype(BF16)) * qdec_ref[hd]
            kd = (k.astype(F32) * kdec_ref[hd]).astype(BF16)
            state[hd] = chunk_decay[hd] * st + _dot_tn(kd, v)
            dev = o - jnp.mean(o, axis=-1, keepdims=True)
            var = jnp.mean(dev * dev, axis=-1, keepdims=True)
            o_ref[0, rows, ls] = (dev * lax.rsqrt(var + GN_EPS)).astype(BF16)


def _ret_tables():
    log_gamma = np.log1p(-np.exp2(-5.0 - np.arange(RET_HEADS, dtype=np.float64)))
    idx = np.arange(RET_CHUNK, dtype=np.float64)
    rel = idx[:, None] - idx[None, :]
    intra = np.where(rel >= 0, np.exp(log_gamma[:, None, None] * np.maximum(rel, 0.0)), 0.0)
    q_decay = np.exp(log_gamma[:, None] * (idx + 1.0))
    k_decay = np.exp(log_gamma[:, None] * (RET_CHUNK - 1.0 - idx))
    widen = lambda t: np.broadcast_to(t[:, :, None], (RET_HEADS, RET_CHUNK, RET_DK))
    chunk_decay = tuple(float(v) for v in np.exp(log_gamma * RET_CHUNK))
    return (jnp.asarray(intra, F32), jnp.asarray(widen(q_decay), F32),
            jnp.asarray(widen(k_decay), F32), chunk_decay)


HEADS_PER_GROUP = SSM_HEADS // SSM_GROUPS
GROUP_W = HEADS_PER_GROUP * SSM_HEAD_DIM


def _ssd_body(xbc_ref, scal_ref, scalT_ref, ea_sel_ref, dt_sel_ref, dskip_ref, o_ref, state):
    tile = xbc_ref.shape[1]
    cl = SSD_CHUNK
    lane = lax.broadcasted_iota(jnp.int32, (cl, LANES), 1)
    row_i = lax.broadcasted_iota(jnp.int32, (cl, cl), 0)
    col_i = lax.broadcasted_iota(jnp.int32, (cl, cl), 1)
    causal = col_i <= row_i
    low_half = lane < SSM_HEAD_DIM
    ea_sel = ea_sel_ref[...]
    dt_sel = dt_sel_ref[...]
    for c0 in range(0, tile, cl):
        rows = slice(c0, c0 + cl)
        sc = scal_ref[0, rows, :]
        sc2 = sc * LOG2_E
        in_acum = (lane >= SC_ACUM) & (lane < SC_CUMF)
        acum = jnp.where(in_acum, sc, 0.0)
        ea = jnp.exp(acum)
        dec_end = jnp.exp(acum[cl - 1:cl, :] - acum)
        dt = jnp.where(lane < SC_ACUM, sc, 0.0)
        ea_hi = ea.astype(BF16)
        ea_lo = (ea - ea_hi.astype(F32)).astype(BF16)
        ea_x = _dot(ea_hi, ea_sel) + _dot(ea_lo, ea_sel)
        w_x = _dot(dec_end.astype(BF16), ea_sel) * _dot(dt.astype(BF16), dt_sel)
        xs_b = xbc_ref[0, rows, 0:SSM_INNER]
        xs = xs_b.astype(F32)
        xw_b = (xs * w_x).astype(BF16)
        for g in range(SSM_GROUPS):
            gs = slice(g * GROUP_W, (g + 1) * GROUP_W)
            bm = xbc_ref[0, rows, SSM_INNER + g * SSM_STATE:SSM_INNER + (g + 1) * SSM_STATE]
            cm_lo = SSM_INNER + SSM_GROUPS * SSM_STATE + g * SSM_STATE
            cm = xbc_ref[0, rows, cm_lo:cm_lo + SSM_STATE]
            cb = _dot_nt(cm, bm)
            st = state[g]
            y_inter = _dot(cm, st.astype(BF16)) * ea_x[:, gs]
            state[g] = st * ea_x[cl - 1:cl, gs] + _dot_tn(bm, xw_b[:, gs])
            for pair in range(HEADS_PER_GROUP // 2):
                mats = []
                for hh in (2 * pair, 2 * pair + 1):
                    hd = g * HEADS_PER_GROUP + hh
                    a_col = jnp.broadcast_to(sc2[:, SC_ACUM + hd:SC_ACUM + hd + 1], (cl, cl))
                    a_row = scalT_ref[0, SC_ACUM + hd:SC_ACUM + hd + 1, rows] * LOG2_E
                    dt_row = scalT_ref[0, SC_DT + hd:SC_DT + hd + 1, rows]
                    lmat = jnp.exp2(jnp.where(causal, a_col - (a_row - jnp.log2(dt_row)), NEG_BIG))
                    mats.append((lmat * cb).astype(BF16))
                lo = g * GROUP_W + pair * LANES
                x_pair = xs_b[:, lo:lo + LANES]
                zero = jnp.zeros_like(x_pair)
                rhs = jnp.concatenate([jnp.where(low_half, x_pair, zero),
                                       jnp.where(low_half, zero, x_pair)], axis=0)
                y_pair = _dot(jnp.concatenate(mats, axis=1), rhs)
                y_pair = y_pair + y_inter[:, pair * LANES:(pair + 1) * LANES]
                y_pair = y_pair + dskip_ref[:, lo:lo + LANES] * xs[:, lo:lo + LANES]
                o_ref[0, rows, lo:lo + LANES] = y_pair.astype(BF16)


def _ssd_tables():
    ea_sel = np.zeros((LANES, SSM_INNER), np.float32)
    dt_sel = np.zeros((LANES, SSM_INNER), np.float32)
    for hd in range(SSM_HEADS):
        ea_sel[SC_ACUM + hd, hd * SSM_HEAD_DIM:(hd + 1) * SSM_HEAD_DIM] = 1.0
        dt_sel[SC_DT + hd, hd * SSM_HEAD_DIM:(hd + 1) * SSM_HEAD_DIM] = 1.0
    return jnp.asarray(ea_sel, BF16), jnp.asarray(dt_sel, BF16)


FOX_HALVES = 2
FOX_KV = 1024


def _fox_kernel(q_ref, k_ref, v_ref, o_ref, m_s, acc_s, s_a, s_b):
    tq, tk = FOX_TILE, FOX_KV
    th = tq // FOX_HALVES
    n_q = q_ref.shape[1] // tq
    jobs = [(i, t) for i in range(n_q) for t in range(-(-((i + 1) * tq) // tk))]
    bufs = (s_a, s_b)

    def first_row(i, t):
        return max(0, t * tk - i * tq) // th * th

    def scores(s_ref, i, t):
        r0 = first_row(i, t)
        s_ref[r0:, :] = _dot(q_ref[0, i * tq + r0:(i + 1) * tq, :],
                             k_ref[0, :, t * tk:(t + 1) * tk])

    def consume(s_ref, i, t):
        slot = i % 2
        for half in range(first_row(i, t) // th, FOX_HALVES):
            rows = slice(half * th, (half + 1) * th)
            row_lo = i * tq + half * th
            ncols = min(tk, row_lo + th - t * tk)
            s = s_ref[rows, :ncols]
            if t * tk + ncols - 1 > row_lo:
                row_i = row_lo + lax.broadcasted_iota(jnp.int32, (th, ncols), 0)
                col_i = t * tk + lax.broadcasted_iota(jnp.int32, (th, ncols), 1)
                s = jnp.where(col_i <= row_i, s, NEG_BIG)
            m_old = m_s[slot, rows, :]
            m_new = jnp.maximum(m_old, jnp.max(s, axis=-1, keepdims=True))
            p = jnp.concatenate([jnp.exp2(s[:, c0:c0 + LANES] - m_new)
                                 for c0 in range(0, ncols, LANES)], axis=1)
            alpha = jnp.exp2(m_old - m_new)
            pv = _dot(p.astype(BF16), v_ref[0, t * tk:t * tk + ncols, :])
            for c0 in range(0, FOX_SLAB, LANES):
                acc_s[slot, rows, c0:c0 + LANES] = (alpha * acc_s[slot, rows, c0:c0 + LANES]
                                                    + pv[:, c0:c0 + LANES])
            m_s[slot, rows, :] = m_new

    scores(bufs[0], *jobs[0])
    for n, (i, t) in enumerate(jobs):
        if t == 0:
            m_s[i % 2] = jnp.full((tq, LANES), NEG_BIG, F32)
            acc_s[i % 2] = jnp.zeros((tq, FOX_SLAB), F32)
        if n + 1 < len(jobs):
            scores(bufs[(n + 1) % 2], *jobs[n + 1])
        consume(bufs[n % 2], i, t)
        if n + 1 == len(jobs) or jobs[n + 1][0] != i:
            acc = acc_s[i % 2]
            o_ref[0, i * tq:(i + 1) * tq, :] = (acc[:, :FOX_HEAD_DIM]
                                                / acc[:, FOX_HEAD_DIM:FOX_HEAD_DIM + 1]).astype(BF16)


def _fox_call(fq, fk, fv):
    bsz, seq, _ = fq.shape
    tq = FOX_TILE
    return pl.pallas_call(
        _fox_kernel,
        grid=(bsz, FOX_HEADS),
        in_specs=[pl.BlockSpec((1, seq, FOX_SLAB), lambda b, h: (b, 0, h)),
                  pl.BlockSpec((1, FOX_SLAB, seq), lambda b, h: (b, h, 0)),
                  pl.BlockSpec((1, seq, FOX_SLAB), lambda b, h: (b, 0, h))],
        out_specs=pl.BlockSpec((1, seq, FOX_HEAD_DIM), lambda b, h: (b, 0, h)),
        out_shape=jax.ShapeDtypeStruct((bsz, seq, FOX_W), BF16),
        scratch_shapes=[pltpu.VMEM((2, tq, LANES), F32), pltpu.VMEM((2, tq, FOX_SLAB), F32),
                        pltpu.VMEM((tq, FOX_KV), F32), pltpu.VMEM((tq, FOX_KV), F32)],
        compiler_params=_params(2),
        name="fox_attention",
    )(fq, fk, fv)


MG_RG = 0
MG_SZ = MG_RG + RET_W
MG_GL = MG_SZ + SSM_INNER
MG_WIDTH = MG_GL + N_BRANCH * D_MODEL


def _merge_kernel(x_ref, mod_ref, gpre_ref, gpost_ref, wg_ref, ret_ref, yb_ref, ssd_ref, fox_ref,
                  ssmnorm_ref, wro_ref, wpo_ref, wso_ref, wfo_ref, wout_ref, o_ref):
    x = x_ref[...]
    hb = _modulated(x, mod_ref, gpre_ref, 1).astype(BF16)
    rg = _dot(hb, wg_ref[:, MG_RG:MG_RG + RET_W])
    y_a = (_silu(rg) * ret_ref[...].astype(F32)).astype(BF16)
    sz = _dot(hb, wg_ref[:, MG_SZ:MG_SZ + SSM_INNER])
    y_c = _rms(ssd_ref[...].astype(F32) * _silu(sz), ssmnorm_ref[...]).astype(BF16)
    merged = None
    for br, (y, w_ref) in enumerate(((y_a, wro_ref), (yb_ref[...], wpo_ref),
                                     (y_c, wso_ref), (fox_ref[...], wfo_ref))):
        lo = MG_GL + br * D_MODEL
        gate = _sigmoid(_dot(hb, wg_ref[:, lo:lo + D_MODEL]))
        part = gate * _dot(y, w_ref[...])
        merged = part if merged is None else merged + part
    y_out = _dot(merged.astype(BF16), wout_ref[...])
    o_ref[...] = x + mod_ref[0, 5:6, :] * _rms(y_out, gpost_ref[1:2, :])


def _merge_call(x2d, mod, gpre, gpost, wg_b, ret, yb, ssd, fox, ssmnorm, wro, wpo, wso, wfo, wout, seq):
    n, d = x2d.shape
    tm = TOK_TILE
    tiles_per_seq = seq // tm
    tok = lambda w: pl.BlockSpec((tm, w), lambda i: (i, 0))
    return pl.pallas_call(
        _merge_kernel,
        grid=(n // tm,),
        in_specs=[tok(d),
                  pl.BlockSpec((1, 3 * N_SUBLAYERS, d), lambda i: (i // tiles_per_seq, 0, 0)),
                  _resident(gpre.shape), _resident(gpost.shape),
                  pl.BlockSpec((d, REGION_W), lambda i: (0, 1), pipeline_mode=pl.Buffered(1)),
                  tok(RET_W), tok(POOL_W), tok(SSM_INNER), tok(FOX_W),
                  _resident(ssmnorm.shape), _resident(wro.shape), _resident(wpo.shape),
                  _resident(wso.shape), _resident(wfo.shape), _resident(wout.shape)],
        out_specs=tok(d),
        out_shape=jax.ShapeDtypeStruct((n, d), F32),
        compiler_params=_params(1),
        name="mixer_merge",
    )(x2d, mod, gpre, gpost, wg_b, ret.reshape(n, -1), yb.reshape(n, -1), ssd.reshape(n, -1),
      fox.reshape(n, -1), ssmnorm, wro, wpo, wso, wfo, wout)


def _rotary_tables(seq):
    half = RET_DK // 2
    inv = ROPE_BASE ** (-jnp.arange(half, dtype=F32) / half)
    ang = jnp.arange(seq, dtype=F32)[:, None] * inv[None, :]
    cos = jnp.cos(ang)
    sin = jnp.sin(ang)
    return jnp.concatenate([cos, cos], axis=-1), jnp.concatenate([-sin, sin], axis=-1)


def _place(vals, lane0):
    return jnp.zeros((1, LANES), F32).at[0, lane0:lane0 + vals.shape[0]].set(vals.astype(F32))


def kernel(x, c, w_ada, b_ada, norm_pre, norm_post, w_ffn_in, w_ffn_out, w_in, b_forget, pool_w, pool_scale, conv_w, conv_b, dt_bias, a_log, d_skip, ssm_norm, w_ret_out, w_pool_out, w_ssm_out, w_fox_out, w_out):
    bsz, seq, d = x.shape
    depth = w_ada.shape[0]
    n = bsz * seq
    mods = _ada_call(c, w_ada, b_ada)
    cos_t, sin_t = _rotary_tables(seq)
    tril_b = jnp.asarray(np.tril(np.ones((SSD_CHUNK, SSD_CHUNK), np.float32)), BF16)
    w_in_b = w_in.astype(BF16)
    w_ffn_in_b = w_ffn_in.astype(BF16)
    w_ffn_out_b = w_ffn_out.astype(BF16)
    for i in range(depth):
        mod = mods[i].reshape(bsz, 3 * N_SUBLAYERS, d)
        gpre, gpost = norm_pre[i], norm_post[i]
        x2d = _ffn_call(x.reshape(n, d), mod, gpre, gpost, w_ffn_in_b, w_ffn_out_b, i, 0, seq)

        w_regrouped = _regroup_call(w_in_b, i)
        w_proj = w_gate = w_regrouped
        sbias = _place(dt_bias[i], SC_DT) + _place(dt_bias[i], SC_ACUM) + _place(b_forget[i], SC_CUMF)
        alog = _place(a_log[i], SC_ACUM)
        dskip_x = jnp.repeat(d_skip[i].astype(F32), SSM_HEAD_DIM)[None, :]

        yb, fq, fk, fv, ret, ssd = _proj_call(
            x2d.reshape(bsz, seq, d), mod, gpre, w_proj, cos_t, sin_t, pool_w[i].astype(BF16),
            pool_scale[i][None, :], conv_w[i], conv_b[i][None, :], sbias, alog, tril_b, dskip_x)
        fox = _fox_call(fq, fk, fv)
        x2d = _merge_call(x2d, mod, gpre, gpost, w_gate, ret, yb, ssd, fox, ssm_norm[i][None, :],
                          w_ret_out[i].astype(BF16), w_pool_out[i].astype(BF16),
                          w_ssm_out[i].astype(BF16), w_fox_out[i].astype(BF16),
                          w_out[i].astype(BF16), seq)
        x2d = _ffn_call(x2d, mod, gpre, gpost, w_ffn_in_b, w_ffn_out_b, i, 1, seq)
        x = x2d.reshape(bsz, seq, d)
    return x
```

```python
import functools

import jax
import jax.numpy as jnp
import numpy as np
from jax import lax
from jax.experimental import pallas as pl
from jax.experimental.pallas import tpu as pltpu

F32 = jnp.float32
BF16 = jnp.bfloat16

D_MODEL = 1024
RET_HEADS = 4
RET_DK = 128
RET_DV = 128
RET_W = RET_HEADS * RET_DK
ROPE_BASE = 10000.0
POOL_WINDOWS = (2, 4, 8, 16)
POOL_GROUPS = 4
POOL_GROUP_DIM = 128
POOL_W = POOL_GROUPS * POOL_GROUP_DIM
SSM_HEADS = 16
SSM_HEAD_DIM = 64
SSM_INNER = SSM_HEADS * SSM_HEAD_DIM
SSM_GROUPS = 2
SSM_STATE = 128
SSM_CONV = 4
SSM_XBC = SSM_INNER + 2 * SSM_GROUPS * SSM_STATE
FOX_HEADS = 4
FOX_HEAD_DIM = 128
FOX_W = FOX_HEADS * FOX_HEAD_DIM
N_BRANCH = 4
D_FF = 2816
N_SUBLAYERS = 3
RMS_EPS = 1e-6
GN_EPS = 1e-5
IN_SIZES = (RET_W, RET_W, RET_W, RET_W, POOL_W, SSM_INNER, SSM_XBC, SSM_HEADS,
            FOX_W, FOX_W, FOX_W, FOX_HEADS, N_BRANCH * D_MODEL)

LANES = 128
VMEM_LIMIT = 56 * 1024 * 1024
ADA_TN = 1152
TOK_TILE = 512
MXU_DIM = 256
FFN_CHUNK = 6 * MXU_DIM
RET_CHUNK = 256
SSD_CHUNK = 128
FOX_TILE = 1024
LOG2_E = float(np.log2(np.e))
NEG_BIG = -1e30

SC_DT = 0
SC_ACUM = 16
SC_CUMF = 32
FOX_SLAB = 2 * FOX_HEAD_DIM
FOX_PIECES = 3


def _sigmoid(v):
    return 1.0 / (1.0 + jnp.exp(-v))


def _silu(v):
    return v * (1.0 / (1.0 + jnp.exp2(v * (-LOG2_E))))


def _rms(v, gain):
    return v * lax.rsqrt(jnp.mean(v * v, axis=-1, keepdims=True) + RMS_EPS) * gain


def _modulated(x, mod_ref, gpre_ref, sub):
    shift = mod_ref[0, 3 * sub:3 * sub + 1, :]
    scale = mod_ref[0, 3 * sub + 1:3 * sub + 2, :]
    return _rms(x, gpre_ref[sub:sub + 1, :]) * (1.0 + scale) + shift


def _dot(a, b):
    return jnp.dot(a, b, preferred_element_type=F32)


def _dot_nt(a, b):
    return lax.dot_general(a, b, (((1,), (1,)), ((), ())), preferred_element_type=F32)


def _dot_tn(a, b):
    return lax.dot_general(a, b, (((0,), (0,)), ((), ())), preferred_element_type=F32)


def _resident(shape):
    nd = len(shape)
    return pl.BlockSpec(shape, lambda *_: (0,) * nd, pipeline_mode=pl.Buffered(1))


def _params(n_axes):
    return pltpu.CompilerParams(dimension_semantics=("arbitrary",) * n_axes,
                                vmem_limit_bytes=VMEM_LIMIT)


def _ada_kernel(c_ref, w_ref, b_ref, o_ref):
    sc = _silu(c_ref[...]).astype(BF16)
    o_ref[0] = _dot(sc, w_ref[0].astype(BF16)) + b_ref[0]


def _ada_call(c, w_ada, b_ada):
    depth, d, width = w_ada.shape
    bsz = c.shape[0]
    return pl.pallas_call(
        _ada_kernel,
        grid=(depth, width // ADA_TN),
        in_specs=[pl.BlockSpec((bsz, d), lambda l, n: (0, 0)),
                  pl.BlockSpec((1, d, ADA_TN), lambda l, n: (l, 0, n)),
                  pl.BlockSpec((1, 1, ADA_TN), lambda l, n: (l, 0, n))],
        out_specs=pl.BlockSpec((1, bsz, ADA_TN), lambda l, n: (l, 0, n)),
        out_shape=jax.ShapeDtypeStruct((depth, bsz, width), F32),
        compiler_params=_params(2),
        name="adaln",
    )(c, w_ada, b_ada.reshape(depth, 1, width))


def _ffn_kernel(x_ref, mod_ref, gpre_ref, gpost_ref, win_ref, wout_ref, o_ref, *, sub):
    x = x_ref[...]
    hb = _modulated(x, mod_ref, gpre_ref, sub).astype(BF16)
    y = None
    for lo in range(0, D_FF, FFN_CHUNK):
        hi = min(lo + FFN_CHUNK, D_FF)
        g = _dot(hb, win_ref[0, 0, :, lo:hi])
        u = _dot(hb, win_ref[0, 0, :, D_FF + lo:D_FF + hi])
        part = _dot((_silu(g) * u).astype(BF16), wout_ref[0, 0, lo:hi, :])
        y = part if y is None else y + part
    gate = mod_ref[0, 3 * sub + 2:3 * sub + 3, :]
    o_ref[...] = x + (0.5 * gate) * _rms(y, gpost_ref[sub:sub + 1, :])


def _ffn_call(x2d, mod, gpre, gpost, w_in_b, w_out_b, layer, which, seq):
    n, d = x2d.shape
    tiles_per_seq = seq // TOK_TILE
    sub = 2 * which
    pick = lambda w: pl.BlockSpec((1, 1) + w.shape[2:], lambda i: (layer, which, 0, 0),
                                  pipeline_mode=pl.Buffered(1))
    return pl.pallas_call(
        functools.partial(_ffn_kernel, sub=sub),
        grid=(n // TOK_TILE,),
        in_specs=[pl.BlockSpec((TOK_TILE, d), lambda i: (i, 0)),
                  pl.BlockSpec((1, 3 * N_SUBLAYERS, d), lambda i: (i // tiles_per_seq, 0, 0)),
                  _resident(gpre.shape), _resident(gpost.shape),
                  pick(w_in_b), pick(w_out_b)],
        out_specs=pl.BlockSpec((TOK_TILE, d), lambda i: (i, 0)),
        out_shape=jax.ShapeDtypeStruct((n, d), F32),
        compiler_params=_params(1),
        name="ffn",
    )(x2d, mod, gpre, gpost, w_in_b, w_out_b)


IN_OFFSETS = tuple(int(v) for v in np.cumsum((0,) + IN_SIZES[:-1]))
(IN_RQ, IN_RK, IN_RV, IN_RG, IN_PU, IN_SZ, IN_XBC, IN_DT, IN_FQ, IN_FK, IN_FV, IN_FF, IN_GL) = IN_OFFSETS
ALIGNED_W = IN_DT
GATE_W = N_BRANCH * D_MODEL
SIDE_W = GATE_W // 2
assert ALIGNED_W % LANES == 0 and all(off % LANES == 0 for off in IN_OFFSETS if off < ALIGNED_W)


def _regroup_plan():
    shifts = sorted({off % LANES for off in IN_OFFSETS})
    kind_of = {s: k for k, s in enumerate(shifts)}
    scalar_kind, zero_kind = len(shifts), len(shifts) + 1
    place = np.zeros((len(shifts) + 2, 2 * LANES, LANES), np.float32)
    for s, k in kind_of.items():
        place[k, s + np.arange(LANES), np.arange(LANES)] = 1.0
    dt_lane, ff_lane = IN_DT % LANES, IN_FF % LANES
    assert dt_lane + SSM_HEADS <= LANES and ff_lane + FOX_HEADS <= LANES
    for hd in range(SSM_HEADS):
        place[scalar_kind, dt_lane + hd, SC_DT + hd] = 1.0
        place[scalar_kind, dt_lane + hd, SC_ACUM + hd] = 1.0
    for hd in range(FOX_HEADS):
        place[scalar_kind, LANES + ff_lane + hd, SC_CUMF + hd] = 1.0
    lo, hi, kind = [], [], []

    def span(off, width):
        for c in range(off, off + width, LANES):
            lo.append(c // LANES)
            hi.append(c // LANES + (1 if c % LANES else 0))
            kind.append(kind_of[c % LANES])

    span(IN_GL, GATE_W)
    for off in (IN_FQ, IN_FK, IN_FV):
        span(off, FOX_W)
    lo.append(IN_DT // LANES); hi.append(IN_FF // LANES); kind.append(scalar_kind)
    while len(lo) < (GATE_W + SIDE_W) // LANES:
        lo.append(0); hi.append(0); kind.append(zero_kind)
    as_i32 = lambda v: jnp.asarray(np.asarray(v, np.int32))
    need = place[:, LANES:, :].any(axis=2).astype(np.float32)[:, None, :]
    return as_i32(lo), as_i32(hi), as_i32(kind), jnp.asarray(place, BF16), jnp.asarray(need, F32)


def _regroup_kernel(lo_ref, hi_ref, kind_ref, wlo_ref, whi_ref, place_ref, need_ref, o_ref):
    del lo_ref, hi_ref
    kind = kind_ref[pl.program_id(0)]
    second = whi_ref[0]
    second = jnp.where(need_ref[kind] > 0.0, second, jnp.zeros_like(second))
    pair = jnp.concatenate([wlo_ref[0], second], axis=1).astype(BF16)
    o_ref[...] = _dot(pair, place_ref[kind]).astype(BF16)


def _regroup_call(w_in, layer):
    _, d, _ = w_in.shape
    lo, hi, kind, place, need = _regroup_plan()
    n_blocks = lo.shape[0]
    grid_spec = pltpu.PrefetchScalarGridSpec(
        num_scalar_prefetch=3,
        grid=(n_blocks,),
        in_specs=[pl.BlockSpec((1, d, LANES), lambda o, lo, hi, kind: (layer, 0, lo[o])),
                  pl.BlockSpec((1, d, LANES), lambda o, lo, hi, kind: (layer, 0, hi[o])),
                  pl.BlockSpec(place.shape, lambda o, lo, hi, kind: (0, 0, 0)),
                  pl.BlockSpec(need.shape, lambda o, lo, hi, kind: (0, 0, 0))],
        out_specs=pl.BlockSpec((d, LANES), lambda o, lo, hi, kind: (0, o)))
    return pl.pallas_call(
        _regroup_kernel,
        grid_spec=grid_spec,
        out_shape=jax.ShapeDtypeStruct((d, n_blocks * LANES), BF16),
        compiler_params=_params(1),
        name="regroup_w_in",
    )(lo, hi, kind, w_in, w_in, place, need)


SD_FQ = 0
SD_FK = SD_FQ + FOX_W
SD_FV = SD_FK + FOX_W
SD_SC = SD_FV + FOX_W
assert SD_SC + LANES <= SIDE_W
POOL_HALO = 16
CONV_COLS = SSM_XBC // 3
CONV_HALO = 8


def _cumsum_rows(tril_b, s):
    p1 = s.astype(BF16)
    r1 = s - p1.astype(F32)
    p2 = r1.astype(BF16)
    p3 = (r1 - p2.astype(F32)).astype(BF16)
    return _dot(tril_b, p1) + _dot(tril_b, p2) + _dot(tril_b, p3)


def _proj_kernel(x_ref, mod_ref, gpre_ref, wm_ref, ws_ref, cos_ref, sin_ref, poolw_ref, pscale_ref,
                 convw_ref, convb_ref, sbias_ref, alog_ref, tril_ref, selq_ref, selk_ref,
                 ea_sel_ref, dt_sel_ref, dskip_ref, idec_ref, qdec_ref, kdec_ref,
                 yb_ref, fq_ref, fk_ref, fv_ref, ret_ref, ssd_ref,
                 pbuf, cbuf, carry, rq_ref, rk_ref, rv_ref, xbc_ref, scal_ref, scalT_ref,
                 ret_state, ssd_state, *, chunk_decay):
    j = pl.program_id(1)
    tm = x_ref.shape[1]

    @pl.when(j == 0)
    def _():
        pbuf[0:POOL_HALO, :] = jnp.zeros((POOL_HALO, POOL_W), F32)
        cbuf[0:CONV_HALO, :] = jnp.zeros((CONV_HALO, SSM_XBC), F32)
        carry[...] = jnp.zeros_like(carry)
        ret_state[...] = jnp.zeros_like(ret_state)
        ssd_state[...] = jnp.zeros_like(ssd_state)

    @pl.when(j > 0)
    def _():
        pbuf[0:POOL_HALO, :] = pbuf[tm:tm + POOL_HALO, :]
        cbuf[0:CONV_HALO, :] = cbuf[tm:tm + CONV_HALO, :]

    hb = _modulated(x_ref[0], mod_ref, gpre_ref, 1).astype(BF16)
    cos = cos_ref[...]
    sin = sin_ref[...]
    pos = j * tm + lax.broadcasted_iota(jnp.int32, (tm, POOL_GROUP_DIM), 0)

    def rotary(base, ref, scl):
        cos_s, sin_s = (cos, sin) if scl is None else (cos * scl, sin * scl)
        t = _dot(hb, wm_ref[0, :, base:base + RET_W])
        for hd in range(RET_HEADS):
            th = t[:, hd * RET_DK:(hd + 1) * RET_DK]
            r = th * cos_s + pltpu.roll(th, RET_DK // 2, 1) * sin_s
            ref[0, :, hd * RET_DK:(hd + 1) * RET_DK] = r.astype(BF16)

    def pool_mm():
        pu = _dot(hb, wm_ref[0, :, IN_PU:IN_PU + POOL_W])
        pbuf[POOL_HALO:POOL_HALO + tm, :] = pu
        return pu

    def pool_ep(pu, g):
        win = POOL_WINDOWS[g]
        ls = slice(g * POOL_GROUP_DIM, (g + 1) * POOL_GROUP_DIM)
        cur = pu[:, ls]
        acc = pbuf[:, ls]
        span = 1
        while span < win:
            acc = acc + pltpu.roll(acc, span, 0)
            span *= 2
        count = jnp.minimum(pos + 1, win).astype(F32)
        pooled = acc[POOL_HALO:, :] / count - cur
        mixed = _dot(pooled.astype(BF16), poolw_ref[g]) * pscale_ref[:, ls]
        yb_ref[0, :, ls] = mixed.astype(BF16)

    def conv_mm(c0):
        cs = slice(c0, c0 + CONV_COLS)
        xr = _dot(hb, wm_ref[0, :, IN_XBC + c0:IN_XBC + c0 + CONV_COLS])
        cbuf[CONV_HALO:CONV_HALO + tm, cs] = xr
        return xr

    def conv_ep(xr, c0):
        cs = slice(c0, c0 + CONV_COLS)
        conv = xr * convw_ref[SSM_CONV - 1:SSM_CONV, cs] + convb_ref[:, cs]
        for k in range(SSM_CONV - 1):
            off = CONV_HALO - (SSM_CONV - 1) + k
            conv = conv + cbuf[off:off + tm, cs] * convw_ref[k:k + 1, cs]
        xbc_ref[0, :, cs] = _silu(conv).astype(BF16)

    def fox_mm(base, ref, scl, transposed):
        f = _dot(hb, ws_ref[:, base:base + FOX_W])
        if scl is not None:
            f = f * scl
        for hd in range(FOX_HEADS):
            src_l = slice(hd * FOX_HEAD_DIM, (hd + 1) * FOX_HEAD_DIM)
            dst_l = slice(hd * FOX_SLAB, hd * FOX_SLAB + FOX_HEAD_DIM)
            if transposed:
                ref[0, dst_l, :] = f[:, src_l].T.astype(BF16)
            else:
                ref[0, :, dst_l] = f[:, src_l].astype(BF16)

    pu = pool_mm()
    xr0 = conv_mm(0)
    xr1 = conv_mm(CONV_COLS)
    xr2 = conv_mm(2 * CONV_COLS)
    fox_mm(SD_FQ, fq_ref, FOX_HEAD_DIM ** -0.5 * LOG2_E, False)
    for g in range(POOL_GROUPS):
        pool_ep(pu, g)
    fox_mm(SD_FK, fk_ref, None, True)
    conv_ep(xr0, 0)
    fox_mm(SD_FV, fv_ref, None, False)
    conv_ep(xr1, CONV_COLS)
    rv_ref[0] = _dot(hb, wm_ref[0, :, IN_RV:IN_RV + RET_W]).astype(BF16)
    conv_ep(xr2, 2 * CONV_COLS)
    rotary(IN_RQ, rq_ref, None)
    rotary(IN_RK, rk_ref, RET_DK ** -0.5)


    z = _dot(hb, ws_ref[:, SD_SC:SD_SC + LANES]) + sbias_ref[...]
    tail = jnp.log1p(jnp.exp(-jnp.abs(z)))
    softplus = jnp.maximum(z, 0.0) + tail
    log_sig = jnp.minimum(z, 0.0) - tail
    lane = lax.broadcasted_iota(jnp.int32, (tm, LANES), 1)
    a_row = -jnp.exp(alog_ref[...])
    in_acum = (lane >= SC_ACUM) & (lane < SC_CUMF)
    in_cumf = (lane >= SC_CUMF) & (lane < SC_CUMF + FOX_HEADS)
    src = jnp.where(in_acum, softplus * a_row, jnp.where(in_cumf, log_sig, 0.0))
    tril_b = tril_ref[...]
    run = carry[...]
    lane_c = lax.broadcasted_iota(jnp.int32, (SSD_CHUNK, LANES), 1)
    for c0 in range(0, tm, SSD_CHUNK):
        rows = slice(c0, c0 + SSD_CHUNK)
        local = _cumsum_rows(tril_b, src[rows, :])
        total = local + run
        scal_ref[0, rows, :] = jnp.where(lane_c < SC_ACUM, softplus[rows, :],
                                         jnp.where(lane_c < SC_CUMF, local, total))
        run = total[SSD_CHUNK - 1:SSD_CHUNK, :]
    carry[...] = run
    sc_all = scal_ref[0]
    scalT_ref[0] = sc_all.T

    bias = sc_all * LOG2_E
    c1 = bias.astype(BF16)
    r1 = bias - c1.astype(F32)
    c2 = r1.astype(BF16)
    c3 = (r1 - c2.astype(F32)).astype(BF16)
    lane_h = lax.broadcasted_iota(jnp.int32, (tm, FOX_HEAD_DIM), 1)
    ones_q = jnp.where((lane_h >= FOX_PIECES) & (lane_h < 2 * FOX_PIECES), 1.0, 0.0)
    ones_k = jnp.where(lane_h < FOX_PIECES, 1.0, 0.0)
    ones_v = jnp.where(lane_h == 0, 1.0, 0.0).astype(BF16)
    bias_q = _dot(c1, selq_ref[0]) + _dot(c2, selq_ref[1]) + _dot(c3, selq_ref[2])
    bias_k = _dot(c1, selk_ref[0]) + _dot(c2, selk_ref[1]) + _dot(c3, selk_ref[2])
    for hd in range(FOX_HEADS):
        src_l = slice(hd * FOX_HEAD_DIM, (hd + 1) * FOX_HEAD_DIM)
        dst_l = slice(hd * FOX_SLAB + FOX_HEAD_DIM, (hd + 1) * FOX_SLAB)
        fq_ref[0, :, dst_l] = (bias_q[:, src_l] + ones_q).astype(BF16)
        fk_ref[0, dst_l, :] = (ones_k - bias_k[:, src_l]).T.astype(BF16)
        fv_ref[0, :, dst_l] = ones_v

    _ret_body(rq_ref, rk_ref, rv_ref, idec_ref, qdec_ref, kdec_ref, ret_ref, ret_state, chunk_decay)
    _ssd_body(xbc_ref, scal_ref, scalT_ref, ea_sel_ref, dt_sel_ref, dskip_ref, ssd_ref, ssd_state)


def _fox_select_tables():
    selq = np.zeros((FOX_PIECES, LANES, FOX_W), np.float32)
    selk = np.zeros((FOX_PIECES, LANES, FOX_W), np.float32)
    for piece in range(FOX_PIECES):
        for hd in range(FOX_HEADS):
            selq[piece, SC_CUMF + hd, hd * FOX_HEAD_DIM + piece] = 1.0
            selk[piece, SC_CUMF + hd, hd * FOX_HEAD_DIM + FOX_PIECES + piece] = 1.0
    return jnp.asarray(selq, BF16), jnp.asarray(selk, BF16)


def _proj_call(x, mod, gpre, w_in_b, w_side, layer, cos_t, sin_t, poolw_b, pscale, convw, convb, sbias, alog, tril_b, dskip_x):
    bsz, seq, d = x.shape
    tm = TOK_TILE
    selq, selk = _fox_select_tables()
    ea_sel, dt_sel = _ssd_tables()
    idec, qdec, kdec, chunk_decay = _ret_tables()
    tok = lambda w: pl.BlockSpec((1, tm, w), lambda b, j: (b, j, 0))
    bf = lambda w: jax.ShapeDtypeStruct((bsz, seq, w), BF16)
    fox_w = FOX_HEADS * FOX_SLAB
    consts = (poolw_b, pscale, convw, convb, sbias, alog, tril_b, selq, selk,
              ea_sel, dt_sel, dskip_x, idec, qdec, kdec)
    return pl.pallas_call(
        functools.partial(_proj_kernel, chunk_decay=chunk_decay),
        grid=(bsz, seq // tm),
        in_specs=[tok(d),
                  pl.BlockSpec((1, 3 * N_SUBLAYERS, d), lambda b, j: (b, 0, 0)),
                  _resident(gpre.shape),
                  pl.BlockSpec((1, d, ALIGNED_W), lambda b, j: (layer, 0, 0), pipeline_mode=pl.Buffered(1)),
                  pl.BlockSpec((d, SIDE_W), lambda b, j: (0, GATE_W // SIDE_W), pipeline_mode=pl.Buffered(1)),
                  pl.BlockSpec((tm, RET_DK), lambda b, j: (j, 0)),
                  pl.BlockSpec((tm, RET_DK), lambda b, j: (j, 0))] + [_resident(c.shape) for c in consts],
        out_specs=[tok(POOL_W), tok(fox_w), pl.BlockSpec((1, fox_w, tm), lambda b, j: (b, 0, j)), tok(fox_w),
                   tok(RET_W), tok(SSM_INNER)],
        out_shape=[bf(POOL_W), bf(fox_w), jax.ShapeDtypeStruct((bsz, fox_w, seq), BF16), bf(fox_w),
                   bf(RET_W), bf(SSM_INNER)],
        scratch_shapes=[pltpu.VMEM((POOL_HALO + tm, POOL_W), F32),
                        pltpu.VMEM((CONV_HALO + tm, SSM_XBC), F32),
                        pltpu.VMEM((1, LANES), F32),
                        pltpu.VMEM((1, tm, RET_W), BF16), pltpu.VMEM((1, tm, RET_W), BF16),
                        pltpu.VMEM((1, tm, RET_W), BF16), pltpu.VMEM((1, tm, SSM_XBC), BF16),
                        pltpu.VMEM((1, tm, LANES), F32), pltpu.VMEM((1, LANES, tm), F32),
                        pltpu.VMEM((RET_HEADS, RET_DK, RET_DV), F32),
                        pltpu.VMEM((SSM_GROUPS, SSM_STATE, GROUP_W), F32)],
        compiler_params=_params(2),
        name="mixer_proj",
    )(x, mod, gpre, w_in_b, w_side, cos_t, sin_t, *consts)


def _ret_body(q_ref, k_ref, v_ref, idec_ref, qdec_ref, kdec_ref, o_ref, state, chunk_decay):
    tile = q_ref.shape[1]
    for hd in range(RET_HEADS):
        ls = slice(hd * RET_DK, (hd + 1) * RET_DK)
        for c0 in range(0, tile, RET_CHUNK):
            rows = slice(c0, c0 + RET_CHUNK)
            q = q_ref[0, rows, ls]
            k = k_ref[0, rows, ls]
            v = v_ref[0, rows, ls]
            st = state[hd]
            scores = _dot_nt(q, k) * idec_ref[hd]
            o = _dot(scores.astype(BF16), v) + _dot(q, st.astype(BF16)) * qdec_ref[hd]
            kd = (k.astype(F32) * kdec_ref[hd]).astype(BF16)
            state[hd] = chunk_decay[hd] * st + _dot_tn(kd, v)
            dev = o - jnp.mean(o, axis=-1, keepdims=True)
            var = jnp.mean(dev * dev, axis=-1, keepdims=True)
            o_ref[0, rows, ls] = (dev * lax.rsqrt(var + GN_EPS)).astype(BF16)


def _ret_tables():
    log_gamma = np.log1p(-np.exp2(-5.0 - np.arange(RET_HEADS, dtype=np.float64)))
    idx = np.arange(RET_CHUNK, dtype=np.float64)
    rel = idx[:, None] - idx[None, :]
    intra = np.where(rel >= 0, np.exp(log_gamma[:, None, None] * np.maximum(rel, 0.0)), 0.0)
    q_decay = np.exp(log_gamma[:, None] * (idx + 1.0))
    k_decay = np.exp(log_gamma[:, None] * (RET_CHUNK - 1.0 - idx))
    widen = lambda t: np.broadcast_to(t[:, :, None], (RET_HEADS, RET_CHUNK, RET_DK))
    chunk_decay = tuple(float(v) for v in np.exp(log_gamma * RET_CHUNK))
    return (jnp.asarray(intra, F32), jnp.asarray(widen(q_decay), F32),
            jnp.asarray(widen(k_decay), F32), chunk_decay)


HEADS_PER_GROUP = SSM_HEADS // SSM_GROUPS
GROUP_W = HEADS_PER_GROUP * SSM_HEAD_DIM


def _ssd_body(xbc_ref, scal_ref, scalT_ref, ea_sel_ref, dt_sel_ref, dskip_ref, o_ref, state):
    tile = xbc_ref.shape[1]
    cl = SSD_CHUNK
    lane = lax.broadcasted_iota(jnp.int32, (cl, LANES), 1)
    row_i = lax.broadcasted_iota(jnp.int32, (cl, cl), 0)
    col_i = lax.broadcasted_iota(jnp.int32, (cl, cl), 1)
    causal = col_i <= row_i
    low_half = lane < SSM_HEAD_DIM
    ea_sel = ea_sel_ref[...]
    dt_sel = dt_sel_ref[...]
    for c0 in range(0, tile, cl):
        rows = slice(c0, c0 + cl)
        sc = scal_ref[0, rows, :]
        sc2 = sc * LOG2_E
        in_acum = (lane >= SC_ACUM) & (lane < SC_CUMF)
        acum = jnp.where(in_acum, sc, 0.0)
        ea = jnp.exp(acum)
        dec_end = jnp.exp(acum[cl - 1:cl, :] - acum)
        dt = jnp.where(lane < SC_ACUM, sc, 0.0)
        ea_hi = ea.astype(BF16)
        ea_lo = (ea - ea_hi.astype(F32)).astype(BF16)
        ea_x = _dot(ea_hi, ea_sel) + _dot(ea_lo, ea_sel)
        w_x = _dot(dec_end.astype(BF16), ea_sel) * _dot(dt.astype(BF16), dt_sel)
        xs_b = xbc_ref[0, rows, 0:SSM_INNER]
        xs = xs_b.astype(F32)
        xw_b = (xs * w_x).astype(BF16)
        for g in range(SSM_GROUPS):
            gs = slice(g * GROUP_W, (g + 1) * GROUP_W)
            bm = xbc_ref[0, rows, SSM_INNER + g * SSM_STATE:SSM_INNER + (g + 1) * SSM_STATE]
            cm_lo = SSM_INNER + SSM_GROUPS * SSM_STATE + g * SSM_STATE
            cm = xbc_ref[0, rows, cm_lo:cm_lo + SSM_STATE]
            cb = _dot_nt(cm, bm)
            st = state[g]
            y_inter = _dot(cm, st.astype(BF16)) * ea_x[:, gs]
            state[g] = st * ea_x[cl - 1:cl, gs] + _dot_tn(bm, xw_b[:, gs])
            for pair in range(HEADS_PER_GROUP // 2):
                mats = []
                for hh in (2 * pair, 2 * pair + 1):
                    hd = g * HEADS_PER_GROUP + hh
                    a_col = jnp.broadcast_to(sc2[:, SC_ACUM + hd:SC_ACUM + hd + 1], (cl, cl))
                    a_row = scalT_ref[0, SC_ACUM + hd:SC_ACUM + hd + 1, rows] * LOG2_E
                    dt_row = scalT_ref[0, SC_DT + hd:SC_DT + hd + 1, rows]
                    lmat = jnp.exp2(jnp.where(causal, a_col - (a_row - jnp.log2(dt_row)), NEG_BIG))
                    mats.append((lmat * cb).astype(BF16))
                lo = g * GROUP_W + pair * LANES
                x_pair = xs_b[:, lo:lo + LANES]
                zero = jnp.zeros_like(x_pair)
                rhs = jnp.concatenate([jnp.where(low_half, x_pair, zero),
                                       jnp.where(low_half, zero, x_pair)], axis=0)
                y_pair = _dot(jnp.concatenate(mats, axis=1), rhs)
                y_pair = y_pair + y_inter[:, pair * LANES:(pair + 1) * LANES]
                y_pair = y_pair + dskip_ref[:, lo:lo + LANES] * xs[:, lo:lo + LANES]
                o_ref[0, rows, lo:lo + LANES] = y_pair.astype(BF16)


def _ssd_tables():
    ea_sel = np.zeros((LANES, SSM_INNER), np.float32)
    dt_sel = np.zeros((LANES, SSM_INNER), np.float32)
    for hd in range(SSM_HEADS):
        ea_sel[SC_ACUM + hd, hd * SSM_HEAD_DIM:(hd + 1) * SSM_HEAD_DIM] = 1.0
        dt_sel[SC_DT + hd, hd * SSM_HEAD_DIM:(hd + 1) * SSM_HEAD_DIM] = 1.0
    return jnp.asarray(ea_sel, BF16), jnp.asarray(dt_sel, BF16)


FOX_HALVES = 2
FOX_KV = 1024


def _fox_kernel(q_ref, k_ref, v_ref, o_ref, m_s, acc_s, s_a, s_b):
    tq, tk = FOX_TILE, FOX_KV
    th = tq // FOX_HALVES
    n_q = q_ref.shape[1] // tq
    jobs = [(i, t) for i in range(n_q) for t in range(-(-((i + 1) * tq) // tk))]
    bufs = (s_a, s_b)

    def first_row(i, t):
        return max(0, t * tk - i * tq) // th * th

    def scores(s_ref, i, t):
        r0 = first_row(i, t)
        s_ref[r0:, :] = _dot(q_ref[0, i * tq + r0:(i + 1) * tq, :],
                             k_ref[0, :, t * tk:(t + 1) * tk])

    def consume(s_ref, i, t):
        slot = i % 2
        for half in range(first_row(i, t) // th, FOX_HALVES):
            rows = slice(half * th, (half + 1) * th)
            row_lo = i * tq + half * th
            ncols = min(tk, row_lo + th - t * tk)
            s = s_ref[rows, :ncols]
            if t * tk + ncols - 1 > row_lo:
                row_i = row_lo + lax.broadcasted_iota(jnp.int32, (th, ncols), 0)
                col_i = t * tk + lax.broadcasted_iota(jnp.int32, (th, ncols), 1)
                s = jnp.where(col_i <= row_i, s, NEG_BIG)
            m_old = m_s[slot, rows, :]
            m_new = jnp.maximum(m_old, jnp.max(s, axis=-1, keepdims=True))
            p = jnp.concatenate([jnp.exp2(s[:, c0:c0 + LANES] - m_new)
                                 for c0 in range(0, ncols, LANES)], axis=1)
            alpha = jnp.exp2(m_old - m_new)
            pv = _dot(p.astype(BF16), v_ref[0, t * tk:t * tk + ncols, :])
            for c0 in range(0, FOX_SLAB, LANES):
                acc_s[slot, rows, c0:c0 + LANES] = (alpha * acc_s[slot, rows, c0:c0 + LANES]
                                                    + pv[:, c0:c0 + LANES])
            m_s[slot, rows, :] = m_new

    scores(bufs[0], *jobs[0])
    for n, (i, t) in enumerate(jobs):
        if t == 0:
            m_s[i % 2] = jnp.full((tq, LANES), NEG_BIG, F32)
            acc_s[i % 2] = jnp.zeros((tq, FOX_SLAB), F32)
        if n + 1 < len(jobs):
            scores(bufs[(n + 1) % 2], *jobs[n + 1])
        consume(bufs[n % 2], i, t)
        if n + 1 == len(jobs) or jobs[n + 1][0] != i:
            acc = acc_s[i % 2]
            o_ref[0, i * tq:(i + 1) * tq, :] = (acc[:, :FOX_HEAD_DIM]
                                                / acc[:, FOX_HEAD_DIM:FOX_HEAD_DIM + 1]).astype(BF16)


def _fox_call(fq, fk, fv):
    bsz, seq, _ = fq.shape
    tq = FOX_TILE
    return pl.pallas_call(
        _fox_kernel,
        grid=(bsz, FOX_HEADS),
        in_specs=[pl.BlockSpec((1, seq, FOX_SLAB), lambda b, h: (b, 0, h)),
                  pl.BlockSpec((1, FOX_SLAB, seq), lambda b, h: (b, h, 0)),
                  pl.BlockSpec((1, seq, FOX_SLAB), lambda b, h: (b, 0, h))],
        out_specs=pl.BlockSpec((1, seq, FOX_HEAD_DIM), lambda b, h: (b, 0, h)),
        out_shape=jax.ShapeDtypeStruct((bsz, seq, FOX_W), BF16),
        scratch_shapes=[pltpu.VMEM((2, tq, LANES), F32), pltpu.VMEM((2, tq, FOX_SLAB), F32),
                        pltpu.VMEM((tq, FOX_KV), F32), pltpu.VMEM((tq, FOX_KV), F32)],
        compiler_params=_params(2),
        name="fox_attention",
    )(fq, fk, fv)


MG_SZ_BLOCK = 2 * D_MODEL
assert IN_RG % RET_W == 0 and IN_SZ // MG_SZ_BLOCK == (IN_SZ + SSM_INNER - 1) // MG_SZ_BLOCK


def _merge_kernel(x_ref, mod_ref, gpre_ref, gpost_ref, wrg_ref, wsz_ref, wg_ref, ret_ref, yb_ref, ssd_ref, fox_ref,
                  ssmnorm_ref, wro_ref, wpo_ref, wso_ref, wfo_ref, wout_ref, o_ref):
    x = x_ref[...]
    hb = _modulated(x, mod_ref, gpre_ref, 1).astype(BF16)
    rg = _dot(hb, wrg_ref[0])
    y_a = (_silu(rg) * ret_ref[...].astype(F32)).astype(BF16)
    sz = _dot(hb, wsz_ref[0, :, IN_SZ % MG_SZ_BLOCK:IN_SZ % MG_SZ_BLOCK + SSM_INNER])
    y_c = _rms(ssd_ref[...].astype(F32) * _silu(sz), ssmnorm_ref[...]).astype(BF16)
    merged = None
    for br, (y, w_ref) in enumerate(((y_a, wro_ref), (yb_ref[...], wpo_ref),
                                     (y_c, wso_ref), (fox_ref[...], wfo_ref))):
        lo = br * D_MODEL
        gate = _sigmoid(_dot(hb, wg_ref[:, lo:lo + D_MODEL]))
        part = gate * _dot(y, w_ref[...])
        merged = part if merged is None else merged + part
    y_out = _dot(merged.astype(BF16), wout_ref[...])
    o_ref[...] = x + mod_ref[0, 5:6, :] * _rms(y_out, gpost_ref[1:2, :])


def _merge_call(x2d, mod, gpre, gpost, w_in_b, w_gate, layer, ret, yb, ssd, fox, ssmnorm, wro, wpo, wso, wfo, wout, seq):
    n, d = x2d.shape
    tm = TOK_TILE
    tiles_per_seq = seq // tm
    tok = lambda w: pl.BlockSpec((tm, w), lambda i: (i, 0))
    return pl.pallas_call(
        _merge_kernel,
        grid=(n // tm,),
        in_specs=[tok(d),
                  pl.BlockSpec((1, 3 * N_SUBLAYERS, d), lambda i: (i // tiles_per_seq, 0, 0)),
                  _resident(gpre.shape), _resident(gpost.shape),
                  pl.BlockSpec((1, d, RET_W), lambda i: (layer, 0, IN_RG // RET_W), pipeline_mode=pl.Buffered(1)),
                  pl.BlockSpec((1, d, MG_SZ_BLOCK), lambda i: (layer, 0, IN_SZ // MG_SZ_BLOCK),
                               pipeline_mode=pl.Buffered(1)),
                  pl.BlockSpec((d, GATE_W), lambda i: (0, 0), pipeline_mode=pl.Buffered(1)),
                  tok(RET_W), tok(POOL_W), tok(SSM_INNER), tok(FOX_W),
                  _resident(ssmnorm.shape), _resident(wro.shape), _resident(wpo.shape),
                  _resident(wso.shape), _resident(wfo.shape), _resident(wout.shape)],
        out_specs=tok(d),
        out_shape=jax.ShapeDtypeStruct((n, d), F32),
        compiler_params=_params(1),
        name="mixer_merge",
    )(x2d, mod, gpre, gpost, w_in_b, w_in_b, w_gate, ret.reshape(n, -1), yb.reshape(n, -1), ssd.reshape(n, -1),
      fox.reshape(n, -1), ssmnorm, wro, wpo, wso, wfo, wout)


def _rotary_tables(seq):
    half = RET_DK // 2
    inv = ROPE_BASE ** (-jnp.arange(half, dtype=F32) / half)
    ang = jnp.arange(seq, dtype=F32)[:, None] * inv[None, :]
    cos = jnp.cos(ang)
    sin = jnp.sin(ang)
    return jnp.concatenate([cos, cos], axis=-1), jnp.concatenate([-sin, sin], axis=-1)


def _place(vals, lane0):
    return jnp.zeros((1, LANES), F32).at[0, lane0:lane0 + vals.shape[0]].set(vals.astype(F32))


def kernel(x, c, w_ada, b_ada, norm_pre, norm_post, w_ffn_in, w_ffn_out, w_in, b_forget, pool_w, pool_scale, conv_w, conv_b, dt_bias, a_log, d_skip, ssm_norm, w_ret_out, w_pool_out, w_ssm_out, w_fox_out, w_out):
    bsz, seq, d = x.shape
    depth = w_ada.shape[0]
    n = bsz * seq
    mods = _ada_call(c, w_ada, b_ada)
    cos_t, sin_t = _rotary_tables(seq)
    tril_b = jnp.asarray(np.tril(np.ones((SSD_CHUNK, SSD_CHUNK), np.float32)), BF16)
    w_in_b = w_in.astype(BF16)
    w_ffn_in_b = w_ffn_in.astype(BF16)
    w_ffn_out_b = w_ffn_out.astype(BF16)
    for i in range(depth):
        mod = mods[i].reshape(bsz, 3 * N_SUBLAYERS, d)
        gpre, gpost = norm_pre[i], norm_post[i]
        x2d = _ffn_call(x.reshape(n, d), mod, gpre, gpost, w_ffn_in_b, w_ffn_out_b, i, 0, seq)

        w_regrouped = _regroup_call(w_in_b, i)
        sbias = _place(dt_bias[i], SC_DT) + _place(dt_bias[i], SC_ACUM) + _place(b_forget[i], SC_CUMF)
        alog = _place(a_log[i], SC_ACUM)
        dskip_x = jnp.repeat(d_skip[i].astype(F32), SSM_HEAD_DIM)[None, :]

        yb, fq, fk, fv, ret, ssd = _proj_call(
            x2d.reshape(bsz, seq, d), mod, gpre, w_in_b, w_regrouped, i, cos_t, sin_t, pool_w[i].astype(BF16),
            pool_scale[i][None, :], conv_w[i], conv_b[i][None, :], sbias, alog, tril_b, dskip_x)
        fox = _fox_call(fq, fk, fv)
        x2d = _merge_call(x2d, mod, gpre, gpost, w_in_b, w_regrouped, i, ret, yb, ssd, fox, ssm_norm[i][None, :],
                          w_ret_out[i].astype(BF16), w_pool_out[i].astype(BF16),
                          w_ssm_out[i].astype(BF16), w_fox_out[i].astype(BF16),
                          w_out[i].astype(BF16), seq)
        x2d = _ffn_call(x2d, mod, gpre, gpost, w_ffn_in_b, w_ffn_out_b, i, 1, seq)
        x = x2d.reshape(bsz, seq, d)
    return x
```

```python
import functools

import jax
import jax.numpy as jnp
import numpy as np
from jax import lax
from jax.experimental import pallas as pl
from jax.experimental.pallas import tpu as pltpu

F32 = jnp.float32
BF16 = jnp.bfloat16

D_MODEL = 1024
RET_HEADS = 4
RET_DK = 128
RET_DV = 128
RET_W = RET_HEADS * RET_DK
ROPE_BASE = 10000.0
POOL_WINDOWS = (2, 4, 8, 16)
POOL_GROUPS = 4
POOL_GROUP_DIM = 128
POOL_W = POOL_GROUPS * POOL_GROUP_DIM
SSM_HEADS = 16
SSM_HEAD_DIM = 64
SSM_INNER = SSM_HEADS * SSM_HEAD_DIM
SSM_GROUPS = 2
SSM_STATE = 128
SSM_CONV = 4
SSM_XBC = SSM_INNER + 2 * SSM_GROUPS * SSM_STATE
FOX_HEADS = 4
FOX_HEAD_DIM = 128
FOX_W = FOX_HEADS * FOX_HEAD_DIM
N_BRANCH = 4
D_FF = 2816
N_SUBLAYERS = 3
RMS_EPS = 1e-6
GN_EPS = 1e-5
IN_SIZES = (RET_W, RET_W, RET_W, RET_W, POOL_W, SSM_INNER, SSM_XBC, SSM_HEADS,
            FOX_W, FOX_W, FOX_W, FOX_HEADS, N_BRANCH * D_MODEL)

LANES = 128
VMEM_LIMIT = 56 * 1024 * 1024
ADA_TN = 1152
TOK_TILE = 512
SEQ_TILE = 512
MXU_DIM = 256
FFN_CHUNK = 6 * MXU_DIM
RET_CHUNK = 256
SSD_CHUNK = 128
FOX_TILE = 1024
LOG2_E = float(np.log2(np.e))
NEG_BIG = -1e30

SC_DT = 0
SC_ACUM = 16
SC_CUMF = 32
FOX_SLAB = 2 * FOX_HEAD_DIM
FOX_PIECES = 3


def _sigmoid(v):
    return 1.0 / (1.0 + jnp.exp(-v))


def _silu(v):
    return v * (1.0 / (1.0 + jnp.exp2(v * (-LOG2_E))))


def _rms(v, gain):
    return v * lax.rsqrt(jnp.mean(v * v, axis=-1, keepdims=True) + RMS_EPS) * gain


def _modulated(x, mod_ref, gpre_ref, sub):
    shift = mod_ref[0, 3 * sub:3 * sub + 1, :]
    scale = mod_ref[0, 3 * sub + 1:3 * sub + 2, :]
    return _rms(x, gpre_ref[sub:sub + 1, :]) * (1.0 + scale) + shift


def _dot(a, b):
    return jnp.dot(a, b, preferred_element_type=F32)


def _dot_nt(a, b):
    return lax.dot_general(a, b, (((1,), (1,)), ((), ())), preferred_element_type=F32)


def _dot_tn(a, b):
    return lax.dot_general(a, b, (((0,), (0,)), ((), ())), preferred_element_type=F32)


def _resident(shape):
    nd = len(shape)
    return pl.BlockSpec(shape, lambda *_: (0,) * nd, pipeline_mode=pl.Buffered(1))


def _params(n_axes):
    return pltpu.CompilerParams(dimension_semantics=("arbitrary",) * n_axes,
                                vmem_limit_bytes=VMEM_LIMIT)


def _ada_kernel(c_ref, w_ref, b_ref, o_ref):
    sc = _silu(c_ref[...]).astype(BF16)
    o_ref[0] = _dot(sc, w_ref[0].astype(BF16)) + b_ref[0]


def _ada_call(c, w_ada, b_ada):
    depth, d, width = w_ada.shape
    bsz = c.shape[0]
    return pl.pallas_call(
        _ada_kernel,
        grid=(depth, width // ADA_TN),
        in_specs=[pl.BlockSpec((bsz, d), lambda l, n: (0, 0)),
                  pl.BlockSpec((1, d, ADA_TN), lambda l, n: (l, 0, n)),
                  pl.BlockSpec((1, 1, ADA_TN), lambda l, n: (l, 0, n))],
        out_specs=pl.BlockSpec((1, bsz, ADA_TN), lambda l, n: (l, 0, n)),
        out_shape=jax.ShapeDtypeStruct((depth, bsz, width), F32),
        compiler_params=_params(2),
        name="adaln",
    )(c, w_ada, b_ada.reshape(depth, 1, width))


def _ffn_kernel(x_ref, mod_ref, gpre_ref, gpost_ref, win_ref, wout_ref, o_ref, *, sub):
    x = x_ref[...]
    hb = _modulated(x, mod_ref, gpre_ref, sub).astype(BF16)
    y = None
    for lo in range(0, D_FF, FFN_CHUNK):
        hi = min(lo + FFN_CHUNK, D_FF)
        g = _dot(hb, win_ref[0, 0, :, lo:hi])
        u = _dot(hb, win_ref[0, 0, :, D_FF + lo:D_FF + hi])
        part = _dot((_silu(g) * u).astype(BF16), wout_ref[0, 0, lo:hi, :])
        y = part if y is None else y + part
    gate = mod_ref[0, 3 * sub + 2:3 * sub + 3, :]
    o_ref[...] = x + (0.5 * gate) * _rms(y, gpost_ref[sub:sub + 1, :])


def _ffn_call(x2d, mod, gpre, gpost, w_in_b, w_out_b, layer, which, seq):
    n, d = x2d.shape
    tiles_per_seq = seq // TOK_TILE
    sub = 2 * which
    pick = lambda w: pl.BlockSpec((1, 1) + w.shape[2:], lambda i: (layer, which, 0, 0),
                                  pipeline_mode=pl.Buffered(1))
    return pl.pallas_call(
        functools.partial(_ffn_kernel, sub=sub),
        grid=(n // TOK_TILE,),
        in_specs=[pl.BlockSpec((TOK_TILE, d), lambda i: (i, 0)),
                  pl.BlockSpec((1, 3 * N_SUBLAYERS, d), lambda i: (i // tiles_per_seq, 0, 0)),
                  _resident(gpre.shape), _resident(gpost.shape),
                  pick(w_in_b), pick(w_out_b)],
        out_specs=pl.BlockSpec((TOK_TILE, d), lambda i: (i, 0)),
        out_shape=jax.ShapeDtypeStruct((n, d), F32),
        compiler_params=_params(1),
        name="ffn",
    )(x2d, mod, gpre, gpost, w_in_b, w_out_b)


IN_OFFSETS = tuple(int(v) for v in np.cumsum((0,) + IN_SIZES[:-1]))
(IN_RQ, IN_RK, IN_RV, IN_RG, IN_PU, IN_SZ, IN_XBC, IN_DT, IN_FQ, IN_FK, IN_FV, IN_FF, IN_GL) = IN_OFFSETS
ALIGNED_W = IN_DT
GATE_W = N_BRANCH * D_MODEL
SIDE_W = GATE_W // 2
assert ALIGNED_W % LANES == 0 and all(off % LANES == 0 for off in IN_OFFSETS if off < ALIGNED_W)


def _regroup_plan():
    shifts = sorted({off % LANES for off in IN_OFFSETS})
    kind_of = {s: k for k, s in enumerate(shifts)}
    scalar_kind, zero_kind = len(shifts), len(shifts) + 1
    place = np.zeros((len(shifts) + 2, 2 * LANES, LANES), np.float32)
    for s, k in kind_of.items():
        place[k, s + np.arange(LANES), np.arange(LANES)] = 1.0
    dt_lane, ff_lane = IN_DT % LANES, IN_FF % LANES
    assert dt_lane + SSM_HEADS <= LANES and ff_lane + FOX_HEADS <= LANES
    for hd in range(SSM_HEADS):
        place[scalar_kind, dt_lane + hd, SC_DT + hd] = 1.0
        place[scalar_kind, dt_lane + hd, SC_ACUM + hd] = 1.0
    for hd in range(FOX_HEADS):
        place[scalar_kind, LANES + ff_lane + hd, SC_CUMF + hd] = 1.0
    lo, hi, kind = [], [], []

    def span(off, width):
        for c in range(off, off + width, LANES):
            lo.append(c // LANES)
            hi.append(c // LANES + (1 if c % LANES else 0))
            kind.append(kind_of[c % LANES])

    span(IN_GL, GATE_W)
    for off in (IN_FQ, IN_FK, IN_FV):
        span(off, FOX_W)
    lo.append(IN_DT // LANES); hi.append(IN_FF // LANES); kind.append(scalar_kind)
    while len(lo) < (GATE_W + SIDE_W) // LANES:
        lo.append(0); hi.append(0); kind.append(zero_kind)
    as_i32 = lambda v: jnp.asarray(np.asarray(v, np.int32))
    need = place[:, LANES:, :].any(axis=2).astype(np.float32)[:, None, :]
    return as_i32(lo), as_i32(hi), as_i32(kind), jnp.asarray(place, BF16), jnp.asarray(need, F32)


def _regroup_kernel(lo_ref, hi_ref, kind_ref, wlo_ref, whi_ref, place_ref, need_ref, o_ref):
    del lo_ref, hi_ref
    kind = kind_ref[pl.program_id(0)]
    second = whi_ref[0]
    second = jnp.where(need_ref[kind] > 0.0, second, jnp.zeros_like(second))
    pair = jnp.concatenate([wlo_ref[0], second], axis=1).astype(BF16)
    o_ref[...] = _dot(pair, place_ref[kind]).astype(BF16)


def _regroup_call(w_in, layer):
    _, d, _ = w_in.shape
    lo, hi, kind, place, need = _regroup_plan()
    n_blocks = lo.shape[0]
    grid_spec = pltpu.PrefetchScalarGridSpec(
        num_scalar_prefetch=3,
        grid=(n_blocks,),
        in_specs=[pl.BlockSpec((1, d, LANES), lambda o, lo, hi, kind: (layer, 0, lo[o])),
                  pl.BlockSpec((1, d, LANES), lambda o, lo, hi, kind: (layer, 0, hi[o])),
                  pl.BlockSpec(place.shape, lambda o, lo, hi, kind: (0, 0, 0)),
                  pl.BlockSpec(need.shape, lambda o, lo, hi, kind: (0, 0, 0))],
        out_specs=pl.BlockSpec((d, LANES), lambda o, lo, hi, kind: (0, o)))
    return pl.pallas_call(
        _regroup_kernel,
        grid_spec=grid_spec,
        out_shape=jax.ShapeDtypeStruct((d, n_blocks * LANES), BF16),
        compiler_params=_params(1),
        name="regroup_w_in",
    )(lo, hi, kind, w_in, w_in, place, need)


SD_FQ = 0
SD_FK = SD_FQ + FOX_W
SD_FV = SD_FK + FOX_W
SD_SC = SD_FV + FOX_W
assert SD_SC + LANES <= SIDE_W
POOL_HALO = 16
CONV_COLS = SSM_XBC // 3
CONV_HALO = 8


def _cumsum_rows(tril_b, s):
    p1 = s.astype(BF16)
    r1 = s - p1.astype(F32)
    p2 = r1.astype(BF16)
    p3 = (r1 - p2.astype(F32)).astype(BF16)
    return _dot(tril_b, p1) + _dot(tril_b, p2) + _dot(tril_b, p3)


def _proj_kernel(x_ref, mod_ref, gpre_ref, wm_ref, ws_ref, cos_ref, sin_ref, poolw_ref, pscale_ref,
                 convw_ref, convb_ref, sbias_ref, alog_ref, tril_ref, selq_ref, selk_ref,
                 rq_ref, rk_ref, rv_ref, yb_ref, xbc_ref, fq_ref, fk_ref, fv_ref, scal_ref, scalT_ref,
                 pbuf, cbuf, carry):
    j = pl.program_id(1)
    tm = x_ref.shape[1]

    @pl.when(j == 0)
    def _():
        pbuf[0:POOL_HALO, :] = jnp.zeros((POOL_HALO, POOL_W), F32)
        cbuf[0:CONV_HALO, :] = jnp.zeros((CONV_HALO, SSM_XBC), F32)
        carry[...] = jnp.zeros_like(carry)

    @pl.when(j > 0)
    def _():
        pbuf[0:POOL_HALO, :] = pbuf[tm:tm + POOL_HALO, :]
        cbuf[0:CONV_HALO, :] = cbuf[tm:tm + CONV_HALO, :]

    hb = _modulated(x_ref[0], mod_ref, gpre_ref, 1).astype(BF16)
    cos = cos_ref[...]
    sin = sin_ref[...]
    pos = j * tm + lax.broadcasted_iota(jnp.int32, (tm, POOL_GROUP_DIM), 0)

    def rotary(base, ref, scl):
        cos_s, sin_s = (cos, sin) if scl is None else (cos * scl, sin * scl)
        t = _dot(hb, wm_ref[0, :, base:base + RET_W])
        for hd in range(RET_HEADS):
            th = t[:, hd * RET_DK:(hd + 1) * RET_DK]
            r = th * cos_s + pltpu.roll(th, RET_DK // 2, 1) * sin_s
            ref[0, :, hd * RET_DK:(hd + 1) * RET_DK] = r.astype(BF16)

    def pool_mm():
        pu = _dot(hb, wm_ref[0, :, IN_PU:IN_PU + POOL_W])
        pbuf[POOL_HALO:POOL_HALO + tm, :] = pu
        return pu

    def pool_ep(pu, g):
        win = POOL_WINDOWS[g]
        ls = slice(g * POOL_GROUP_DIM, (g + 1) * POOL_GROUP_DIM)
        cur = pu[:, ls]
        acc = pbuf[:, ls]
        span = 1
        while span < win:
            acc = acc + pltpu.roll(acc, span, 0)
            span *= 2
        count = jnp.minimum(pos + 1, win).astype(F32)
        pooled = acc[POOL_HALO:, :] / count - cur
        mixed = _dot(pooled.astype(BF16), poolw_ref[g]) * pscale_ref[:, ls]
        yb_ref[0, :, ls] = mixed.astype(BF16)

    def conv_mm(c0):
        cs = slice(c0, c0 + CONV_COLS)
        xr = _dot(hb, wm_ref[0, :, IN_XBC + c0:IN_XBC + c0 + CONV_COLS])
        cbuf[CONV_HALO:CONV_HALO + tm, cs] = xr
        return xr

    def conv_ep(xr, c0):
        cs = slice(c0, c0 + CONV_COLS)
        conv = xr * convw_ref[SSM_CONV - 1:SSM_CONV, cs] + convb_ref[:, cs]
        for k in range(SSM_CONV - 1):
            off = CONV_HALO - (SSM_CONV - 1) + k
            conv = conv + cbuf[off:off + tm, cs] * convw_ref[k:k + 1, cs]
        xbc_ref[0, :, cs] = _silu(conv).astype(BF16)

    def fox_mm(base, ref, scl, transposed):
        f = _dot(hb, ws_ref[:, base:base + FOX_W])
        if scl is not None:
            f = f * scl
        for hd in range(FOX_HEADS):
            src_l = slice(hd * FOX_HEAD_DIM, (hd + 1) * FOX_HEAD_DIM)
            dst_l = slice(hd * FOX_SLAB, hd * FOX_SLAB + FOX_HEAD_DIM)
            if transposed:
                ref[0, dst_l, :] = f[:, src_l].T.astype(BF16)
            else:
                ref[0, :, dst_l] = f[:, src_l].astype(BF16)

    pu = pool_mm()
    xr0 = conv_mm(0)
    xr1 = conv_mm(CONV_COLS)
    xr2 = conv_mm(2 * CONV_COLS)
    fox_mm(SD_FQ, fq_ref, FOX_HEAD_DIM ** -0.5 * LOG2_E, False)
    for g in range(POOL_GROUPS):
        pool_ep(pu, g)
    fox_mm(SD_FK, fk_ref, None, True)
    conv_ep(xr0, 0)
    fox_mm(SD_FV, fv_ref, None, False)
    conv_ep(xr1, CONV_COLS)
    rv_ref[0] = _dot(hb, wm_ref[0, :, IN_RV:IN_RV + RET_W]).astype(BF16)
    conv_ep(xr2, 2 * CONV_COLS)
    rotary(IN_RQ, rq_ref, None)
    rotary(IN_RK, rk_ref, RET_DK ** -0.5)


    z = _dot(hb, ws_ref[:, SD_SC:SD_SC + LANES]) + sbias_ref[...]
    tail = jnp.log1p(jnp.exp(-jnp.abs(z)))
    softplus = jnp.maximum(z, 0.0) + tail
    log_sig = jnp.minimum(z, 0.0) - tail
    lane = lax.broadcasted_iota(jnp.int32, (tm, LANES), 1)
    a_row = -jnp.exp(alog_ref[...])
    in_acum = (lane >= SC_ACUM) & (lane < SC_CUMF)
    in_cumf = (lane >= SC_CUMF) & (lane < SC_CUMF + FOX_HEADS)
    src = jnp.where(in_acum, softplus * a_row, jnp.where(in_cumf, log_sig, 0.0))
    tril_b = tril_ref[...]
    run = carry[...]
    lane_c = lax.broadcasted_iota(jnp.int32, (SSD_CHUNK, LANES), 1)
    for c0 in range(0, tm, SSD_CHUNK):
        rows = slice(c0, c0 + SSD_CHUNK)
        local = _cumsum_rows(tril_b, src[rows, :])
        total = local + run
        scal_ref[0, rows, :] = jnp.where(lane_c < SC_ACUM, softplus[rows, :],
                                         jnp.where(lane_c < SC_CUMF, local, total))
        run = total[SSD_CHUNK - 1:SSD_CHUNK, :]
    carry[...] = run
    sc_all = scal_ref[0]
    scalT_ref[0] = sc_all.T

    bias = sc_all * LOG2_E
    c1 = bias.astype(BF16)
    r1 = bias - c1.astype(F32)
    c2 = r1.astype(BF16)
    c3 = (r1 - c2.astype(F32)).astype(BF16)
    lane_h = lax.broadcasted_iota(jnp.int32, (tm, FOX_HEAD_DIM), 1)
    ones_q = jnp.where((lane_h >= FOX_PIECES) & (lane_h < 2 * FOX_PIECES), 1.0, 0.0)
    ones_k = jnp.where(lane_h < FOX_PIECES, 1.0, 0.0)
    ones_v = jnp.where(lane_h == 0, 1.0, 0.0).astype(BF16)
    bias_q = _dot(c1, selq_ref[0]) + _dot(c2, selq_ref[1]) + _dot(c3, selq_ref[2])
    bias_k = _dot(c1, selk_ref[0]) + _dot(c2, selk_ref[1]) + _dot(c3, selk_ref[2])
    for hd in range(FOX_HEADS):
        src_l = slice(hd * FOX_HEAD_DIM, (hd + 1) * FOX_HEAD_DIM)
        dst_l = slice(hd * FOX_SLAB + FOX_HEAD_DIM, (hd + 1) * FOX_SLAB)
        fq_ref[0, :, dst_l] = (bias_q[:, src_l] + ones_q).astype(BF16)
        fk_ref[0, dst_l, :] = (ones_k - bias_k[:, src_l]).T.astype(BF16)
        fv_ref[0, :, dst_l] = ones_v


def _fox_select_tables():
    selq = np.zeros((FOX_PIECES, LANES, FOX_W), np.float32)
    selk = np.zeros((FOX_PIECES, LANES, FOX_W), np.float32)
    for piece in range(FOX_PIECES):
        for hd in range(FOX_HEADS):
            selq[piece, SC_CUMF + hd, hd * FOX_HEAD_DIM + piece] = 1.0
            selk[piece, SC_CUMF + hd, hd * FOX_HEAD_DIM + FOX_PIECES + piece] = 1.0
    return jnp.asarray(selq, BF16), jnp.asarray(selk, BF16)


def _proj_call(x, mod, gpre, w_in_b, w_side, layer, cos_t, sin_t, poolw_b, pscale, convw, convb, sbias, alog, tril_b):
    bsz, seq, d = x.shape
    tm = TOK_TILE
    selq, selk = _fox_select_tables()
    tok = lambda w: pl.BlockSpec((1, tm, w), lambda b, j: (b, j, 0))
    bf = lambda w: jax.ShapeDtypeStruct((bsz, seq, w), BF16)
    fox_w = FOX_HEADS * FOX_SLAB
    consts = (poolw_b, pscale, convw, convb, sbias, alog, tril_b, selq, selk)
    return pl.pallas_call(
        _proj_kernel,
        grid=(bsz, seq // tm),
        in_specs=[tok(d),
                  pl.BlockSpec((1, 3 * N_SUBLAYERS, d), lambda b, j: (b, 0, 0)),
                  _resident(gpre.shape),
                  pl.BlockSpec((1, d, ALIGNED_W), lambda b, j: (layer, 0, 0), pipeline_mode=pl.Buffered(1)),
                  pl.BlockSpec((d, SIDE_W), lambda b, j: (0, GATE_W // SIDE_W), pipeline_mode=pl.Buffered(1)),
                  pl.BlockSpec((tm, RET_DK), lambda b, j: (j, 0)),
                  pl.BlockSpec((tm, RET_DK), lambda b, j: (j, 0))] + [_resident(c.shape) for c in consts],
        out_specs=[tok(RET_W), tok(RET_W), tok(RET_W), tok(POOL_W), tok(SSM_XBC),
                   tok(fox_w), pl.BlockSpec((1, fox_w, tm), lambda b, j: (b, 0, j)), tok(fox_w), tok(LANES),
                   pl.BlockSpec((1, LANES, tm), lambda b, j: (b, 0, j))],
        out_shape=[bf(RET_W), bf(RET_W), bf(RET_W), bf(POOL_W), bf(SSM_XBC),
                   bf(fox_w), jax.ShapeDtypeStruct((bsz, fox_w, seq), BF16), bf(fox_w),
                   jax.ShapeDtypeStruct((bsz, seq, LANES), F32),
                   jax.ShapeDtypeStruct((bsz, LANES, seq), F32)],
        scratch_shapes=[pltpu.VMEM((POOL_HALO + tm, POOL_W), F32),
                        pltpu.VMEM((CONV_HALO + tm, SSM_XBC), F32),
                        pltpu.VMEM((1, LANES), F32)],
        compiler_params=_params(2),
        name="mixer_proj",
    )(x, mod, gpre, w_in_b, w_side, cos_t, sin_t, *consts)


def _ret_kernel(q_ref, k_ref, v_ref, idec_ref, qdec_ref, kdec_ref, o_ref, state, *, chunk_decay):
    @pl.when(pl.program_id(1) == 0)
    def _():
        state[...] = jnp.zeros_like(state)

    tile = q_ref.shape[1]
    for hd in range(RET_HEADS):
        ls = slice(hd * RET_DK, (hd + 1) * RET_DK)
        for c0 in range(0, tile, RET_CHUNK):
            rows = slice(c0, c0 + RET_CHUNK)
            q = q_ref[0, rows, ls]
            k = k_ref[0, rows, ls]
            v = v_ref[0, rows, ls]
            st = state[hd]
            scores = _dot_nt(q, k) * idec_ref[hd]
            o = _dot(scores.astype(BF16), v) + _dot(q, st.astype(BF16)) * qdec_ref[hd]
            kd = (k.astype(F32) * kdec_ref[hd]).astype(BF16)
            state[hd] = chunk_decay[hd] * st + _dot_tn(kd, v)
            dev = o - jnp.mean(o, axis=-1, keepdims=True)
            var = jnp.mean(dev * dev, axis=-1, keepdims=True)
            o_ref[0, rows, ls] = (dev * lax.rsqrt(var + GN_EPS)).astype(BF16)


def _ret_tables():
    log_gamma = np.log1p(-np.exp2(-5.0 - np.arange(RET_HEADS, dtype=np.float64)))
    idx = np.arange(RET_CHUNK, dtype=np.float64)
    rel = idx[:, None] - idx[None, :]
    intra = np.where(rel >= 0, np.exp(log_gamma[:, None, None] * np.maximum(rel, 0.0)), 0.0)
    q_decay = np.exp(log_gamma[:, None] * (idx + 1.0))
    k_decay = np.exp(log_gamma[:, None] * (RET_CHUNK - 1.0 - idx))
    widen = lambda t: np.broadcast_to(t[:, :, None], (RET_HEADS, RET_CHUNK, RET_DK))
    chunk_decay = tuple(float(v) for v in np.exp(log_gamma * RET_CHUNK))
    return (jnp.asarray(intra, F32), jnp.asarray(widen(q_decay), F32),
            jnp.asarray(widen(k_decay), F32), chunk_decay)


def _ret_call(rq, rk, rv):
    bsz, seq, w = rq.shape
    idec, qdec, kdec, chunk_decay = _ret_tables()
    tok = pl.BlockSpec((1, SEQ_TILE, w), lambda b, j: (b, j, 0))
    return pl.pallas_call(
        functools.partial(_ret_kernel, chunk_decay=chunk_decay),
        grid=(bsz, seq // SEQ_TILE),
        in_specs=[tok, tok, tok, _resident(idec.shape), _resident(qdec.shape), _resident(kdec.shape)],
        out_specs=tok,
        out_shape=jax.ShapeDtypeStruct((bsz, seq, w), BF16),
        scratch_shapes=[pltpu.VMEM((RET_HEADS, RET_DK, RET_DV), F32)],
        compiler_params=_params(2),
        name="retention",
    )(rq, rk, rv, idec, qdec, kdec)


HEADS_PER_GROUP = SSM_HEADS // SSM_GROUPS
GROUP_W = HEADS_PER_GROUP * SSM_HEAD_DIM


def _ssd_kernel(xbc_ref, scal_ref, scalT_ref, ea_sel_ref, dt_sel_ref, dskip_ref, o_ref, state):
    @pl.when(pl.program_id(1) == 0)
    def _():
        state[...] = jnp.zeros_like(state)

    tile = xbc_ref.shape[1]
    cl = SSD_CHUNK
    lane = lax.broadcasted_iota(jnp.int32, (cl, LANES), 1)
    row_i = lax.broadcasted_iota(jnp.int32, (cl, cl), 0)
    col_i = lax.broadcasted_iota(jnp.int32, (cl, cl), 1)
    causal = col_i <= row_i
    low_half = lane < SSM_HEAD_DIM
    ea_sel = ea_sel_ref[...]
    dt_sel = dt_sel_ref[...]
    for c0 in range(0, tile, cl):
        rows = slice(c0, c0 + cl)
        sc = scal_ref[0, rows, :]
        sc2 = sc * LOG2_E
        in_acum = (lane >= SC_ACUM) & (lane < SC_CUMF)
        acum = jnp.where(in_acum, sc, 0.0)
        ea = jnp.exp(acum)
        dec_end = jnp.exp(acum[cl - 1:cl, :] - acum)
        dt = jnp.where(lane < SC_ACUM, sc, 0.0)
        ea_hi = ea.astype(BF16)
        ea_lo = (ea - ea_hi.astype(F32)).astype(BF16)
        ea_x = _dot(ea_hi, ea_sel) + _dot(ea_lo, ea_sel)
        w_x = _dot(dec_end.astype(BF16), ea_sel) * _dot(dt.astype(BF16), dt_sel)
        xs_b = xbc_ref[0, rows, 0:SSM_INNER]
        xs = xs_b.astype(F32)
        xw_b = (xs * w_x).astype(BF16)
        for g in range(SSM_GROUPS):
            gs = slice(g * GROUP_W, (g + 1) * GROUP_W)
            bm = xbc_ref[0, rows, SSM_INNER + g * SSM_STATE:SSM_INNER + (g + 1) * SSM_STATE]
            cm_lo = SSM_INNER + SSM_GROUPS * SSM_STATE + g * SSM_STATE
            cm = xbc_ref[0, rows, cm_lo:cm_lo + SSM_STATE]
            cb = _dot_nt(cm, bm)
            st = state[g]
            y_inter = _dot(cm, st.astype(BF16)) * ea_x[:, gs]
            state[g] = st * ea_x[cl - 1:cl, gs] + _dot_tn(bm, xw_b[:, gs])
            for pair in range(HEADS_PER_GROUP // 2):
                mats = []
                for hh in (2 * pair, 2 * pair + 1):
                    hd = g * HEADS_PER_GROUP + hh
                    a_col = jnp.broadcast_to(sc2[:, SC_ACUM + hd:SC_ACUM + hd + 1], (cl, cl))
                    a_row = scalT_ref[0, SC_ACUM + hd:SC_ACUM + hd + 1, rows] * LOG2_E
                    dt_row = scalT_ref[0, SC_DT + hd:SC_DT + hd + 1, rows]
                    lmat = jnp.exp2(jnp.where(causal, a_col - (a_row - jnp.log2(dt_row)), NEG_BIG))
                    mats.append((lmat * cb).astype(BF16))
                lo = g * GROUP_W + pair * LANES
                x_pair = xs_b[:, lo:lo + LANES]
                zero = jnp.zeros_like(x_pair)
                rhs = jnp.concatenate([jnp.where(low_half, x_pair, zero),
                                       jnp.where(low_half, zero, x_pair)], axis=0)
                y_pair = _dot(jnp.concatenate(mats, axis=1), rhs)
                y_pair = y_pair + y_inter[:, pair * LANES:(pair + 1) * LANES]
                y_pair = y_pair + dskip_ref[:, lo:lo + LANES] * xs[:, lo:lo + LANES]
                o_ref[0, rows, lo:lo + LANES] = y_pair.astype(BF16)


def _ssd_tables():
    ea_sel = np.zeros((LANES, SSM_INNER), np.float32)
    dt_sel = np.zeros((LANES, SSM_INNER), np.float32)
    for hd in range(SSM_HEADS):
        ea_sel[SC_ACUM + hd, hd * SSM_HEAD_DIM:(hd + 1) * SSM_HEAD_DIM] = 1.0
        dt_sel[SC_DT + hd, hd * SSM_HEAD_DIM:(hd + 1) * SSM_HEAD_DIM] = 1.0
    return jnp.asarray(ea_sel, BF16), jnp.asarray(dt_sel, BF16)


def _ssd_call(xbc, scal, scal_t, dskip_x):
    bsz, seq, _ = xbc.shape
    ea_sel, dt_sel = _ssd_tables()
    return pl.pallas_call(
        _ssd_kernel,
        grid=(bsz, seq // SEQ_TILE),
        in_specs=[pl.BlockSpec((1, SEQ_TILE, SSM_XBC), lambda b, j: (b, j, 0)),
                  pl.BlockSpec((1, SEQ_TILE, LANES), lambda b, j: (b, j, 0)),
                  pl.BlockSpec((1, LANES, SEQ_TILE), lambda b, j: (b, 0, j)),
                  _resident(ea_sel.shape), _resident(dt_sel.shape), _resident(dskip_x.shape)],
        out_specs=pl.BlockSpec((1, SEQ_TILE, SSM_INNER), lambda b, j: (b, j, 0)),
        out_shape=jax.ShapeDtypeStruct((bsz, seq, SSM_INNER), BF16),
        scratch_shapes=[pltpu.VMEM((SSM_GROUPS, SSM_STATE, GROUP_W), F32)],
        compiler_params=_params(2),
        name="ssd",
    )(xbc, scal, scal_t, ea_sel, dt_sel, dskip_x)


FOX_HALVES = 2
FOX_KV = 1024


def _fox_kernel(q_ref, k_ref, v_ref, o_ref, m_s, acc_s, s_a, s_b):
    tq, tk = FOX_TILE, FOX_KV
    th = tq // FOX_HALVES
    n_q = q_ref.shape[1] // tq
    jobs = [(i, t) for i in range(n_q) for t in range(-(-((i + 1) * tq) // tk))]
    bufs = (s_a, s_b)

    def first_row(i, t):
        return max(0, t * tk - i * tq) // th * th

    def scores(s_ref, i, t):
        r0 = first_row(i, t)
        s_ref[r0:, :] = _dot(q_ref[0, i * tq + r0:(i + 1) * tq, :],
                             k_ref[0, :, t * tk:(t + 1) * tk])

    def consume(s_ref, i, t):
        slot = i % 2
        for half in range(first_row(i, t) // th, FOX_HALVES):
            rows = slice(half * th, (half + 1) * th)
            row_lo = i * tq + half * th
            ncols = min(tk, row_lo + th - t * tk)
            s = s_ref[rows, :ncols]
            if t * tk + ncols - 1 > row_lo:
                row_i = row_lo + lax.broadcasted_iota(jnp.int32, (th, ncols), 0)
                col_i = t * tk + lax.broadcasted_iota(jnp.int32, (th, ncols), 1)
                s = jnp.where(col_i <= row_i, s, NEG_BIG)
            m_old = m_s[slot, rows, :]
            m_new = jnp.maximum(m_old, jnp.max(s, axis=-1, keepdims=True))
            p = jnp.concatenate([jnp.exp2(s[:, c0:c0 + LANES] - m_new)
                                 for c0 in range(0, ncols, LANES)], axis=1)
            alpha = jnp.exp2(m_old - m_new)
            pv = _dot(p.astype(BF16), v_ref[0, t * tk:t * tk + ncols, :])
            for c0 in range(0, FOX_SLAB, LANES):
                acc_s[slot, rows, c0:c0 + LANES] = (alpha * acc_s[slot, rows, c0:c0 + LANES]
                                                    + pv[:, c0:c0 + LANES])
            m_s[slot, rows, :] = m_new

    scores(bufs[0], *jobs[0])
    for n, (i, t) in enumerate(jobs):
        if t == 0:
            m_s[i % 2] = jnp.full((tq, LANES), NEG_BIG, F32)
            acc_s[i % 2] = jnp.zeros((tq, FOX_SLAB), F32)
        if n + 1 < len(jobs):
            scores(bufs[(n + 1) % 2], *jobs[n + 1])
        consume(bufs[n % 2], i, t)
        if n + 1 == len(jobs) or jobs[n + 1][0] != i:
            acc = acc_s[i % 2]
            o_ref[0, i * tq:(i + 1) * tq, :] = (acc[:, :FOX_HEAD_DIM]
                                                / acc[:, FOX_HEAD_DIM:FOX_HEAD_DIM + 1]).astype(BF16)


def _fox_call(fq, fk, fv):
    bsz, seq, _ = fq.shape
    tq = FOX_TILE
    return pl.pallas_call(
        _fox_kernel,
        grid=(bsz, FOX_HEADS),
        in_specs=[pl.BlockSpec((1, seq, FOX_SLAB), lambda b, h: (b, 0, h)),
                  pl.BlockSpec((1, FOX_SLAB, seq), lambda b, h: (b, h, 0)),
                  pl.BlockSpec((1, seq, FOX_SLAB), lambda b, h: (b, 0, h))],
        out_specs=pl.BlockSpec((1, seq, FOX_HEAD_DIM), lambda b, h: (b, 0, h)),
        out_shape=jax.ShapeDtypeStruct((bsz, seq, FOX_W), BF16),
        scratch_shapes=[pltpu.VMEM((2, tq, LANES), F32), pltpu.VMEM((2, tq, FOX_SLAB), F32),
                        pltpu.VMEM((tq, FOX_KV), F32), pltpu.VMEM((tq, FOX_KV), F32)],
        compiler_params=_params(2),
        name="fox_attention",
    )(fq, fk, fv)


MG_SZ_BLOCK = 2 * D_MODEL
assert IN_RG % RET_W == 0 and IN_SZ // MG_SZ_BLOCK == (IN_SZ + SSM_INNER - 1) // MG_SZ_BLOCK


def _merge_kernel(x_ref, mod_ref, gpre_ref, gpost_ref, wrg_ref, wsz_ref, wg_ref, ret_ref, yb_ref, ssd_ref, fox_ref,
                  ssmnorm_ref, wro_ref, wpo_ref, wso_ref, wfo_ref, wout_ref, o_ref):
    x = x_ref[...]
    hb = _modulated(x, mod_ref, gpre_ref, 1).astype(BF16)
    rg = _dot(hb, wrg_ref[0])
    y_a = (_silu(rg) * ret_ref[...].astype(F32)).astype(BF16)
    sz = _dot(hb, wsz_ref[0, :, IN_SZ % MG_SZ_BLOCK:IN_SZ % MG_SZ_BLOCK + SSM_INNER])
    y_c = _rms(ssd_ref[...].astype(F32) * _silu(sz), ssmnorm_ref[...]).astype(BF16)
    merged = None
    for br, (y, w_ref) in enumerate(((y_a, wro_ref), (yb_ref[...], wpo_ref),
                                     (y_c, wso_ref), (fox_ref[...], wfo_ref))):
        lo = br * D_MODEL
        gate = _sigmoid(_dot(hb, wg_ref[:, lo:lo + D_MODEL]))
        part = gate * _dot(y, w_ref[...])
        merged = part if merged is None else merged + part
    y_out = _dot(merged.astype(BF16), wout_ref[...])
    o_ref[...] = x + mod_ref[0, 5:6, :] * _rms(y_out, gpost_ref[1:2, :])


def _merge_call(x2d, mod, gpre, gpost, w_in_b, w_gate, layer, ret, yb, ssd, fox, ssmnorm, wro, wpo, wso, wfo, wout, seq):
    n, d = x2d.shape
    tm = TOK_TILE
    tiles_per_seq = seq // tm
    tok = lambda w: pl.BlockSpec((tm, w), lambda i: (i, 0))
    return pl.pallas_call(
        _merge_kernel,
        grid=(n // tm,),
        in_specs=[tok(d),
                  pl.BlockSpec((1, 3 * N_SUBLAYERS, d), lambda i: (i // tiles_per_seq, 0, 0)),
                  _resident(gpre.shape), _resident(gpost.shape),
                  pl.BlockSpec((1, d, RET_W), lambda i: (layer, 0, IN_RG // RET_W), pipeline_mode=pl.Buffered(1)),
                  pl.BlockSpec((1, d, MG_SZ_BLOCK), lambda i: (layer, 0, IN_SZ // MG_SZ_BLOCK),
                               pipeline_mode=pl.Buffered(1)),
                  pl.BlockSpec((d, GATE_W), lambda i: (0, 0), pipeline_mode=pl.Buffered(1)),
                  tok(RET_W), tok(POOL_W), tok(SSM_INNER), tok(FOX_W),
                  _resident(ssmnorm.shape), _resident(wro.shape), _resident(wpo.shape),
                  _resident(wso.shape), _resident(wfo.shape), _resident(wout.shape)],
        out_specs=tok(d),
        out_shape=jax.ShapeDtypeStruct((n, d), F32),
        compiler_params=_params(1),
        name="mixer_merge",
    )(x2d, mod, gpre, gpost, w_in_b, w_in_b, w_gate, ret.reshape(n, -1), yb.reshape(n, -1), ssd.reshape(n, -1),
      fox.reshape(n, -1), ssmnorm, wro, wpo, wso, wfo, wout)


def _rotary_tables(seq):
    half = RET_DK // 2
    inv = ROPE_BASE ** (-jnp.arange(half, dtype=F32) / half)
    ang = jnp.arange(seq, dtype=F32)[:, None] * inv[None, :]
    cos = jnp.cos(ang)
    sin = jnp.sin(ang)
    return jnp.concatenate([cos, cos], axis=-1), jnp.concatenate([-sin, sin], axis=-1)


def _place(vals, lane0):
    return jnp.zeros((1, LANES), F32).at[0, lane0:lane0 + vals.shape[0]].set(vals.astype(F32))


def kernel(x, c, w_ada, b_ada, norm_pre, norm_post, w_ffn_in, w_ffn_out, w_in, b_forget, pool_w, pool_scale, conv_w, conv_b, dt_bias, a_log, d_skip, ssm_norm, w_ret_out, w_pool_out, w_ssm_out, w_fox_out, w_out):
    bsz, seq, d = x.shape
    depth = w_ada.shape[0]
    n = bsz * seq
    mods = _ada_call(c, w_ada, b_ada)
    cos_t, sin_t = _rotary_tables(seq)
    tril_b = jnp.asarray(np.tril(np.ones((SSD_CHUNK, SSD_CHUNK), np.float32)), BF16)
    w_in_b = w_in.astype(BF16)
    w_ffn_in_b = w_ffn_in.astype(BF16)
    w_ffn_out_b = w_ffn_out.astype(BF16)
    for i in range(depth):
        mod = mods[i].reshape(bsz, 3 * N_SUBLAYERS, d)
        gpre, gpost = norm_pre[i], norm_post[i]
        x2d = _ffn_call(x.reshape(n, d), mod, gpre, gpost, w_ffn_in_b, w_ffn_out_b, i, 0, seq)

        w_regrouped = _regroup_call(w_in_b, i)
        sbias = _place(dt_bias[i], SC_DT) + _place(dt_bias[i], SC_ACUM) + _place(b_forget[i], SC_CUMF)
        alog = _place(a_log[i], SC_ACUM)
        dskip_x = jnp.repeat(d_skip[i].astype(F32), SSM_HEAD_DIM)[None, :]

        rq, rk, rv, yb, xbc, fq, fk, fv, scal, scal_t = _proj_call(
            x2d.reshape(bsz, seq, d), mod, gpre, w_in_b, w_regrouped, i, cos_t, sin_t, pool_w[i].astype(BF16),
            pool_scale[i][None, :], conv_w[i], conv_b[i][None, :], sbias, alog, tril_b)
        ret = _ret_call(rq, rk, rv)
        ssd = _ssd_call(xbc, scal, scal_t, dskip_x)
        fox = _fox_call(fq, fk, fv)
        x2d = _merge_call(x2d, mod, gpre, gpost, w_in_b, w_regrouped, i, ret, yb, ssd, fox, ssm_norm[i][None, :],
                          w_ret_out[i].astype(BF16), w_pool_out[i].astype(BF16),
                          w_ssm_out[i].astype(BF16), w_fox_out[i].astype(BF16),
                          w_out[i].astype(BF16), seq)
        x2d = _ffn_call(x2d, mod, gpre, gpost, w_ffn_in_b, w_ffn_out_b, i, 1, seq)
        x = x2d.reshape(bsz, seq, d)
    return x
```

```python
import functools

import jax
import jax.numpy as jnp
import numpy as np
from jax import lax
from jax.experimental import pallas as pl
from jax.experimental.pallas import tpu as pltpu

F32 = jnp.float32
BF16 = jnp.bfloat16

D_MODEL = 1024
RET_HEADS = 4
RET_DK = 128
RET_DV = 128
RET_W = RET_HEADS * RET_DK
ROPE_BASE = 10000.0
POOL_WINDOWS = (2, 4, 8, 16)
POOL_GROUPS = 4
POOL_GROUP_DIM = 128
POOL_W = POOL_GROUPS * POOL_GROUP_DIM
SSM_HEADS = 16
SSM_HEAD_DIM = 64
SSM_INNER = SSM_HEADS * SSM_HEAD_DIM
SSM_GROUPS = 2
SSM_STATE = 128
SSM_CONV = 4
SSM_XBC = SSM_INNER + 2 * SSM_GROUPS * SSM_STATE
FOX_HEADS = 4
FOX_HEAD_DIM = 128
FOX_W = FOX_HEADS * FOX_HEAD_DIM
N_BRANCH = 4
D_FF = 2816
N_SUBLAYERS = 3
RMS_EPS = 1e-6
GN_EPS = 1e-5
IN_SIZES = (RET_W, RET_W, RET_W, RET_W, POOL_W, SSM_INNER, SSM_XBC, SSM_HEADS,
            FOX_W, FOX_W, FOX_W, FOX_HEADS, N_BRANCH * D_MODEL)

LANES = 128
VMEM_LIMIT = 56 * 1024 * 1024
ADA_TN = 1152
TOK_TILE = 512
SEQ_TILE = 512
MXU_DIM = 256
FFN_CHUNK = 6 * MXU_DIM
RET_CHUNK = 256
SSD_CHUNK = 128
FOX_TILE = 1024
LOG2_E = float(np.log2(np.e))
NEG_BIG = -1e30

SC_DT = 0
SC_ACUM = 16
SC_CUMF = 32
FOX_SLAB = 2 * FOX_HEAD_DIM
FOX_PIECES = 3


def _sigmoid(v):
    return 1.0 / (1.0 + jnp.exp(-v))


def _silu(v):
    return v * (1.0 / (1.0 + jnp.exp2(v * (-LOG2_E))))


def _rms(v, gain):
    return v * lax.rsqrt(jnp.mean(v * v, axis=-1, keepdims=True) + RMS_EPS) * gain


def _modulated(x, mod_ref, gpre_ref, sub):
    shift = mod_ref[0, 3 * sub:3 * sub + 1, :]
    scale = mod_ref[0, 3 * sub + 1:3 * sub + 2, :]
    return _rms(x, gpre_ref[sub:sub + 1, :]) * (1.0 + scale) + shift


def _dot(a, b):
    return jnp.dot(a, b, preferred_element_type=F32)


def _dot_nt(a, b):
    return lax.dot_general(a, b, (((1,), (1,)), ((), ())), preferred_element_type=F32)


def _dot_tn(a, b):
    return lax.dot_general(a, b, (((0,), (0,)), ((), ())), preferred_element_type=F32)


def _resident(shape):
    nd = len(shape)
    return pl.BlockSpec(shape, lambda *_: (0,) * nd, pipeline_mode=pl.Buffered(1))


def _params(n_axes):
    return pltpu.CompilerParams(dimension_semantics=("arbitrary",) * n_axes,
                                vmem_limit_bytes=VMEM_LIMIT)


def _ada_kernel(c_ref, w_ref, b_ref, o_ref):
    sc = _silu(c_ref[...]).astype(BF16)
    o_ref[0] = _dot(sc, w_ref[0].astype(BF16)) + b_ref[0]


def _ada_call(c, w_ada, b_ada):
    depth, d, width = w_ada.shape
    bsz = c.shape[0]
    return pl.pallas_call(
        _ada_kernel,
        grid=(depth, width // ADA_TN),
        in_specs=[pl.BlockSpec((bsz, d), lambda l, n: (0, 0)),
                  pl.BlockSpec((1, d, ADA_TN), lambda l, n: (l, 0, n)),
                  pl.BlockSpec((1, 1, ADA_TN), lambda l, n: (l, 0, n))],
        out_specs=pl.BlockSpec((1, bsz, ADA_TN), lambda l, n: (l, 0, n)),
        out_shape=jax.ShapeDtypeStruct((depth, bsz, width), F32),
        compiler_params=_params(2),
        name="adaln",
    )(c, w_ada, b_ada.reshape(depth, 1, width))


def _ffn_kernel(x_ref, mod_ref, gpre_ref, gpost_ref, win_ref, wout_ref, o_ref, *, sub):
    x = x_ref[...]
    hb = _modulated(x, mod_ref, gpre_ref, sub).astype(BF16)
    y = None
    for lo in range(0, D_FF, FFN_CHUNK):
        hi = min(lo + FFN_CHUNK, D_FF)
        g = _dot(hb, win_ref[0, 0, :, lo:hi])
        u = _dot(hb, win_ref[0, 0, :, D_FF + lo:D_FF + hi])
        part = _dot((_silu(g) * u).astype(BF16), wout_ref[0, 0, lo:hi, :])
        y = part if y is None else y + part
    gate = mod_ref[0, 3 * sub + 2:3 * sub + 3, :]
    o_ref[...] = x + (0.5 * gate) * _rms(y, gpost_ref[sub:sub + 1, :])


def _ffn_call(x2d, mod, gpre, gpost, w_in_b, w_out_b, layer, which, seq):
    n, d = x2d.shape
    tiles_per_seq = seq // TOK_TILE
    sub = 2 * which
    pick = lambda w: pl.BlockSpec((1, 1) + w.shape[2:], lambda i: (layer, which, 0, 0),
                                  pipeline_mode=pl.Buffered(1))
    return pl.pallas_call(
        functools.partial(_ffn_kernel, sub=sub),
        grid=(n // TOK_TILE,),
        in_specs=[pl.BlockSpec((TOK_TILE, d), lambda i: (i, 0)),
                  pl.BlockSpec((1, 3 * N_SUBLAYERS, d), lambda i: (i // tiles_per_seq, 0, 0)),
                  _resident(gpre.shape), _resident(gpost.shape),
                  pick(w_in_b), pick(w_out_b)],
        out_specs=pl.BlockSpec((TOK_TILE, d), lambda i: (i, 0)),
        out_shape=jax.ShapeDtypeStruct((n, d), F32),
        compiler_params=_params(1),
        name="ffn",
    )(x2d, mod, gpre, gpost, w_in_b, w_out_b)


IN_OFFSETS = tuple(int(v) for v in np.cumsum((0,) + IN_SIZES[:-1]))
(IN_RQ, IN_RK, IN_RV, IN_RG, IN_PU, IN_SZ, IN_XBC, IN_DT, IN_FQ, IN_FK, IN_FV, IN_FF, IN_GL) = IN_OFFSETS
ALIGNED_W = IN_DT
GATE_W = N_BRANCH * D_MODEL
SIDE_W = GATE_W // 2
assert ALIGNED_W % LANES == 0 and all(off % LANES == 0 for off in IN_OFFSETS if off < ALIGNED_W)


def _regroup_plan():
    shifts = sorted({off % LANES for off in IN_OFFSETS})
    kind_of = {s: k for k, s in enumerate(shifts)}
    scalar_kind, zero_kind = len(shifts), len(shifts) + 1
    place = np.zeros((len(shifts) + 2, 2 * LANES, LANES), np.float32)
    for s, k in kind_of.items():
        place[k, s + np.arange(LANES), np.arange(LANES)] = 1.0
    dt_lane, ff_lane = IN_DT % LANES, IN_FF % LANES
    assert dt_lane + SSM_HEADS <= LANES and ff_lane + FOX_HEADS <= LANES
    for hd in range(SSM_HEADS):
        place[scalar_kind, dt_lane + hd, SC_DT + hd] = 1.0
        place[scalar_kind, dt_lane + hd, SC_ACUM + hd] = 1.0
    for hd in range(FOX_HEADS):
        place[scalar_kind, LANES + ff_lane + hd, SC_CUMF + hd] = 1.0
    lo, hi, kind = [], [], []

    def span(off, width):
        for c in range(off, off + width, LANES):
            lo.append(c // LANES)
            hi.append(c // LANES + (1 if c % LANES else 0))
            kind.append(kind_of[c % LANES])

    span(IN_GL, GATE_W)
    for off in (IN_FQ, IN_FK, IN_FV):
        span(off, FOX_W)
    lo.append(IN_DT // LANES); hi.append(IN_FF // LANES); kind.append(scalar_kind)
    while len(lo) < (GATE_W + SIDE_W) // LANES:
        lo.append(0); hi.append(0); kind.append(zero_kind)
    as_i32 = lambda v: jnp.asarray(np.asarray(v, np.int32))
    need = place[:, LANES:, :].any(axis=2).astype(np.float32)[:, None, :]
    return as_i32(lo), as_i32(hi), as_i32(kind), jnp.asarray(place, BF16), jnp.asarray(need, F32)


def _regroup_kernel(lo_ref, hi_ref, kind_ref, wlo_ref, whi_ref, place_ref, need_ref, o_ref):
    del lo_ref, hi_ref
    kind = kind_ref[pl.program_id(0)]
    second = whi_ref[0]
    second = jnp.where(need_ref[kind] > 0.0, second, jnp.zeros_like(second))
    pair = jnp.concatenate([wlo_ref[0], second], axis=1).astype(BF16)
    o_ref[...] = _dot(pair, place_ref[kind]).astype(BF16)


def _regroup_call(w_in, layer):
    _, d, _ = w_in.shape
    lo, hi, kind, place, need = _regroup_plan()
    n_blocks = lo.shape[0]
    grid_spec = pltpu.PrefetchScalarGridSpec(
        num_scalar_prefetch=3,
        grid=(n_blocks,),
        in_specs=[pl.BlockSpec((1, d, LANES), lambda o, lo, hi, kind: (layer, 0, lo[o])),
                  pl.BlockSpec((1, d, LANES), lambda o, lo, hi, kind: (layer, 0, hi[o])),
                  pl.BlockSpec(place.shape, lambda o, lo, hi, kind: (0, 0, 0)),
                  pl.BlockSpec(need.shape, lambda o, lo, hi, kind: (0, 0, 0))],
        out_specs=pl.BlockSpec((d, LANES), lambda o, lo, hi, kind: (0, o)))
    return pl.pallas_call(
        _regroup_kernel,
        grid_spec=grid_spec,
        out_shape=jax.ShapeDtypeStruct((d, n_blocks * LANES), BF16),
        compiler_params=_params(1),
        name="regroup_w_in",
    )(lo, hi, kind, w_in, w_in, place, need)


SD_FQ = 0
SD_FK = SD_FQ + FOX_W
SD_FV = SD_FK + FOX_W
SD_SC = SD_FV + FOX_W
assert SD_SC + LANES <= SIDE_W
POOL_HALO = 16
CONV_COLS = SSM_XBC // 3
CONV_HALO = 8


def _cumsum_rows(tril_b, s):
    p1 = s.astype(BF16)
    r1 = s - p1.astype(F32)
    p2 = r1.astype(BF16)
    p3 = (r1 - p2.astype(F32)).astype(BF16)
    return _dot(tril_b, p1) + _dot(tril_b, p2) + _dot(tril_b, p3)


def _proj_kernel(x_ref, mod_ref, gpre_ref, wm_ref, ws_ref, cos_ref, sin_ref, poolw_ref, pscale_ref,
                 convw_ref, convb_ref, sbias_ref, alog_ref, tril_ref, selq_ref, selk_ref,
                 rq_ref, rk_ref, rv_ref, yb_ref, xbc_ref, fq_ref, fk_ref, fv_ref, scal_ref, scalT_ref,
                 pbuf, cbuf, carry):
    j = pl.program_id(1)
    tm = x_ref.shape[1]

    @pl.when(j == 0)
    def _():
        pbuf[0:POOL_HALO, :] = jnp.zeros((POOL_HALO, POOL_W), F32)
        cbuf[0:CONV_HALO, :] = jnp.zeros((CONV_HALO, SSM_XBC), F32)
        carry[...] = jnp.zeros_like(carry)

    @pl.when(j > 0)
    def _():
        pbuf[0:POOL_HALO, :] = pbuf[tm:tm + POOL_HALO, :]
        cbuf[0:CONV_HALO, :] = cbuf[tm:tm + CONV_HALO, :]

    hb = _modulated(x_ref[0], mod_ref, gpre_ref, 1).astype(BF16)
    cos = cos_ref[...]
    sin = sin_ref[...]
    pos = j * tm + lax.broadcasted_iota(jnp.int32, (tm, POOL_GROUP_DIM), 0)

    def rotary(base, ref, scl):
        cos_s, sin_s = (cos, sin) if scl is None else (cos * scl, sin * scl)
        t = _dot(hb, wm_ref[0, :, base:base + RET_W])
        for hd in range(RET_HEADS):
            th = t[:, hd * RET_DK:(hd + 1) * RET_DK]
            r = th * cos_s + pltpu.roll(th, RET_DK // 2, 1) * sin_s
            ref[0, :, hd * RET_DK:(hd + 1) * RET_DK] = r.astype(BF16)

    def pool_mm():
        pu = _dot(hb, wm_ref[0, :, IN_PU:IN_PU + POOL_W])
        pbuf[POOL_HALO:POOL_HALO + tm, :] = pu
        return pu

    def pool_ep(pu, g):
        win = POOL_WINDOWS[g]
        ls = slice(g * POOL_GROUP_DIM, (g + 1) * POOL_GROUP_DIM)
        cur = pu[:, ls]
        acc = pbuf[:, ls]
        span = 1
        while span < win:
            acc = acc + pltpu.roll(acc, span, 0)
            span *= 2
        count = jnp.minimum(pos + 1, win).astype(F32)
        pooled = acc[POOL_HALO:, :] / count - cur
        mixed = _dot(pooled.astype(BF16), poolw_ref[g]) * pscale_ref[:, ls]
        yb_ref[0, :, ls] = mixed.astype(BF16)

    def conv_mm(c0):
        cs = slice(c0, c0 + CONV_COLS)
        xr = _dot(hb, wm_ref[0, :, IN_XBC + c0:IN_XBC + c0 + CONV_COLS])
        cbuf[CONV_HALO:CONV_HALO + tm, cs] = xr
        return xr

    def conv_ep(xr, c0):
        cs = slice(c0, c0 + CONV_COLS)
        conv = xr * convw_ref[SSM_CONV - 1:SSM_CONV, cs] + convb_ref[:, cs]
        for k in range(SSM_CONV - 1):
            off = CONV_HALO - (SSM_CONV - 1) + k
            conv = conv + cbuf[off:off + tm, cs] * convw_ref[k:k + 1, cs]
        xbc_ref[0, :, cs] = _silu(conv).astype(BF16)

    def fox_mm(base, ref, scl, transposed):
        f = _dot(hb, ws_ref[:, base:base + FOX_W])
        if scl is not None:
            f = f * scl
        for hd in range(FOX_HEADS):
            src_l = slice(hd * FOX_HEAD_DIM, (hd + 1) * FOX_HEAD_DIM)
            dst_l = slice(hd * FOX_SLAB, hd * FOX_SLAB + FOX_HEAD_DIM)
            if transposed:
                ref[0, dst_l, :] = f[:, src_l].T.astype(BF16)
            else:
                ref[0, :, dst_l] = f[:, src_l].astype(BF16)

    pu = pool_mm()
    xr0 = conv_mm(0)
    xr1 = conv_mm(CONV_COLS)
    xr2 = conv_mm(2 * CONV_COLS)
    fox_mm(SD_FQ, fq_ref, FOX_HEAD_DIM ** -0.5 * LOG2_E, False)
    for g in range(POOL_GROUPS):
        pool_ep(pu, g)
    fox_mm(SD_FK, fk_ref, None, True)
    conv_ep(xr0, 0)
    fox_mm(SD_FV, fv_ref, None, False)
    conv_ep(xr1, CONV_COLS)
    rv_ref[0] = _dot(hb, wm_ref[0, :, IN_RV:IN_RV + RET_W]).astype(BF16)
    conv_ep(xr2, 2 * CONV_COLS)
    rotary(IN_RQ, rq_ref, None)
    rotary(IN_RK, rk_ref, RET_DK ** -0.5)


    z = _dot(hb, ws_ref[:, SD_SC:SD_SC + LANES]) + sbias_ref[...]
    tail = jnp.log1p(jnp.exp(-jnp.abs(z)))
    softplus = jnp.maximum(z, 0.0) + tail
    log_sig = jnp.minimum(z, 0.0) - tail
    lane = lax.broadcasted_iota(jnp.int32, (tm, LANES), 1)
    a_row = -jnp.exp(alog_ref[...])
    in_acum = (lane >= SC_ACUM) & (lane < SC_CUMF)
    in_cumf = (lane >= SC_CUMF) & (lane < SC_CUMF + FOX_HEADS)
    src = jnp.where(in_acum, softplus * a_row, jnp.where(in_cumf, log_sig, 0.0))
    tril_b = tril_ref[...]
    run = carry[...]
    lane_c = lax.broadcasted_iota(jnp.int32, (SSD_CHUNK, LANES), 1)
    for c0 in range(0, tm, SSD_CHUNK):
        rows = slice(c0, c0 + SSD_CHUNK)
        local = _cumsum_rows(tril_b, src[rows, :])
        total = local + run
        scal_ref[0, rows, :] = jnp.where(lane_c < SC_ACUM, softplus[rows, :],
                                         jnp.where(lane_c < SC_CUMF, local, total))
        run = total[SSD_CHUNK - 1:SSD_CHUNK, :]
    carry[...] = run
    sc_all = scal_ref[0]
    scalT_ref[0] = sc_all.T

    bias = sc_all * LOG2_E
    c1 = bias.astype(BF16)
    r1 = bias - c1.astype(F32)
    c2 = r1.astype(BF16)
    c3 = (r1 - c2.astype(F32)).astype(BF16)
    lane_h = lax.broadcasted_iota(jnp.int32, (tm, FOX_HEAD_DIM), 1)
    ones_q = jnp.where((lane_h >= FOX_PIECES) & (lane_h < 2 * FOX_PIECES), 1.0, 0.0)
    ones_k = jnp.where(lane_h < FOX_PIECES, 1.0, 0.0)
    ones_v = jnp.where(lane_h == 0, 1.0, 0.0).astype(BF16)
    bias_q = _dot(c1, selq_ref[0]) + _dot(c2, selq_ref[1]) + _dot(c3, selq_ref[2])
    bias_k = _dot(c1, selk_ref[0]) + _dot(c2, selk_ref[1]) + _dot(c3, selk_ref[2])
    for hd in range(FOX_HEADS):
        src_l = slice(hd * FOX_HEAD_DIM, (hd + 1) * FOX_HEAD_DIM)
        dst_l = slice(hd * FOX_SLAB + FOX_HEAD_DIM, (hd + 1) * FOX_SLAB)
        fq_ref[0, :, dst_l] = (bias_q[:, src_l] + ones_q).astype(BF16)
        fk_ref[0, dst_l, :] = (ones_k - bias_k[:, src_l]).T.astype(BF16)
        fv_ref[0, :, dst_l] = ones_v


def _fox_select_tables():
    selq = np.zeros((FOX_PIECES, LANES, FOX_W), np.float32)
    selk = np.zeros((FOX_PIECES, LANES, FOX_W), np.float32)
    for piece in range(FOX_PIECES):
        for hd in range(FOX_HEADS):
            selq[piece, SC_CUMF + hd, hd * FOX_HEAD_DIM + piece] = 1.0
            selk[piece, SC_CUMF + hd, hd * FOX_HEAD_DIM + FOX_PIECES + piece] = 1.0
    return jnp.asarray(selq, BF16), jnp.asarray(selk, BF16)


def _proj_call(x, mod, gpre, w_in_b, w_side, layer, cos_t, sin_t, poolw_b, pscale, convw, convb, sbias, alog, tril_b):
    bsz, seq, d = x.shape
    tm = TOK_TILE
    selq, selk = _fox_select_tables()
    tok = lambda w: pl.BlockSpec((1, tm, w), lambda b, j: (b, j, 0))
    bf = lambda w: jax.ShapeDtypeStruct((bsz, seq, w), BF16)
    fox_w = FOX_HEADS * FOX_SLAB
    consts = (poolw_b, pscale, convw, convb, sbias, alog, tril_b, selq, selk)
    return pl.pallas_call(
        _proj_kernel,
        grid=(bsz, seq // tm),
        in_specs=[tok(d),
                  pl.BlockSpec((1, 3 * N_SUBLAYERS, d), lambda b, j: (b, 0, 0)),
                  _resident(gpre.shape),
                  pl.BlockSpec((1, d, ALIGNED_W), lambda b, j: (layer, 0, 0), pipeline_mode=pl.Buffered(1)),
                  pl.BlockSpec((d, SIDE_W), lambda b, j: (0, GATE_W // SIDE_W), pipeline_mode=pl.Buffered(1)),
                  pl.BlockSpec((tm, RET_DK), lambda b, j: (j, 0)),
                  pl.BlockSpec((tm, RET_DK), lambda b, j: (j, 0))] + [_resident(c.shape) for c in consts],
        out_specs=[tok(RET_W), tok(RET_W), tok(RET_W), tok(POOL_W), tok(SSM_XBC),
                   tok(fox_w), pl.BlockSpec((1, fox_w, tm), lambda b, j: (b, 0, j)), tok(fox_w), tok(LANES),
                   pl.BlockSpec((1, LANES, tm), lambda b, j: (b, 0, j))],
        out_shape=[bf(RET_W), bf(RET_W), bf(RET_W), bf(POOL_W), bf(SSM_XBC),
                   bf(fox_w), jax.ShapeDtypeStruct((bsz, fox_w, seq), BF16), bf(fox_w),
                   jax.ShapeDtypeStruct((bsz, seq, LANES), F32),
                   jax.ShapeDtypeStruct((bsz, LANES, seq), F32)],
        scratch_shapes=[pltpu.VMEM((POOL_HALO + tm, POOL_W), F32),
                        pltpu.VMEM((CONV_HALO + tm, SSM_XBC), F32),
                        pltpu.VMEM((1, LANES), F32)],
        compiler_params=_params(2),
        name="mixer_proj",
    )(x, mod, gpre, w_in_b, w_side, cos_t, sin_t, *consts)


def _ret_kernel(q_ref, k_ref, v_ref, idec_ref, qdec_ref, kdec_ref, o_ref, state, *, chunk_decay):
    @pl.when(pl.program_id(1) == 0)
    def _():
        state[...] = jnp.zeros_like(state)

    tile = q_ref.shape[1]
    for c0 in range(0, tile, RET_CHUNK):
        rows = slice(c0, c0 + RET_CHUNK)
        for hd in range(RET_HEADS):
            ls = slice(hd * RET_DK, (hd + 1) * RET_DK)
            q = q_ref[0, rows, ls]
            k = k_ref[0, rows, ls]
            v = v_ref[0, rows, ls]
            st = state[hd]
            scores = _dot_nt(q, k) * idec_ref[hd]
            o = _dot(scores.astype(BF16), v) + _dot(q, st.astype(BF16)) * qdec_ref[hd]
            kd = (k.astype(F32) * kdec_ref[hd]).astype(BF16)
            state[hd] = chunk_decay[hd] * st + _dot_tn(kd, v)
            dev = o - jnp.mean(o, axis=-1, keepdims=True)
            var = jnp.mean(dev * dev, axis=-1, keepdims=True)
            o_ref[0, rows, ls] = (dev * lax.rsqrt(var + GN_EPS)).astype(BF16)


def _ret_tables():
    log_gamma = np.log1p(-np.exp2(-5.0 - np.arange(RET_HEADS, dtype=np.float64)))
    idx = np.arange(RET_CHUNK, dtype=np.float64)
    rel = idx[:, None] - idx[None, :]
    intra = np.where(rel >= 0, np.exp(log_gamma[:, None, None] * np.maximum(rel, 0.0)), 0.0)
    q_decay = np.exp(log_gamma[:, None] * (idx + 1.0))
    k_decay = np.exp(log_gamma[:, None] * (RET_CHUNK - 1.0 - idx))
    widen = lambda t: np.broadcast_to(t[:, :, None], (RET_HEADS, RET_CHUNK, RET_DK))
    chunk_decay = tuple(float(v) for v in np.exp(log_gamma * RET_CHUNK))
    return (jnp.asarray(intra, F32), jnp.asarray(widen(q_decay), F32),
            jnp.asarray(widen(k_decay), F32), chunk_decay)


def _ret_call(rq, rk, rv):
    bsz, seq, w = rq.shape
    idec, qdec, kdec, chunk_decay = _ret_tables()
    tok = pl.BlockSpec((1, SEQ_TILE, w), lambda b, j: (b, j, 0))
    return pl.pallas_call(
        functools.partial(_ret_kernel, chunk_decay=chunk_decay),
        grid=(bsz, seq // SEQ_TILE),
        in_specs=[tok, tok, tok, _resident(idec.shape), _resident(qdec.shape), _resident(kdec.shape)],
        out_specs=tok,
        out_shape=jax.ShapeDtypeStruct((bsz, seq, w), BF16),
        scratch_shapes=[pltpu.VMEM((RET_HEADS, RET_DK, RET_DV), F32)],
        compiler_params=_params(2),
        name="retention",
    )(rq, rk, rv, idec, qdec, kdec)


HEADS_PER_GROUP = SSM_HEADS // SSM_GROUPS
GROUP_W = HEADS_PER_GROUP * SSM_HEAD_DIM


def _ssd_kernel(xbc_ref, scal_ref, scalT_ref, ea_sel_ref, dt_sel_ref, dskip_ref, o_ref, state):
    @pl.when(pl.program_id(1) == 0)
    def _():
        state[...] = jnp.zeros_like(state)

    tile = xbc_ref.shape[1]
    cl = SSD_CHUNK
    lane = lax.broadcasted_iota(jnp.int32, (cl, LANES), 1)
    row_i = lax.broadcasted_iota(jnp.int32, (cl, cl), 0)
    col_i = lax.broadcasted_iota(jnp.int32, (cl, cl), 1)
    causal = col_i <= row_i
    low_half = lane < SSM_HEAD_DIM
    ea_sel = ea_sel_ref[...]
    dt_sel = dt_sel_ref[...]
    for c0 in range(0, tile, cl):
        rows = slice(c0, c0 + cl)
        sc = scal_ref[0, rows, :]
        sc2 = sc * LOG2_E
        in_acum = (lane >= SC_ACUM) & (lane < SC_CUMF)
        acum = jnp.where(in_acum, sc, 0.0)
        ea = jnp.exp(acum)
        dec_end = jnp.exp(acum[cl - 1:cl, :] - acum)
        dt = jnp.where(lane < SC_ACUM, sc, 0.0)
        ea_hi = ea.astype(BF16)
        ea_lo = (ea - ea_hi.astype(F32)).astype(BF16)
        ea_x = _dot(ea_hi, ea_sel) + _dot(ea_lo, ea_sel)
        w_x = _dot(dec_end.astype(BF16), ea_sel) * _dot(dt.astype(BF16), dt_sel)
        xs_b = xbc_ref[0, rows, 0:SSM_INNER]
        xs = xs_b.astype(F32)
        xw_b = (xs * w_x).astype(BF16)
        shared = []
        for g in range(SSM_GROUPS):
            gs = slice(g * GROUP_W, (g + 1) * GROUP_W)
            bm = xbc_ref[0, rows, SSM_INNER + g * SSM_STATE:SSM_INNER + (g + 1) * SSM_STATE]
            cm_lo = SSM_INNER + SSM_GROUPS * SSM_STATE + g * SSM_STATE
            cm = xbc_ref[0, rows, cm_lo:cm_lo + SSM_STATE]
            cb = _dot_nt(cm, bm)
            st = state[g]
            y_inter = _dot(cm, st.astype(BF16)) * ea_x[:, gs]
            state[g] = st * ea_x[cl - 1:cl, gs] + _dot_tn(bm, xw_b[:, gs])
            shared.append((cb, y_inter))
        for g in range(SSM_GROUPS):
            cb, y_inter = shared[g]
            for pair in range(HEADS_PER_GROUP // 2):
                mats = []
                for hh in (2 * pair, 2 * pair + 1):
                    hd = g * HEADS_PER_GROUP + hh
                    a_col = jnp.broadcast_to(sc2[:, SC_ACUM + hd:SC_ACUM + hd + 1], (cl, cl))
                    a_row = scalT_ref[0, SC_ACUM + hd:SC_ACUM + hd + 1, rows] * LOG2_E
                    dt_row = scalT_ref[0, SC_DT + hd:SC_DT + hd + 1, rows]
                    lmat = jnp.exp2(jnp.where(causal, a_col - (a_row - jnp.log2(dt_row)), NEG_BIG))
                    mats.append((lmat * cb).astype(BF16))
                lo = g * GROUP_W + pair * LANES
                x_pair = xs_b[:, lo:lo + LANES]
                zero = jnp.zeros_like(x_pair)
                rhs = jnp.concatenate([jnp.where(low_half, x_pair, zero),
                                       jnp.where(low_half, zero, x_pair)], axis=0)
                y_pair = _dot(jnp.concatenate(mats, axis=1), rhs)
                y_pair = y_pair + y_inter[:, pair * LANES:(pair + 1) * LANES]
                y_pair = y_pair + dskip_ref[:, lo:lo + LANES] * xs[:, lo:lo + LANES]
                o_ref[0, rows, lo:lo + LANES] = y_pair.astype(BF16)


def _ssd_tables():
    ea_sel = np.zeros((LANES, SSM_INNER), np.float32)
    dt_sel = np.zeros((LANES, SSM_INNER), np.float32)
    for hd in range(SSM_HEADS):
        ea_sel[SC_ACUM + hd, hd * SSM_HEAD_DIM:(hd + 1) * SSM_HEAD_DIM] = 1.0
        dt_sel[SC_DT + hd, hd * SSM_HEAD_DIM:(hd + 1) * SSM_HEAD_DIM] = 1.0
    return jnp.asarray(ea_sel, BF16), jnp.asarray(dt_sel, BF16)


def _ssd_call(xbc, scal, scal_t, dskip_x):
    bsz, seq, _ = xbc.shape
    ea_sel, dt_sel = _ssd_tables()
    return pl.pallas_call(
        _ssd_kernel,
        grid=(bsz, seq // SEQ_TILE),
        in_specs=[pl.BlockSpec((1, SEQ_TILE, SSM_XBC), lambda b, j: (b, j, 0)),
                  pl.BlockSpec((1, SEQ_TILE, LANES), lambda b, j: (b, j, 0)),
                  pl.BlockSpec((1, LANES, SEQ_TILE), lambda b, j: (b, 0, j)),
                  _resident(ea_sel.shape), _resident(dt_sel.shape), _resident(dskip_x.shape)],
        out_specs=pl.BlockSpec((1, SEQ_TILE, SSM_INNER), lambda b, j: (b, j, 0)),
        out_shape=jax.ShapeDtypeStruct((bsz, seq, SSM_INNER), BF16),
        scratch_shapes=[pltpu.VMEM((SSM_GROUPS, SSM_STATE, GROUP_W), F32)],
        compiler_params=_params(2),
        name="ssd",
    )(xbc, scal, scal_t, ea_sel, dt_sel, dskip_x)


FOX_HALVES = 2
FOX_KV = 1024


def _fox_kernel(q_ref, k_ref, v_ref, o_ref, m_s, acc_s, s_a, s_b):
    tq, tk = FOX_TILE, FOX_KV
    th = tq // FOX_HALVES
    n_q = q_ref.shape[1] // tq
    jobs = [(i, t) for i in range(n_q) for t in range(-(-((i + 1) * tq) // tk))]
    bufs = (s_a, s_b)

    def first_row(i, t):
        return max(0, t * tk - i * tq) // th * th

    def scores(s_ref, i, t):
        r0 = first_row(i, t)
        s_ref[r0:, :] = _dot(q_ref[0, i * tq + r0:(i + 1) * tq, :],
                             k_ref[0, :, t * tk:(t + 1) * tk])

    def consume(s_ref, i, t):
        slot = i % 2
        for half in range(first_row(i, t) // th, FOX_HALVES):
            rows = slice(half * th, (half + 1) * th)
            row_lo = i * tq + half * th
            ncols = min(tk, row_lo + th - t * tk)
            s = s_ref[rows, :ncols]
            if t * tk + ncols - 1 > row_lo:
                row_i = row_lo + lax.broadcasted_iota(jnp.int32, (th, ncols), 0)
                col_i = t * tk + lax.broadcasted_iota(jnp.int32, (th, ncols), 1)
                s = jnp.where(col_i <= row_i, s, NEG_BIG)
            m_old = m_s[slot, rows, :]
            m_new = jnp.maximum(m_old, jnp.max(s, axis=-1, keepdims=True))
            p = jnp.concatenate([jnp.exp2(s[:, c0:c0 + LANES] - m_new)
                                 for c0 in range(0, ncols, LANES)], axis=1)
            alpha = jnp.exp2(m_old - m_new)
            pv = _dot(p.astype(BF16), v_ref[0, t * tk:t * tk + ncols, :])
            for c0 in range(0, FOX_SLAB, LANES):
                acc_s[slot, rows, c0:c0 + LANES] = (alpha * acc_s[slot, rows, c0:c0 + LANES]
                                                    + pv[:, c0:c0 + LANES])
            m_s[slot, rows, :] = m_new

    scores(bufs[0], *jobs[0])
    for n, (i, t) in enumerate(jobs):
        if t == 0:
            m_s[i % 2] = jnp.full((tq, LANES), NEG_BIG, F32)
            acc_s[i % 2] = jnp.zeros((tq, FOX_SLAB), F32)
        if n + 1 < len(jobs):
            scores(bufs[(n + 1) % 2], *jobs[n + 1])
        consume(bufs[n % 2], i, t)
        if n + 1 == len(jobs) or jobs[n + 1][0] != i:
            acc = acc_s[i % 2]
            o_ref[0, i * tq:(i + 1) * tq, :] = (acc[:, :FOX_HEAD_DIM]
                                                / acc[:, FOX_HEAD_DIM:FOX_HEAD_DIM + 1]).astype(BF16)


def _fox_call(fq, fk, fv):
    bsz, seq, _ = fq.shape
    tq = FOX_TILE
    return pl.pallas_call(
        _fox_kernel,
        grid=(bsz, FOX_HEADS),
        in_specs=[pl.BlockSpec((1, seq, FOX_SLAB), lambda b, h: (b, 0, h)),
                  pl.BlockSpec((1, FOX_SLAB, seq), lambda b, h: (b, h, 0)),
                  pl.BlockSpec((1, seq, FOX_SLAB), lambda b, h: (b, 0, h))],
        out_specs=pl.BlockSpec((1, seq, FOX_HEAD_DIM), lambda b, h: (b, 0, h)),
        out_shape=jax.ShapeDtypeStruct((bsz, seq, FOX_W), BF16),
        scratch_shapes=[pltpu.VMEM((2, tq, LANES), F32), pltpu.VMEM((2, tq, FOX_SLAB), F32),
                        pltpu.VMEM((tq, FOX_KV), F32), pltpu.VMEM((tq, FOX_KV), F32)],
        compiler_params=_params(2),
        name="fox_attention",
    )(fq, fk, fv)


MG_SZ_BLOCK = 2 * D_MODEL
assert IN_RG % RET_W == 0 and IN_SZ // MG_SZ_BLOCK == (IN_SZ + SSM_INNER - 1) // MG_SZ_BLOCK


def _merge_kernel(x_ref, mod_ref, gpre_ref, gpost_ref, wrg_ref, wsz_ref, wg_ref, ret_ref, yb_ref, ssd_ref, fox_ref,
                  ssmnorm_ref, wro_ref, wpo_ref, wso_ref, wfo_ref, wout_ref, o_ref):
    x = x_ref[...]
    hb = _modulated(x, mod_ref, gpre_ref, 1).astype(BF16)
    rg = _dot(hb, wrg_ref[0])
    y_a = (_silu(rg) * ret_ref[...].astype(F32)).astype(BF16)
    sz = _dot(hb, wsz_ref[0, :, IN_SZ % MG_SZ_BLOCK:IN_SZ % MG_SZ_BLOCK + SSM_INNER])
    y_c = _rms(ssd_ref[...].astype(F32) * _silu(sz), ssmnorm_ref[...]).astype(BF16)
    merged = None
    for br, (y, w_ref) in enumerate(((y_a, wro_ref), (yb_ref[...], wpo_ref),
                                     (y_c, wso_ref), (fox_ref[...], wfo_ref))):
        lo = br * D_MODEL
        gate = _sigmoid(_dot(hb, wg_ref[:, lo:lo + D_MODEL]))
        part = gate * _dot(y, w_ref[...])
        merged = part if merged is None else merged + part
    y_out = _dot(merged.astype(BF16), wout_ref[...])
    o_ref[...] = x + mod_ref[0, 5:6, :] * _rms(y_out, gpost_ref[1:2, :])


def _merge_call(x2d, mod, gpre, gpost, w_in_b, w_gate, layer, ret, yb, ssd, fox, ssmnorm, wro, wpo, wso, wfo, wout, seq):
    n, d = x2d.shape
    tm = TOK_TILE
    tiles_per_seq = seq // tm
    tok = lambda w: pl.BlockSpec((tm, w), lambda i: (i, 0))
    return pl.pallas_call(
        _merge_kernel,
        grid=(n // tm,),
        in_specs=[tok(d),
                  pl.BlockSpec((1, 3 * N_SUBLAYERS, d), lambda i: (i // tiles_per_seq, 0, 0)),
                  _resident(gpre.shape), _resident(gpost.shape),
                  pl.BlockSpec((1, d, RET_W), lambda i: (layer, 0, IN_RG // RET_W), pipeline_mode=pl.Buffered(1)),
                  pl.BlockSpec((1, d, MG_SZ_BLOCK), lambda i: (layer, 0, IN_SZ // MG_SZ_BLOCK),
                               pipeline_mode=pl.Buffered(1)),
                  pl.BlockSpec((d, GATE_W), lambda i: (0, 0), pipeline_mode=pl.Buffered(1)),
                  tok(RET_W), tok(POOL_W), tok(SSM_INNER), tok(FOX_W),
                  _resident(ssmnorm.shape), _resident(wro.shape), _resident(wpo.shape),
                  _resident(wso.shape), _resident(wfo.shape), _resident(wout.shape)],
        out_specs=tok(d),
        out_shape=jax.ShapeDtypeStruct((n, d), F32),
        compiler_params=_params(1),
        name="mixer_merge",
    )(x2d, mod, gpre, gpost, w_in_b, w_in_b, w_gate, ret.reshape(n, -1), yb.reshape(n, -1), ssd.reshape(n, -1),
      fox.reshape(n, -1), ssmnorm, wro, wpo, wso, wfo, wout)


def _rotary_tables(seq):
    half = RET_DK // 2
    inv = ROPE_BASE ** (-jnp.arange(half, dtype=F32) / half)
    ang = jnp.arange(seq, dtype=F32)[:, None] * inv[None, :]
    cos = jnp.cos(ang)
    sin = jnp.sin(ang)
    return jnp.concatenate([cos, cos], axis=-1), jnp.concatenate([-sin, sin], axis=-1)


def _place(vals, lane0):
    return jnp.zeros((1, LANES), F32).at[0, lane0:lane0 + vals.shape[0]].set(vals.astype(F32))


def kernel(x, c, w_ada, b_ada, norm_pre, norm_post, w_ffn_in, w_ffn_out, w_in, b_forget, pool_w, pool_scale, conv_w, conv_b, dt_bias, a_log, d_skip, ssm_norm, w_ret_out, w_pool_out, w_ssm_out, w_fox_out, w_out):
    bsz, seq, d = x.shape
    depth = w_ada.shape[0]
    n = bsz * seq
    mods = _ada_call(c, w_ada, b_ada)
    cos_t, sin_t = _rotary_tables(seq)
    tril_b = jnp.asarray(np.tril(np.ones((SSD_CHUNK, SSD_CHUNK), np.float32)), BF16)
    w_in_b = w_in.astype(BF16)
    w_ffn_in_b = w_ffn_in.astype(BF16)
    w_ffn_out_b = w_ffn_out.astype(BF16)
    for i in range(depth):
        mod = mods[i].reshape(bsz, 3 * N_SUBLAYERS, d)
        gpre, gpost = norm_pre[i], norm_post[i]
        x2d = _ffn_call(x.reshape(n, d), mod, gpre, gpost, w_ffn_in_b, w_ffn_out_b, i, 0, seq)

        w_regrouped = _regroup_call(w_in_b, i)
        sbias = _place(dt_bias[i], SC_DT) + _place(dt_bias[i], SC_ACUM) + _place(b_forget[i], SC_CUMF)
        alog = _place(a_log[i], SC_ACUM)
        dskip_x = jnp.repeat(d_skip[i].astype(F32), SSM_HEAD_DIM)[None, :]

        rq, rk, rv, yb, xbc, fq, fk, fv, scal, scal_t = _proj_call(
            x2d.reshape(bsz, seq, d), mod, gpre, w_in_b, w_regrouped, i, cos_t, sin_t, pool_w[i].astype(BF16),
            pool_scale[i][None, :], conv_w[i], conv_b[i][None, :], sbias, alog, tril_b)
        ret = _ret_call(rq, rk, rv)
        ssd = _ssd_call(xbc, scal, scal_t, dskip_x)
        fox = _fox_call(fq, fk, fv)
        x2d = _merge_call(x2d, mod, gpre, gpost, w_in_b, w_regrouped, i, ret, yb, ssd, fox, ssm_norm[i][None, :],
                          w_ret_out[i].astype(BF16), w_pool_out[i].astype(BF16),
                          w_ssm_out[i].astype(BF16), w_fox_out[i].astype(BF16),
                          w_out[i].astype(BF16), seq)
        x2d = _ffn_call(x2d, mod, gpre, gpost, w_ffn_in_b, w_ffn_out_b, i, 1, seq)
        x = x2d.reshape(bsz, seq, d)
    return x
```

```python
import functools

import jax
import jax.numpy as jnp
import numpy as np
from jax import lax
from jax.experimental import pallas as pl
from jax.experimental.pallas import tpu as pltpu

F32 = jnp.float32
BF16 = jnp.bfloat16

D_MODEL = 1024
RET_HEADS = 4
RET_DK = 128
RET_DV = 128
RET_W = RET_HEADS * RET_DK
ROPE_BASE = 10000.0
POOL_WINDOWS = (2, 4, 8, 16)
POOL_GROUPS = 4
POOL_GROUP_DIM = 128
POOL_W = POOL_GROUPS * POOL_GROUP_DIM
SSM_HEADS = 16
SSM_HEAD_DIM = 64
SSM_INNER = SSM_HEADS * SSM_HEAD_DIM
SSM_GROUPS = 2
SSM_STATE = 128
SSM_CONV = 4
SSM_XBC = SSM_INNER + 2 * SSM_GROUPS * SSM_STATE
FOX_HEADS = 4
FOX_HEAD_DIM = 128
FOX_W = FOX_HEADS * FOX_HEAD_DIM
N_BRANCH = 4
D_FF = 2816
N_SUBLAYERS = 3
RMS_EPS = 1e-6
GN_EPS = 1e-5
IN_SIZES = (RET_W, RET_W, RET_W, RET_W, POOL_W, SSM_INNER, SSM_XBC, SSM_HEADS,
            FOX_W, FOX_W, FOX_W, FOX_HEADS, N_BRANCH * D_MODEL)

LANES = 128
VMEM_LIMIT = 56 * 1024 * 1024
ADA_TN = 1152
TOK_TILE = 512
RET_TILE = 2048
SSD_TILE = 1024
MXU_DIM = 256
FFN_CHUNK = 6 * MXU_DIM
RET_CHUNK = 256
SSD_CHUNK = 128
FOX_TILE = 1024
LOG2_E = float(np.log2(np.e))
NEG_BIG = -1e30

SC_DT = 0
SC_ACUM = 16
SC_CUMF = 32
FOX_SLAB = 2 * FOX_HEAD_DIM
FOX_PIECES = 3


def _sigmoid(v):
    return 1.0 / (1.0 + jnp.exp(-v))


def _silu(v):
    return v * (1.0 / (1.0 + jnp.exp2(v * (-LOG2_E))))


def _rms(v, gain):
    return v * lax.rsqrt(jnp.mean(v * v, axis=-1, keepdims=True) + RMS_EPS) * gain


def _modulated(x, mod_ref, gpre_ref, sub):
    shift = mod_ref[0, 3 * sub:3 * sub + 1, :]
    scale = mod_ref[0, 3 * sub + 1:3 * sub + 2, :]
    return _rms(x, gpre_ref[sub:sub + 1, :]) * (1.0 + scale) + shift


def _dot(a, b):
    return jnp.dot(a, b, preferred_element_type=F32)


def _dot_nt(a, b):
    return lax.dot_general(a, b, (((1,), (1,)), ((), ())), preferred_element_type=F32)


def _dot_tn(a, b):
    return lax.dot_general(a, b, (((0,), (0,)), ((), ())), preferred_element_type=F32)


def _resident(shape):
    nd = len(shape)
    return pl.BlockSpec(shape, lambda *_: (0,) * nd, pipeline_mode=pl.Buffered(1))


def _params(n_axes):
    return pltpu.CompilerParams(dimension_semantics=("arbitrary",) * n_axes,
                                vmem_limit_bytes=VMEM_LIMIT)


def _ada_kernel(c_ref, w_ref, b_ref, o_ref):
    sc = _silu(c_ref[...]).astype(BF16)
    o_ref[0] = _dot(sc, w_ref[0].astype(BF16)) + b_ref[0]


def _ada_call(c, w_ada, b_ada):
    depth, d, width = w_ada.shape
    bsz = c.shape[0]
    return pl.pallas_call(
        _ada_kernel,
        grid=(depth, width // ADA_TN),
        in_specs=[pl.BlockSpec((bsz, d), lambda l, n: (0, 0)),
                  pl.BlockSpec((1, d, ADA_TN), lambda l, n: (l, 0, n)),
                  pl.BlockSpec((1, 1, ADA_TN), lambda l, n: (l, 0, n))],
        out_specs=pl.BlockSpec((1, bsz, ADA_TN), lambda l, n: (l, 0, n)),
        out_shape=jax.ShapeDtypeStruct((depth, bsz, width), F32),
        compiler_params=_params(2),
        name="adaln",
    )(c, w_ada, b_ada.reshape(depth, 1, width))


def _ffn_kernel(x_ref, mod_ref, gpre_ref, gpost_ref, win_ref, wout_ref, o_ref, *, sub):
    x = x_ref[...]
    hb = _modulated(x, mod_ref, gpre_ref, sub).astype(BF16)
    y = None
    for lo in range(0, D_FF, FFN_CHUNK):
        hi = min(lo + FFN_CHUNK, D_FF)
        g = _dot(hb, win_ref[0, 0, :, lo:hi])
        u = _dot(hb, win_ref[0, 0, :, D_FF + lo:D_FF + hi])
        part = _dot((_silu(g) * u).astype(BF16), wout_ref[0, 0, lo:hi, :])
        y = part if y is None else y + part
    gate = mod_ref[0, 3 * sub + 2:3 * sub + 3, :]
    o_ref[...] = x + (0.5 * gate) * _rms(y, gpost_ref[sub:sub + 1, :])


def _ffn_call(x2d, mod, gpre, gpost, w_in_b, w_out_b, layer, which, seq):
    n, d = x2d.shape
    tiles_per_seq = seq // TOK_TILE
    sub = 2 * which
    pick = lambda w: pl.BlockSpec((1, 1) + w.shape[2:], lambda i: (layer, which, 0, 0),
                                  pipeline_mode=pl.Buffered(1))
    return pl.pallas_call(
        functools.partial(_ffn_kernel, sub=sub),
        grid=(n // TOK_TILE,),
        in_specs=[pl.BlockSpec((TOK_TILE, d), lambda i: (i, 0)),
                  pl.BlockSpec((1, 3 * N_SUBLAYERS, d), lambda i: (i // tiles_per_seq, 0, 0)),
                  _resident(gpre.shape), _resident(gpost.shape),
                  pick(w_in_b), pick(w_out_b)],
        out_specs=pl.BlockSpec((TOK_TILE, d), lambda i: (i, 0)),
        out_shape=jax.ShapeDtypeStruct((n, d), F32),
        compiler_params=_params(1),
        name="ffn",
    )(x2d, mod, gpre, gpost, w_in_b, w_out_b)


IN_OFFSETS = tuple(int(v) for v in np.cumsum((0,) + IN_SIZES[:-1]))
(IN_RQ, IN_RK, IN_RV, IN_RG, IN_PU, IN_SZ, IN_XBC, IN_DT, IN_FQ, IN_FK, IN_FV, IN_FF, IN_GL) = IN_OFFSETS
ALIGNED_W = IN_DT
GATE_W = N_BRANCH * D_MODEL
SIDE_W = GATE_W // 2
assert ALIGNED_W % LANES == 0 and all(off % LANES == 0 for off in IN_OFFSETS if off < ALIGNED_W)


def _regroup_plan():
    shifts = sorted({off % LANES for off in IN_OFFSETS})
    kind_of = {s: k for k, s in enumerate(shifts)}
    scalar_kind, zero_kind = len(shifts), len(shifts) + 1
    place = np.zeros((len(shifts) + 2, 2 * LANES, LANES), np.float32)
    for s, k in kind_of.items():
        place[k, s + np.arange(LANES), np.arange(LANES)] = 1.0
    dt_lane, ff_lane = IN_DT % LANES, IN_FF % LANES
    assert dt_lane + SSM_HEADS <= LANES and ff_lane + FOX_HEADS <= LANES
    for hd in range(SSM_HEADS):
        place[scalar_kind, dt_lane + hd, SC_DT + hd] = 1.0
        place[scalar_kind, dt_lane + hd, SC_ACUM + hd] = 1.0
    for hd in range(FOX_HEADS):
        place[scalar_kind, LANES + ff_lane + hd, SC_CUMF + hd] = 1.0
    lo, hi, kind = [], [], []

    def span(off, width):
        for c in range(off, off + width, LANES):
            lo.append(c // LANES)
            hi.append(c // LANES + (1 if c % LANES else 0))
            kind.append(kind_of[c % LANES])

    span(IN_GL, GATE_W)
    for off in (IN_FQ, IN_FK, IN_FV):
        span(off, FOX_W)
    lo.append(IN_DT // LANES); hi.append(IN_FF // LANES); kind.append(scalar_kind)
    while len(lo) < (GATE_W + SIDE_W) // LANES:
        lo.append(0); hi.append(0); kind.append(zero_kind)
    as_i32 = lambda v: jnp.asarray(np.asarray(v, np.int32))
    need = place[:, LANES:, :].any(axis=2).astype(np.float32)[:, None, :]
    return as_i32(lo), as_i32(hi), as_i32(kind), jnp.asarray(place, BF16), jnp.asarray(need, F32)


def _regroup_kernel(lo_ref, hi_ref, kind_ref, wlo_ref, whi_ref, place_ref, need_ref, o_ref):
    del lo_ref, hi_ref
    kind = kind_ref[pl.program_id(0)]
    second = whi_ref[0]
    second = jnp.where(need_ref[kind] > 0.0, second, jnp.zeros_like(second))
    pair = jnp.concatenate([wlo_ref[0], second], axis=1).astype(BF16)
    o_ref[...] = _dot(pair, place_ref[kind]).astype(BF16)


def _regroup_call(w_in, layer):
    _, d, _ = w_in.shape
    lo, hi, kind, place, need = _regroup_plan()
    n_blocks = lo.shape[0]
    grid_spec = pltpu.PrefetchScalarGridSpec(
        num_scalar_prefetch=3,
        grid=(n_blocks,),
        in_specs=[pl.BlockSpec((1, d, LANES), lambda o, lo, hi, kind: (layer, 0, lo[o])),
                  pl.BlockSpec((1, d, LANES), lambda o, lo, hi, kind: (layer, 0, hi[o])),
                  pl.BlockSpec(place.shape, lambda o, lo, hi, kind: (0, 0, 0)),
                  pl.BlockSpec(need.shape, lambda o, lo, hi, kind: (0, 0, 0))],
        out_specs=pl.BlockSpec((d, LANES), lambda o, lo, hi, kind: (0, o)))
    return pl.pallas_call(
        _regroup_kernel,
        grid_spec=grid_spec,
        out_shape=jax.ShapeDtypeStruct((d, n_blocks * LANES), BF16),
        compiler_params=_params(1),
        name="regroup_w_in",
    )(lo, hi, kind, w_in, w_in, place, need)


SD_FQ = 0
SD_FK = SD_FQ + FOX_W
SD_FV = SD_FK + FOX_W
SD_SC = SD_FV + FOX_W
assert SD_SC + LANES <= SIDE_W
POOL_HALO = 16
CONV_COLS = SSM_XBC // 3
CONV_HALO = 8


def _cumsum_rows(tril_b, s):
    p1 = s.astype(BF16)
    r1 = s - p1.astype(F32)
    p2 = r1.astype(BF16)
    p3 = (r1 - p2.astype(F32)).astype(BF16)
    return _dot(tril_b, p1) + _dot(tril_b, p2) + _dot(tril_b, p3)


def _proj_kernel(x_ref, mod_ref, gpre_ref, wm_ref, ws_ref, cos_ref, sin_ref, poolw_ref, pscale_ref,
                 convw_ref, convb_ref, sbias_ref, alog_ref, tril_ref, selq_ref, selk_ref,
                 rq_ref, rk_ref, rv_ref, yb_ref, xbc_ref, fq_ref, fk_ref, fv_ref, scal_ref, scalT_ref,
                 pbuf, cbuf, carry):
    j = pl.program_id(1)
    tm = x_ref.shape[1]

    @pl.when(j == 0)
    def _():
        pbuf[0:POOL_HALO, :] = jnp.zeros((POOL_HALO, POOL_W), F32)
        cbuf[0:CONV_HALO, :] = jnp.zeros((CONV_HALO, SSM_XBC), F32)
        carry[...] = jnp.zeros_like(carry)

    @pl.when(j > 0)
    def _():
        pbuf[0:POOL_HALO, :] = pbuf[tm:tm + POOL_HALO, :]
        cbuf[0:CONV_HALO, :] = cbuf[tm:tm + CONV_HALO, :]

    hb = _modulated(x_ref[0], mod_ref, gpre_ref, 1).astype(BF16)
    cos = cos_ref[...]
    sin = sin_ref[...]
    pos = j * tm + lax.broadcasted_iota(jnp.int32, (tm, POOL_GROUP_DIM), 0)

    def rotary(base, ref, scl):
        cos_s, sin_s = (cos, sin) if scl is None else (cos * scl, sin * scl)
        t = _dot(hb, wm_ref[0, :, base:base + RET_W])
        for hd in range(RET_HEADS):
            th = t[:, hd * RET_DK:(hd + 1) * RET_DK]
            r = th * cos_s + pltpu.roll(th, RET_DK // 2, 1) * sin_s
            ref[0, :, hd * RET_DK:(hd + 1) * RET_DK] = r.astype(BF16)

    def pool_mm():
        pu = _dot(hb, wm_ref[0, :, IN_PU:IN_PU + POOL_W])
        pbuf[POOL_HALO:POOL_HALO + tm, :] = pu
        return pu

    def pool_ep(pu, g):
        win = POOL_WINDOWS[g]
        ls = slice(g * POOL_GROUP_DIM, (g + 1) * POOL_GROUP_DIM)
        cur = pu[:, ls]
        acc = pbuf[:, ls]
        span = 1
        while span < win:
            acc = acc + pltpu.roll(acc, span, 0)
            span *= 2
        count = jnp.minimum(pos + 1, win).astype(F32)
        pooled = acc[POOL_HALO:, :] / count - cur
        mixed = _dot(pooled.astype(BF16), poolw_ref[g]) * pscale_ref[:, ls]
        yb_ref[0, :, ls] = mixed.astype(BF16)

    def conv_mm(c0):
        cs = slice(c0, c0 + CONV_COLS)
        xr = _dot(hb, wm_ref[0, :, IN_XBC + c0:IN_XBC + c0 + CONV_COLS])
        cbuf[CONV_HALO:CONV_HALO + tm, cs] = xr
        return xr

    def conv_ep(xr, c0):
        cs = slice(c0, c0 + CONV_COLS)
        conv = xr * convw_ref[SSM_CONV - 1:SSM_CONV, cs] + convb_ref[:, cs]
        for k in range(SSM_CONV - 1):
            off = CONV_HALO - (SSM_CONV - 1) + k
            conv = conv + cbuf[off:off + tm, cs] * convw_ref[k:k + 1, cs]
        xbc_ref[0, :, cs] = _silu(conv).astype(BF16)

    def fox_mm(base, ref, scl, transposed):
        f = _dot(hb, ws_ref[:, base:base + FOX_W])
        if scl is not None:
            f = f * scl
        for hd in range(FOX_HEADS):
            src_l = slice(hd * FOX_HEAD_DIM, (hd + 1) * FOX_HEAD_DIM)
            dst_l = slice(hd * FOX_SLAB, hd * FOX_SLAB + FOX_HEAD_DIM)
            if transposed:
                ref[0, dst_l, :] = f[:, src_l].T.astype(BF16)
            else:
                ref[0, :, dst_l] = f[:, src_l].astype(BF16)

    pu = pool_mm()
    xr0 = conv_mm(0)
    xr1 = conv_mm(CONV_COLS)
    xr2 = conv_mm(2 * CONV_COLS)
    fox_mm(SD_FQ, fq_ref, FOX_HEAD_DIM ** -0.5 * LOG2_E, False)
    for g in range(POOL_GROUPS):
        pool_ep(pu, g)
    fox_mm(SD_FK, fk_ref, None, True)
    conv_ep(xr0, 0)
    fox_mm(SD_FV, fv_ref, None, False)
    conv_ep(xr1, CONV_COLS)
    rv_ref[0] = _dot(hb, wm_ref[0, :, IN_RV:IN_RV + RET_W]).astype(BF16)
    conv_ep(xr2, 2 * CONV_COLS)
    rotary(IN_RQ, rq_ref, None)
    rotary(IN_RK, rk_ref, RET_DK ** -0.5)


    z = _dot(hb, ws_ref[:, SD_SC:SD_SC + LANES]) + sbias_ref[...]
    tail = jnp.log1p(jnp.exp(-jnp.abs(z)))
    softplus = jnp.maximum(z, 0.0) + tail
    log_sig = jnp.minimum(z, 0.0) - tail
    lane = lax.broadcasted_iota(jnp.int32, (tm, LANES), 1)
    a_row = -jnp.exp(alog_ref[...])
    in_acum = (lane >= SC_ACUM) & (lane < SC_CUMF)
    in_cumf = (lane >= SC_CUMF) & (lane < SC_CUMF + FOX_HEADS)
    src = jnp.where(in_acum, softplus * a_row, jnp.where(in_cumf, log_sig, 0.0))
    tril_b = tril_ref[...]
    run = carry[...]
    lane_c = lax.broadcasted_iota(jnp.int32, (SSD_CHUNK, LANES), 1)
    for c0 in range(0, tm, SSD_CHUNK):
        rows = slice(c0, c0 + SSD_CHUNK)
        local = _cumsum_rows(tril_b, src[rows, :])
        total = local + run
        scal_ref[0, rows, :] = jnp.where(lane_c < SC_ACUM, softplus[rows, :],
                                         jnp.where(lane_c < SC_CUMF, local, total))
        run = total[SSD_CHUNK - 1:SSD_CHUNK, :]
    carry[...] = run
    sc_all = scal_ref[0]
    scalT_ref[0] = sc_all.T

    bias = sc_all * LOG2_E
    c1 = bias.astype(BF16)
    r1 = bias - c1.astype(F32)
    c2 = r1.astype(BF16)
    c3 = (r1 - c2.astype(F32)).astype(BF16)
    lane_h = lax.broadcasted_iota(jnp.int32, (tm, FOX_HEAD_DIM), 1)
    ones_q = jnp.where((lane_h >= FOX_PIECES) & (lane_h < 2 * FOX_PIECES), 1.0, 0.0)
    ones_k = jnp.where(lane_h < FOX_PIECES, 1.0, 0.0)
    ones_v = jnp.where(lane_h == 0, 1.0, 0.0).astype(BF16)
    bias_q = _dot(c1, selq_ref[0]) + _dot(c2, selq_ref[1]) + _dot(c3, selq_ref[2])
    bias_k = _dot(c1, selk_ref[0]) + _dot(c2, selk_ref[1]) + _dot(c3, selk_ref[2])
    for hd in range(FOX_HEADS):
        src_l = slice(hd * FOX_HEAD_DIM, (hd + 1) * FOX_HEAD_DIM)
        dst_l = slice(hd * FOX_SLAB + FOX_HEAD_DIM, (hd + 1) * FOX_SLAB)
        fq_ref[0, :, dst_l] = (bias_q[:, src_l] + ones_q).astype(BF16)
        fk_ref[0, dst_l, :] = (ones_k - bias_k[:, src_l]).T.astype(BF16)
        fv_ref[0, :, dst_l] = ones_v


def _fox_select_tables():
    selq = np.zeros((FOX_PIECES, LANES, FOX_W), np.float32)
    selk = np.zeros((FOX_PIECES, LANES, FOX_W), np.float32)
    for piece in range(FOX_PIECES):
        for hd in range(FOX_HEADS):
            selq[piece, SC_CUMF + hd, hd * FOX_HEAD_DIM + piece] = 1.0
            selk[piece, SC_CUMF + hd, hd * FOX_HEAD_DIM + FOX_PIECES + piece] = 1.0
    return jnp.asarray(selq, BF16), jnp.asarray(selk, BF16)


def _proj_call(x, mod, gpre, w_in_b, w_side, layer, cos_t, sin_t, poolw_b, pscale, convw, convb, sbias, alog, tril_b):
    bsz, seq, d = x.shape
    tm = TOK_TILE
    selq, selk = _fox_select_tables()
    tok = lambda w: pl.BlockSpec((1, tm, w), lambda b, j: (b, j, 0))
    bf = lambda w: jax.ShapeDtypeStruct((bsz, seq, w), BF16)
    fox_w = FOX_HEADS * FOX_SLAB
    consts = (poolw_b, pscale, convw, convb, sbias, alog, tril_b, selq, selk)
    return pl.pallas_call(
        _proj_kernel,
        grid=(bsz, seq // tm),
        in_specs=[tok(d),
                  pl.BlockSpec((1, 3 * N_SUBLAYERS, d), lambda b, j: (b, 0, 0)),
                  _resident(gpre.shape),
                  pl.BlockSpec((1, d, ALIGNED_W), lambda b, j: (layer, 0, 0), pipeline_mode=pl.Buffered(1)),
                  pl.BlockSpec((d, SIDE_W), lambda b, j: (0, GATE_W // SIDE_W), pipeline_mode=pl.Buffered(1)),
                  pl.BlockSpec((tm, RET_DK), lambda b, j: (j, 0)),
                  pl.BlockSpec((tm, RET_DK), lambda b, j: (j, 0))] + [_resident(c.shape) for c in consts],
        out_specs=[tok(RET_W), tok(RET_W), tok(RET_W), tok(POOL_W), tok(SSM_XBC),
                   tok(fox_w), pl.BlockSpec((1, fox_w, tm), lambda b, j: (b, 0, j)), tok(fox_w), tok(LANES),
                   pl.BlockSpec((1, LANES, tm), lambda b, j: (b, 0, j))],
        out_shape=[bf(RET_W), bf(RET_W), bf(RET_W), bf(POOL_W), bf(SSM_XBC),
                   bf(fox_w), jax.ShapeDtypeStruct((bsz, fox_w, seq), BF16), bf(fox_w),
                   jax.ShapeDtypeStruct((bsz, seq, LANES), F32),
                   jax.ShapeDtypeStruct((bsz, LANES, seq), F32)],
        scratch_shapes=[pltpu.VMEM((POOL_HALO + tm, POOL_W), F32),
                        pltpu.VMEM((CONV_HALO + tm, SSM_XBC), F32),
                        pltpu.VMEM((1, LANES), F32)],
        compiler_params=_params(2),
        name="mixer_proj",
    )(x, mod, gpre, w_in_b, w_side, cos_t, sin_t, *consts)


def _ret_kernel(q_ref, k_ref, v_ref, idec_ref, qdec_ref, kdec_ref, o_ref, state, *, chunk_decay):
    @pl.when(pl.program_id(1) == 0)
    def _():
        state[...] = jnp.zeros_like(state)

    tile = q_ref.shape[1]
    for c0 in range(0, tile, RET_CHUNK):
        rows = slice(c0, c0 + RET_CHUNK)
        for hd in range(RET_HEADS):
            ls = slice(hd * RET_DK, (hd + 1) * RET_DK)
            q = q_ref[0, rows, ls]
            k = k_ref[0, rows, ls]
            v = v_ref[0, rows, ls]
            st = state[hd]
            scores = _dot_nt(q, k) * idec_ref[hd]
            o = _dot(scores.astype(BF16), v) + _dot(q, st.astype(BF16)) * qdec_ref[hd]
            kd = (k.astype(F32) * kdec_ref[hd]).astype(BF16)
            state[hd] = chunk_decay[hd] * st + _dot_tn(kd, v)
            dev = o - jnp.mean(o, axis=-1, keepdims=True)
            var = jnp.mean(dev * dev, axis=-1, keepdims=True)
            o_ref[0, rows, ls] = (dev * lax.rsqrt(var + GN_EPS)).astype(BF16)


def _ret_tables():
    log_gamma = np.log1p(-np.exp2(-5.0 - np.arange(RET_HEADS, dtype=np.float64)))
    idx = np.arange(RET_CHUNK, dtype=np.float64)
    rel = idx[:, None] - idx[None, :]
    intra = np.where(rel >= 0, np.exp(log_gamma[:, None, None] * np.maximum(rel, 0.0)), 0.0)
    q_decay = np.exp(log_gamma[:, None] * (idx + 1.0))
    k_decay = np.exp(log_gamma[:, None] * (RET_CHUNK - 1.0 - idx))
    widen = lambda t: np.broadcast_to(t[:, :, None], (RET_HEADS, RET_CHUNK, RET_DK))
    chunk_decay = tuple(float(v) for v in np.exp(log_gamma * RET_CHUNK))
    return (jnp.asarray(intra, F32), jnp.asarray(widen(q_decay), F32),
            jnp.asarray(widen(k_decay), F32), chunk_decay)


def _ret_call(rq, rk, rv):
    bsz, seq, w = rq.shape
    idec, qdec, kdec, chunk_decay = _ret_tables()
    tok = pl.BlockSpec((1, RET_TILE, w), lambda b, j: (b, j, 0))
    return pl.pallas_call(
        functools.partial(_ret_kernel, chunk_decay=chunk_decay),
        grid=(bsz, seq // RET_TILE),
        in_specs=[tok, tok, tok, _resident(idec.shape), _resident(qdec.shape), _resident(kdec.shape)],
        out_specs=tok,
        out_shape=jax.ShapeDtypeStruct((bsz, seq, w), BF16),
        scratch_shapes=[pltpu.VMEM((RET_HEADS, RET_DK, RET_DV), F32)],
        compiler_params=_params(2),
        name="retention",
    )(rq, rk, rv, idec, qdec, kdec)


HEADS_PER_GROUP = SSM_HEADS // SSM_GROUPS
GROUP_W = HEADS_PER_GROUP * SSM_HEAD_DIM


def _ssd_kernel(xbc_ref, scal_ref, scalT_ref, ea_sel_ref, dt_sel_ref, dskip_ref, o_ref, state):
    @pl.when(pl.program_id(1) == 0)
    def _():
        state[...] = jnp.zeros_like(state)

    tile = xbc_ref.shape[1]
    cl = SSD_CHUNK
    lane = lax.broadcasted_iota(jnp.int32, (cl, LANES), 1)
    row_i = lax.broadcasted_iota(jnp.int32, (cl, cl), 0)
    col_i = lax.broadcasted_iota(jnp.int32, (cl, cl), 1)
    causal = col_i <= row_i
    low_half = lane < SSM_HEAD_DIM
    ea_sel = ea_sel_ref[...]
    dt_sel = dt_sel_ref[...]
    for c0 in range(0, tile, cl):
        rows = slice(c0, c0 + cl)
        sc = scal_ref[0, rows, :]
        sc2 = sc * LOG2_E
        in_acum = (lane >= SC_ACUM) & (lane < SC_CUMF)
        acum = jnp.where(in_acum, sc, 0.0)
        ea = jnp.exp(acum)
        dec_end = jnp.exp(acum[cl - 1:cl, :] - acum)
        dt = jnp.where(lane < SC_ACUM, sc, 0.0)
        ea_hi = ea.astype(BF16)
        ea_lo = (ea - ea_hi.astype(F32)).astype(BF16)
        ea_x = _dot(ea_hi, ea_sel) + _dot(ea_lo, ea_sel)
        w_x = _dot(dec_end.astype(BF16), ea_sel) * _dot(dt.astype(BF16), dt_sel)
        xs_b = xbc_ref[0, rows, 0:SSM_INNER]
        xs = xs_b.astype(F32)
        xw_b = (xs * w_x).astype(BF16)
        shared = []
        for g in range(SSM_GROUPS):
            gs = slice(g * GROUP_W, (g + 1) * GROUP_W)
            bm = xbc_ref[0, rows, SSM_INNER + g * SSM_STATE:SSM_INNER + (g + 1) * SSM_STATE]
            cm_lo = SSM_INNER + SSM_GROUPS * SSM_STATE + g * SSM_STATE
            cm = xbc_ref[0, rows, cm_lo:cm_lo + SSM_STATE]
            cb = _dot_nt(cm, bm)
            st = state[g]
            y_inter = _dot(cm, st.astype(BF16)) * ea_x[:, gs]
            state[g] = st * ea_x[cl - 1:cl, gs] + _dot_tn(bm, xw_b[:, gs])
            shared.append((cb, y_inter))
        for g in range(SSM_GROUPS):
            cb, y_inter = shared[g]
            for pair in range(HEADS_PER_GROUP // 2):
                mats = []
                for hh in (2 * pair, 2 * pair + 1):
                    hd = g * HEADS_PER_GROUP + hh
                    a_col = jnp.broadcast_to(sc2[:, SC_ACUM + hd:SC_ACUM + hd + 1], (cl, cl))
                    a_row = scalT_ref[0, SC_ACUM + hd:SC_ACUM + hd + 1, rows] * LOG2_E
                    dt_row = scalT_ref[0, SC_DT + hd:SC_DT + hd + 1, rows]
                    lmat = jnp.exp2(jnp.where(causal, a_col - (a_row - jnp.log2(dt_row)), NEG_BIG))
                    mats.append((lmat * cb).astype(BF16))
                lo = g * GROUP_W + pair * LANES
                x_pair = xs_b[:, lo:lo + LANES]
                zero = jnp.zeros_like(x_pair)
                rhs = jnp.concatenate([jnp.where(low_half, x_pair, zero),
                                       jnp.where(low_half, zero, x_pair)], axis=0)
                y_pair = _dot(jnp.concatenate(mats, axis=1), rhs)
                y_pair = y_pair + y_inter[:, pair * LANES:(pair + 1) * LANES]
                y_pair = y_pair + dskip_ref[:, lo:lo + LANES] * xs[:, lo:lo + LANES]
                o_ref[0, rows, lo:lo + LANES] = y_pair.astype(BF16)


def _ssd_tables():
    ea_sel = np.zeros((LANES, SSM_INNER), np.float32)
    dt_sel = np.zeros((LANES, SSM_INNER), np.float32)
    for hd in range(SSM_HEADS):
        ea_sel[SC_ACUM + hd, hd * SSM_HEAD_DIM:(hd + 1) * SSM_HEAD_DIM] = 1.0
        dt_sel[SC_DT + hd, hd * SSM_HEAD_DIM:(hd + 1) * SSM_HEAD_DIM] = 1.0
    return jnp.asarray(ea_sel, BF16), jnp.asarray(dt_sel, BF16)


def _ssd_call(xbc, scal, scal_t, dskip_x):
    bsz, seq, _ = xbc.shape
    ea_sel, dt_sel = _ssd_tables()
    return pl.pallas_call(
        _ssd_kernel,
        grid=(bsz, seq // SSD_TILE),
        in_specs=[pl.BlockSpec((1, SSD_TILE, SSM_XBC), lambda b, j: (b, j, 0)),
                  pl.BlockSpec((1, SSD_TILE, LANES), lambda b, j: (b, j, 0)),
                  pl.BlockSpec((1, LANES, SSD_TILE), lambda b, j: (b, 0, j)),
                  _resident(ea_sel.shape), _resident(dt_sel.shape), _resident(dskip_x.shape)],
        out_specs=pl.BlockSpec((1, SSD_TILE, SSM_INNER), lambda b, j: (b, j, 0)),
        out_shape=jax.ShapeDtypeStruct((bsz, seq, SSM_INNER), BF16),
        scratch_shapes=[pltpu.VMEM((SSM_GROUPS, SSM_STATE, GROUP_W), F32)],
        compiler_params=_params(2),
        name="ssd",
    )(xbc, scal, scal_t, ea_sel, dt_sel, dskip_x)


FOX_HALVES = 2
FOX_KV = 1024


def _fox_kernel(q_ref, k_ref, v_ref, o_ref, m_s, acc_s, s_a, s_b):
    tq, tk = FOX_TILE, FOX_KV
    th = tq // FOX_HALVES
    n_q = q_ref.shape[1] // tq
    jobs = [(i, t) for i in range(n_q) for t in range(-(-((i + 1) * tq) // tk))]
    bufs = (s_a, s_b)

    def first_row(i, t):
        return max(0, t * tk - i * tq) // th * th

    def scores(s_ref, i, t):
        r0 = first_row(i, t)
        s_ref[r0:, :] = _dot(q_ref[0, i * tq + r0:(i + 1) * tq, :],
                             k_ref[0, :, t * tk:(t + 1) * tk])

    def consume(s_ref, i, t):
        slot = i % 2
        for half in range(first_row(i, t) // th, FOX_HALVES):
            rows = slice(half * th, (half + 1) * th)
            row_lo = i * tq + half * th
            ncols = min(tk, row_lo + th - t * tk)
            s = s_ref[rows, :ncols]
            if t * tk + ncols - 1 > row_lo:
                row_i = row_lo + lax.broadcasted_iota(jnp.int32, (th, ncols), 0)
                col_i = t * tk + lax.broadcasted_iota(jnp.int32, (th, ncols), 1)
                s = jnp.where(col_i <= row_i, s, NEG_BIG)
            m_old = m_s[slot, rows, :]
            m_new = jnp.maximum(m_old, jnp.max(s, axis=-1, keepdims=True))
            p = jnp.concatenate([jnp.exp2(s[:, c0:c0 + LANES] - m_new)
                                 for c0 in range(0, ncols, LANES)], axis=1)
            alpha = jnp.exp2(m_old - m_new)
            pv = _dot(p.astype(BF16), v_ref[0, t * tk:t * tk + ncols, :])
            for c0 in range(0, FOX_SLAB, LANES):
                acc_s[slot, rows, c0:c0 + LANES] = (alpha * acc_s[slot, rows, c0:c0 + LANES]
                                                    + pv[:, c0:c0 + LANES])
            m_s[slot, rows, :] = m_new

    scores(bufs[0], *jobs[0])
    for n, (i, t) in enumerate(jobs):
        if t == 0:
            m_s[i % 2] = jnp.full((tq, LANES), NEG_BIG, F32)
            acc_s[i % 2] = jnp.zeros((tq, FOX_SLAB), F32)
        if n + 1 < len(jobs):
            scores(bufs[(n + 1) % 2], *jobs[n + 1])
        consume(bufs[n % 2], i, t)
        if n + 1 == len(jobs) or jobs[n + 1][0] != i:
            acc = acc_s[i % 2]
            o_ref[0, i * tq:(i + 1) * tq, :] = (acc[:, :FOX_HEAD_DIM]
                                                / acc[:, FOX_HEAD_DIM:FOX_HEAD_DIM + 1]).astype(BF16)


def _fox_call(fq, fk, fv):
    bsz, seq, _ = fq.shape
    tq = FOX_TILE
    return pl.pallas_call(
        _fox_kernel,
        grid=(bsz, FOX_HEADS),
        in_specs=[pl.BlockSpec((1, seq, FOX_SLAB), lambda b, h: (b, 0, h)),
                  pl.BlockSpec((1, FOX_SLAB, seq), lambda b, h: (b, h, 0)),
                  pl.BlockSpec((1, seq, FOX_SLAB), lambda b, h: (b, 0, h))],
        out_specs=pl.BlockSpec((1, seq, FOX_HEAD_DIM), lambda b, h: (b, 0, h)),
        out_shape=jax.ShapeDtypeStruct((bsz, seq, FOX_W), BF16),
        scratch_shapes=[pltpu.VMEM((2, tq, LANES), F32), pltpu.VMEM((2, tq, FOX_SLAB), F32),
                        pltpu.VMEM((tq, FOX_KV), F32), pltpu.VMEM((tq, FOX_KV), F32)],
        compiler_params=_params(2),
        name="fox_attention",
    )(fq, fk, fv)


MG_SZ_BLOCK = 2 * D_MODEL
assert IN_RG % RET_W == 0 and IN_SZ // MG_SZ_BLOCK == (IN_SZ + SSM_INNER - 1) // MG_SZ_BLOCK


def _merge_kernel(x_ref, mod_ref, gpre_ref, gpost_ref, wrg_ref, wsz_ref, wg_ref, ret_ref, yb_ref, ssd_ref, fox_ref,
                  ssmnorm_ref, wro_ref, wpo_ref, wso_ref, wfo_ref, wout_ref, o_ref):
    x = x_ref[...]
    hb = _modulated(x, mod_ref, gpre_ref, 1).astype(BF16)
    rg = _dot(hb, wrg_ref[0])
    y_a = (_silu(rg) * ret_ref[...].astype(F32)).astype(BF16)
    sz = _dot(hb, wsz_ref[0, :, IN_SZ % MG_SZ_BLOCK:IN_SZ % MG_SZ_BLOCK + SSM_INNER])
    y_c = _rms(ssd_ref[...].astype(F32) * _silu(sz), ssmnorm_ref[...]).astype(BF16)
    merged = None
    for br, (y, w_ref) in enumerate(((y_a, wro_ref), (yb_ref[...], wpo_ref),
                                     (y_c, wso_ref), (fox_ref[...], wfo_ref))):
        lo = br * D_MODEL
        gate = _sigmoid(_dot(hb, wg_ref[:, lo:lo + D_MODEL]))
        part = gate * _dot(y, w_ref[...])
        merged = part if merged is None else merged + part
    y_out = _dot(merged.astype(BF16), wout_ref[...])
    o_ref[...] = x + mod_ref[0, 5:6, :] * _rms(y_out, gpost_ref[1:2, :])


def _merge_call(x2d, mod, gpre, gpost, w_in_b, w_gate, layer, ret, yb, ssd, fox, ssmnorm, wro, wpo, wso, wfo, wout, seq):
    n, d = x2d.shape
    tm = TOK_TILE
    tiles_per_seq = seq // tm
    tok = lambda w: pl.BlockSpec((tm, w), lambda i: (i, 0))
    return pl.pallas_call(
        _merge_kernel,
        grid=(n // tm,),
        in_specs=[tok(d),
                  pl.BlockSpec((1, 3 * N_SUBLAYERS, d), lambda i: (i // tiles_per_seq, 0, 0)),
                  _resident(gpre.shape), _resident(gpost.shape),
                  pl.BlockSpec((1, d, RET_W), lambda i: (layer, 0, IN_RG // RET_W), pipeline_mode=pl.Buffered(1)),
                  pl.BlockSpec((1, d, MG_SZ_BLOCK), lambda i: (layer, 0, IN_SZ // MG_SZ_BLOCK),
                               pipeline_mode=pl.Buffered(1)),
                  pl.BlockSpec((d, GATE_W), lambda i: (0, 0), pipeline_mode=pl.Buffered(1)),
                  tok(RET_W), tok(POOL_W), tok(SSM_INNER), tok(FOX_W),
                  _resident(ssmnorm.shape), _resident(wro.shape), _resident(wpo.shape),
                  _resident(wso.shape), _resident(wfo.shape), _resident(wout.shape)],
        out_specs=tok(d),
        out_shape=jax.ShapeDtypeStruct((n, d), F32),
        compiler_params=_params(1),
        name="mixer_merge",
    )(x2d, mod, gpre, gpost, w_in_b, w_in_b, w_gate, ret.reshape(n, -1), yb.reshape(n, -1), ssd.reshape(n, -1),
      fox.reshape(n, -1), ssmnorm, wro, wpo, wso, wfo, wout)


def _rotary_tables(seq):
    half = RET_DK // 2
    inv = ROPE_BASE ** (-jnp.arange(half, dtype=F32) / half)
    ang = jnp.arange(seq, dtype=F32)[:, None] * inv[None, :]
    cos = jnp.cos(ang)
    sin = jnp.sin(ang)
    return jnp.concatenate([cos, cos], axis=-1), jnp.concatenate([-sin, sin], axis=-1)


def _place(vals, lane0):
    return jnp.zeros((1, LANES), F32).at[0, lane0:lane0 + vals.shape[0]].set(vals.astype(F32))


def kernel(x, c, w_ada, b_ada, norm_pre, norm_post, w_ffn_in, w_ffn_out, w_in, b_forget, pool_w, pool_scale, conv_w, conv_b, dt_bias, a_log, d_skip, ssm_norm, w_ret_out, w_pool_out, w_ssm_out, w_fox_out, w_out):
    bsz, seq, d = x.shape
    depth = w_ada.shape[0]
    n = bsz * seq
    mods = _ada_call(c, w_ada, b_ada)
    cos_t, sin_t = _rotary_tables(seq)
    tril_b = jnp.asarray(np.tril(np.ones((SSD_CHUNK, SSD_CHUNK), np.float32)), BF16)
    w_in_b = w_in.astype(BF16)
    w_ffn_in_b = w_ffn_in.astype(BF16)
    w_ffn_out_b = w_ffn_out.astype(BF16)
    for i in range(depth):
        mod = mods[i].reshape(bsz, 3 * N_SUBLAYERS, d)
        gpre, gpost = norm_pre[i], norm_post[i]
        x2d = _ffn_call(x.reshape(n, d), mod, gpre, gpost, w_ffn_in_b, w_ffn_out_b, i, 0, seq)

        w_regrouped = _regroup_call(w_in_b, i)
        sbias = _place(dt_bias[i], SC_DT) + _place(dt_bias[i], SC_ACUM) + _place(b_forget[i], SC_CUMF)
        alog = _place(a_log[i], SC_ACUM)
        dskip_x = jnp.repeat(d_skip[i].astype(F32), SSM_HEAD_DIM)[None, :]

        rq, rk, rv, yb, xbc, fq, fk, fv, scal, scal_t = _proj_call(
            x2d.reshape(bsz, seq, d), mod, gpre, w_in_b, w_regrouped, i, cos_t, sin_t, pool_w[i].astype(BF16),
            pool_scale[i][None, :], conv_w[i], conv_b[i][None, :], sbias, alog, tril_b)
        ret = _ret_call(rq, rk, rv)
        ssd = _ssd_call(xbc, scal, scal_t, dskip_x)
        fox = _fox_call(fq, fk, fv)
        x2d = _merge_call(x2d, mod, gpre, gpost, w_in_b, w_regrouped, i, ret, yb, ssd, fox, ssm_norm[i][None, :],
                          w_ret_out[i].astype(BF16), w_pool_out[i].astype(BF16),
                          w_ssm_out[i].astype(BF16), w_fox_out[i].astype(BF16),
                          w_out[i].astype(BF16), seq)
        x2d = _ffn_call(x2d, mod, gpre, gpost, w_ffn_in_b, w_ffn_out_b, i, 1, seq)
        x = x2d.reshape(bsz, seq, d)
    return x
```
